```python
import math
import jax, jax.numpy as jnp
from jax import lax
import numpy as np

D_MODEL = 2048
BATCH = 8
SEQ = 2048
DEPTH = 4

GRID_W = 64
MIX_WIDTH = D_MODEL
HALF = MIX_WIDTH // 2
HEAD_DIM = 128
NA_HEADS = HALF // HEAD_DIM
NA_WIN_ROWS_MAX = 8
NA_WIN_COLS = 16
HG_HEADS = HALF // HEAD_DIM
HG_EXPAND = 128
HG_FDIM = HG_HEADS * HG_EXPAND
HG_VDIM = HALF // HG_HEADS
HG_CHUNK = 64
POOL_WINDOWS = (2, 4, 8, 16)
POOL_GROUPS = len(POOL_WINDOWS)
POOL_GROUP_DIM = HALF // POOL_GROUPS
GQA_Q_HEADS = HALF // HEAD_DIM
GQA_KV_HEADS = 2
KV_W = GQA_KV_HEADS * HEAD_DIM
Q_BLOCK = 128
ROPE_THETA = 10000.0
FFN_HIDDEN = -(-8 * D_MODEL // (3 * 256)) * 256
AB_IN = 3 * HALF + 2 * HG_FDIM + HG_FDIM + 2 * HALF
CD_IN = HALF + HALF + 2 * KV_W
N_AB = (DEPTH + 1) // 2
N_CD = DEPTH // 2
DN_ALPHA = (2 * DEPTH) ** 0.25
DN_BETA = (8 * DEPTH) ** -0.25
LN_EPS = 1e-5
RMS_EPS = 1e-6

kernel_name = "hybrid_natten_hgrn2_pool_gqa_encoder"


def layer_norm(x, g, b):
    xf = x.astype(jnp.float32)
    mu = jnp.mean(xf, axis=-1, keepdims=True)
    var = jnp.mean(jnp.square(xf - mu), axis=-1, keepdims=True)
    return ((xf - mu) * lax.rsqrt(var + LN_EPS)).astype(x.dtype) * g + b


def rms_norm(x, g):
    xf = x.astype(jnp.float32)
    ms = jnp.mean(jnp.square(xf), axis=-1, keepdims=True)
    return (xf * lax.rsqrt(ms + RMS_EPS)).astype(x.dtype) * g


def neighbourhood_attention(q, k, v, rpb):
    b, t, h, dh = q.shape
    rows = t // GRID_W
    kh = min(NA_WIN_ROWS_MAX, rows)
    qg = q.reshape(b, rows, GRID_W, h, dh) * (dh ** -0.5)
    kg = k.reshape(b, rows, GRID_W, h, dh)
    vg = v.reshape(b, rows, GRID_W, h, dh)
    col = jnp.arange(GRID_W)
    col_start = jnp.clip(col - NA_WIN_COLS // 2, 0, GRID_W - NA_WIN_COLS)
    col_mask = (col[None, :] >= col_start[:, None]) & (col[None, :] < col_start[:, None] + NA_WIN_COLS)
    col_idx = jnp.clip(col[None, :] - col[:, None] + NA_WIN_COLS - 1, 0, 2 * NA_WIN_COLS - 2)

    def row_step(r):
        r0 = jnp.clip(r - kh // 2, 0, rows - kh)
        k_band = lax.dynamic_slice_in_dim(kg, r0, kh, axis=1)
        v_band = lax.dynamic_slice_in_dim(vg, r0, kh, axis=1)
        q_row = lax.dynamic_index_in_dim(qg, r, axis=1, keepdims=False)
        s = jnp.einsum('bqhd,bjkhd->bhqjk', q_row, k_band).astype(jnp.float32)
        row_idx = r0 + jnp.arange(kh) - r + NA_WIN_ROWS_MAX - 1
        bias = rpb[:, row_idx[None, :, None], col_idx[:, None, :]]
        s = s + bias[None].astype(jnp.float32)
        s = jnp.where(col_mask[None, None, :, None, :], s, -jnp.inf)
        p = jax.nn.softmax(s.reshape(b, h, GRID_W, kh * GRID_W), axis=-1)
        p = p.astype(v.dtype).reshape(b, h, GRID_W, kh, GRID_W)
        return jnp.einsum('bhqjk,bjkhd->bqhd', p, v_band)

    out = lax.map(row_step, jnp.arange(rows))
    return out.transpose(1, 0, 2, 3, 4).reshape(b, t, h * dh)


def chunked_gated_recurrence(q, k, v, log_f):
    b, h, t, dk = q.shape
    dv = v.shape[-1]
    c = HG_CHUNK
    n = t // c

    def to_chunks(a):
        return a.astype(jnp.float32).reshape(b, h, n, c, a.shape[-1]).transpose(2, 0, 1, 3, 4)

    qc, kc, vc, gc = (to_chunks(a) for a in (q, k, v, log_f))
    lower = jnp.tril(jnp.ones((c, c), dtype=bool))

    def step(S, inp):
        qi, ki, vi, gi = inp
        G = jnp.cumsum(gi, axis=2)
        diff = G[:, :, :, None, :] - G[:, :, None, :, :]
        decay = jnp.exp(jnp.where(lower[None, None, :, :, None], diff, -jnp.inf))
        A = jnp.einsum('bhtd,bhtsd,bhsd->bhts', qi, decay, ki)
        o = jnp.einsum('bhts,bhsv->bhtv', A, vi) + jnp.einsum('bhtd,bhdv->bhtv', qi * jnp.exp(G), S)
        G_last = G[:, :, -1:, :]
        S = jnp.exp(G_last[:, :, 0, :])[..., None] * S + jnp.einsum('bhsd,bhsv->bhdv', ki * jnp.exp(G_last - G), vi)
        return S, o

    S0 = jnp.zeros((b, h, dk, dv), jnp.float32)
    _, o = lax.scan(step, S0, (qc, kc, vc, gc))
    return o.transpose(1, 2, 0, 3, 4).reshape(b, h, t, dv)


def hgrn2_mixer(q_in, ff_in, fb_in, i_in, g_in, lb, norm_w):
    b, t, _ = q_in.shape

    def heads(z, d):
        return z.reshape(b, t, HG_HEADS, d).transpose(0, 2, 1, 3)

    q = heads(jax.nn.silu(q_in), HG_EXPAND)
    v = heads(i_in, HG_VDIM)

    def direction(z, reverse):
        zf = z.astype(jnp.float32)
        f = lb + (1.0 - lb) * jax.nn.sigmoid(zf)
        k = (1.0 - lb) * jax.nn.sigmoid(-zf)
        args = (q, heads(k, HG_EXPAND), v, heads(jnp.log(f), HG_EXPAND))
        if reverse:
            args = tuple(jnp.flip(a, axis=2) for a in args)
            return jnp.flip(chunked_gated_recurrence(*args), axis=2)
        return chunked_gated_recurrence(*args)

    o = direction(ff_in, False) + direction(fb_in, True)
    o = rms_norm(o.transpose(0, 2, 1, 3), norm_w.reshape(HG_HEADS, HG_VDIM))
    return o.reshape(b, t, HALF).astype(g_in.dtype) * jax.nn.silu(g_in)


def multiscale_pool(x, w_groups, scale):
    b, t, _ = x.shape
    xf = x.astype(jnp.float32)
    cs = jnp.concatenate([jnp.zeros((b, 1, HALF), jnp.float32), jnp.cumsum(xf, axis=1)], axis=1)
    pos = jnp.arange(t)
    outs = []
    for gi, w in enumerate(POOL_WINDOWS):
        sl = slice(gi * POOL_GROUP_DIM, (gi + 1) * POOL_GROUP_DIM)
        lo = jnp.clip(pos - w // 2, 0, t)
        hi = jnp.clip(pos + w // 2, 0, t)
        seg = cs[:, :, sl]
        mean = (seg[:, hi] - seg[:, lo]) / (hi - lo).astype(jnp.float32)[None, :, None]
        outs.append(mean - xf[:, :, sl])
    pooled = jnp.stack(outs, axis=2).astype(x.dtype)
    y = jnp.einsum('btgc,gcd->btgd', pooled, w_groups).reshape(b, t, HALF)
    return y * scale


def axial_rope_tables(t):
    pos = jnp.arange(t)
    row = (pos // GRID_W).astype(jnp.float32)
    col = (pos % GRID_W).astype(jnp.float32)
    n_freq = HEAD_DIM // 4
    inv = ROPE_THETA ** (-jnp.arange(n_freq, dtype=jnp.float32) / n_freq)
    ang = jnp.concatenate([row[:, None] * inv, col[:, None] * inv], axis=-1)
    return jnp.cos(ang), jnp.sin(ang)


def apply_axial_rope(x, cos, sin):
    xr = x.astype(jnp.float32).reshape(*x.shape[:-1], HEAD_DIM // 2, 2)
    x0, x1 = xr[..., 0], xr[..., 1]
    c = cos[None, :, None, :]
    s = sin[None, :, None, :]
    out = jnp.stack([x0 * c - x1 * s, x0 * s + x1 * c], axis=-1).reshape(x.shape)
    return out.astype(x.dtype)


def gqa_attention(q, k, v):
    b, t, hq, dh = q.shape
    hkv = k.shape[2]
    g = hq // hkv
    qb = (q * (dh ** -0.5)).reshape(b, t // Q_BLOCK, Q_BLOCK, hkv, g, dh).transpose(1, 0, 2, 3, 4, 5)

    def block(qi):
        s = jnp.einsum('bqkgd,bskd->bkgqs', qi, k).astype(jnp.float32)
        p = jax.nn.softmax(s, axis=-1).astype(v.dtype)
        return jnp.einsum('bkgqs,bskd->bqkgd', p, v)

    o = lax.map(block, qb)
    return o.transpose(1, 0, 2, 3, 4, 5).reshape(b, t, hq * dh)


def ab_mixer(x, w_in, w_out, rpb, lb, hg_norm_w):
    b, t, _ = x.shape
    proj = x @ w_in
    a_q, a_k, a_v, h_q, h_ff, h_fb, h_i, h_g = jnp.split(proj, 8, axis=-1)

    def heads(z):
        return z.reshape(b, t, NA_HEADS, HEAD_DIM)

    y_a = neighbourhood_attention(heads(a_q), heads(a_k), heads(a_v), rpb)
    y_b = hgrn2_mixer(h_q, h_ff, h_fb, h_i, h_g, lb, hg_norm_w)
    return jnp.concatenate([y_a, y_b], axis=-1) @ w_out


def cd_mixer(x, w_in, w_out, pool_w, pool_scale, q_norm, k_norm, cos, sin):
    b, t, _ = x.shape
    proj = x @ w_in
    c_x, d_q, d_k, d_v = jnp.split(proj, [HALF, 2 * HALF, 2 * HALF + KV_W], axis=-1)
    y_c = multiscale_pool(c_x, pool_w, pool_scale)
    q = apply_axial_rope(rms_norm(d_q.reshape(b, t, GQA_Q_HEADS, HEAD_DIM), q_norm), cos, sin)
    k = apply_axial_rope(rms_norm(d_k.reshape(b, t, GQA_KV_HEADS, HEAD_DIM), k_norm), cos, sin)
    v = d_v.reshape(b, t, GQA_KV_HEADS, HEAD_DIM)
    y_d = gqa_attention(q, k, v)
    return jnp.concatenate([y_c, y_d], axis=-1) @ w_out


def swiglu(x, w_gate, w_up, w_down):
    return (jax.nn.silu(x @ w_gate) * (x @ w_up)) @ w_down


def setup_inputs(seed: int = 0) -> dict:
    key = jax.random.key(seed)
    ks = jax.random.split(key, 19)
    n = jax.random.normal
    f32 = jnp.float32
    return {
        "x": n(ks[0], (BATCH, SEQ, D_MODEL), f32),
        "ab_w_in": n(ks[1], (N_AB, D_MODEL, AB_IN), f32) * D_MODEL ** -0.5,
        "ab_w_out": n(ks[2], (N_AB, MIX_WIDTH, D_MODEL), f32) * (MIX_WIDTH ** -0.5 * DN_BETA),
        "na_rpb": 0.1 * n(ks[3], (N_AB, NA_HEADS, 2 * NA_WIN_ROWS_MAX - 1, 2 * NA_WIN_COLS - 1), f32),
        "hg_lb_logits": 0.5 * n(ks[4], (N_AB, HG_FDIM), f32),
        "hg_norm_w": 1.0 + 0.1 * n(ks[5], (N_AB, HALF), f32),
        "cd_w_in": n(ks[6], (N_CD, D_MODEL, CD_IN), f32) * D_MODEL ** -0.5,
        "cd_w_out": n(ks[7], (N_CD, MIX_WIDTH, D_MODEL), f32) * (MIX_WIDTH ** -0.5 * DN_BETA),
        "pool_w": n(ks[8], (N_CD, POOL_GROUPS, POOL_GROUP_DIM, POOL_GROUP_DIM), f32) * POOL_GROUP_DIM ** -0.5,
        "pool_scale": 1.0 + 0.1 * n(ks[9], (N_CD, HALF), f32),
        "d_q_norm": 1.0 + 0.1 * n(ks[10], (N_CD, HEAD_DIM), f32),
        "d_k_norm": 1.0 + 0.1 * n(ks[11], (N_CD, HEAD_DIM), f32),
        "ln_mix_g": 1.0 + 0.1 * n(ks[12], (DEPTH, D_MODEL), f32),
        "ln_mix_b": 0.02 * n(ks[13], (DEPTH, D_MODEL), f32),
        "ffn_w_gate": n(ks[14], (DEPTH, D_MODEL, FFN_HIDDEN), f32) * D_MODEL ** -0.5,
        "ffn_w_up": n(ks[15], (DEPTH, D_MODEL, FFN_HIDDEN), f32) * D_MODEL ** -0.5,
        "ffn_w_down": n(ks[16], (DEPTH, FFN_HIDDEN, D_MODEL), f32) * (FFN_HIDDEN ** -0.5 * DN_BETA),
        "ln_ffn_g": 1.0 + 0.1 * n(ks[17], (DEPTH, D_MODEL), f32),
        "ln_ffn_b": 0.02 * n(ks[18], (DEPTH, D_MODEL), f32),
    }


def reference(x, ab_w_in, ab_w_out, na_rpb, hg_lb_logits, hg_norm_w, cd_w_in, cd_w_out,
              pool_w, pool_scale, d_q_norm, d_k_norm, ln_mix_g, ln_mix_b,
              ffn_w_gate, ffn_w_up, ffn_w_down, ln_ffn_g, ln_ffn_b):
    t = x.shape[1]
    cos, sin = axial_rope_tables(t)
    lb_p = jax.nn.softmax(hg_lb_logits.astype(jnp.float32), axis=0)
    lower_bounds = jnp.cumsum(lb_p, axis=0) - lb_p[0]
    for layer in range(DEPTH):
        j = layer // 2
        if layer % 2 == 0:
            y = ab_mixer(x, ab_w_in[j], ab_w_out[j], na_rpb[j], lower_bounds[j], hg_norm_w[j])
        else:
            y = cd_mixer(x, cd_w_in[j], cd_w_out[j], pool_w[j], pool_scale[j],
                         d_q_norm[j], d_k_norm[j], cos, sin)
        x = layer_norm(DN_ALPHA * x + y, ln_mix_g[layer], ln_mix_b[layer])
        f = swiglu(x, ffn_w_gate[layer], ffn_w_up[layer], ffn_w_down[layer])
        x = layer_norm(DN_ALPHA * x + f, ln_ffn_g[layer], ln_ffn_b[layer])
    return x
```

```python
import functools

import jax
import jax.numpy as jnp
from jax import lax
from jax.experimental import pallas as pl
from jax.experimental.pallas import tpu as pltpu

D_MODEL = 2048
DEPTH = 4
GRID_W = 64
HALF = 1024
HEAD_DIM = 128
NA_HEADS = 8
NA_WIN_ROWS = 8
NA_WIN_COLS = 16
HG_HEADS = 8
HG_CHUNK = 64
POOL_WINDOWS = (2, 4, 8, 16)
POOL_GROUP_DIM = 256
GQA_Q_HEADS = 8
GQA_KV_HEADS = 2
KV_W = GQA_KV_HEADS * HEAD_DIM
ROPE_THETA = 10000.0
FFN_HIDDEN = 5632
DN_ALPHA = (2 * DEPTH) ** 0.25
LN_EPS = 1e-5
RMS_EPS = 1e-6
NEG_BIG = -1e30

V7X_VMEM_LIMIT_BYTES = 56 * 1024 * 1024

F32 = jnp.float32
BF16 = jnp.bfloat16


def _params(semantics):
    return pltpu.CompilerParams(dimension_semantics=semantics,
                                vmem_limit_bytes=V7X_VMEM_LIMIT_BYTES)


def _sigmoid(x):
    return 1.0 / (1.0 + jnp.exp(-x))


def _silu(x):
    return x * _sigmoid(x)


def _dot(a, b):
    return jnp.dot(a, b, preferred_element_type=F32)


def _dot_nt(a, b):
    return lax.dot_general(a, b, (((1,), (1,)), ((), ())), preferred_element_type=F32)


def _dot_tn(a, b):
    return lax.dot_general(a, b, (((0,), (0,)), ((), ())), preferred_element_type=F32)


def _matmul_kernel(a_ref, w_ref, o_ref):
    o_ref[...] = _dot(a_ref[...], w_ref[...]).astype(o_ref.dtype)


def _matmul(a, w, out_dtype, tm, tn, name):
    m, k = a.shape
    n = w.shape[1]
    return pl.pallas_call(
        _matmul_kernel,
        grid=(n // tn, m // tm),
        in_specs=[pl.BlockSpec((tm, k), lambda j, i: (i, 0)),
                  pl.BlockSpec((k, tn), lambda j, i: (0, j))],
        out_specs=pl.BlockSpec((tm, tn), lambda j, i: (i, j)),
        out_shape=jax.ShapeDtypeStruct((m, n), out_dtype),
        compiler_params=_params(("parallel", "parallel")),
        name=name,
    )(a, w)


def _res_ln(acc, x, g, b):
    z = DN_ALPHA * x + acc
    mu = jnp.mean(z, axis=-1, keepdims=True)
    zc = z - mu
    var = jnp.mean(zc * zc, axis=-1, keepdims=True)
    return zc * lax.rsqrt(var + LN_EPS) * g + b


def _mix_out_kernel(y1_ref, y2_ref, w_ref, x_ref, g_ref, b_ref, of_ref, ob_ref):
    half = y1_ref.shape[1]
    acc = _dot(y1_ref[...], w_ref[:half, :]) + _dot(y2_ref[...], w_ref[half:, :])
    out = _res_ln(acc, x_ref[...], g_ref[...], b_ref[...])
    of_ref[...] = out
    ob_ref[...] = out.astype(BF16)


def _mix_out(y1, y2, w, x, g, b, tm):
    m, half = y1.shape
    d = w.shape[1]
    row = lambda i: (i, 0)
    fixed = lambda i: (0, 0)
    return pl.pallas_call(
        _mix_out_kernel,
        grid=(m // tm,),
        in_specs=[pl.BlockSpec((tm, half), row), pl.BlockSpec((tm, half), row),
                  pl.BlockSpec((2 * half, d), fixed), pl.BlockSpec((tm, d), row),
                  pl.BlockSpec((1, d), fixed), pl.BlockSpec((1, d), fixed)],
        out_specs=[pl.BlockSpec((tm, d), row), pl.BlockSpec((tm, d), row)],
        out_shape=[jax.ShapeDtypeStruct((m, d), F32), jax.ShapeDtypeStruct((m, d), BF16)],
        compiler_params=_params(("parallel",)),
        name="mix_out_ln",
    )(y1, y2, w, x, g.reshape(1, d), b.reshape(1, d))


def _ffn_up_kernel(x_ref, wg_ref, wu_ref, h_ref):
    x = x_ref[...]
    gate = _dot(x, wg_ref[...])
    up = _dot(x, wu_ref[...])
    h_ref[...] = (_silu(gate) * up).astype(h_ref.dtype)


def _ffn_up(xb, wg, wu, tm, tn):
    m, k = xb.shape
    n = wg.shape[1]
    return pl.pallas_call(
        _ffn_up_kernel,
        grid=(n // tn, m // tm),
        in_specs=[pl.BlockSpec((tm, k), lambda j, i: (i, 0)),
                  pl.BlockSpec((k, tn), lambda j, i: (0, j)),
                  pl.BlockSpec((k, tn), lambda j, i: (0, j))],
        out_specs=pl.BlockSpec((tm, tn), lambda j, i: (i, j)),
        out_shape=jax.ShapeDtypeStruct((m, n), BF16),
        compiler_params=_params(("parallel", "parallel")),
        name="ffn_up",
    )(xb, wg, wu)


def _ffn_down_kernel(h_ref, w_ref, x_ref, g_ref, b_ref, of_ref, ob_ref, acc_ref):
    k = pl.program_id(1)

    @pl.when(k == 0)
    def _():
        acc_ref[...] = jnp.zeros_like(acc_ref)

    acc_ref[...] += _dot(h_ref[...], w_ref[...])

    @pl.when(k == pl.num_programs(1) - 1)
    def _():
        out = _res_ln(acc_ref[...], x_ref[...], g_ref[...], b_ref[...])
        of_ref[...] = out
        ob_ref[...] = out.astype(BF16)


def _ffn_down(h, w, x, g, b, tm, tk):
    m, kk = h.shape
    d = w.shape[1]
    return pl.pallas_call(
        _ffn_down_kernel,
        grid=(m // tm, kk // tk),
        in_specs=[pl.BlockSpec((tm, tk), lambda i, k: (i, k)),
                  pl.BlockSpec((tk, d), lambda i, k: (k, 0)),
                  pl.BlockSpec((tm, d), lambda i, k: (i, 0)),
                  pl.BlockSpec((1, d), lambda i, k: (0, 0)),
                  pl.BlockSpec((1, d), lambda i, k: (0, 0))],
        out_specs=[pl.BlockSpec((tm, d), lambda i, k: (i, 0)),
                   pl.BlockSpec((tm, d), lambda i, k: (i, 0))],
        out_shape=[jax.ShapeDtypeStruct((m, d), F32), jax.ShapeDtypeStruct((m, d), BF16)],
        scratch_shapes=[pltpu.VMEM((tm, d), F32)],
        compiler_params=_params(("parallel", "arbitrary")),
        name="ffn_down_ln",
    )(h, w, x, g.reshape(1, d), b.reshape(1, d))


N_BAND = NA_WIN_ROWS
RPB_ROWS = 2 * NA_WIN_ROWS - 1
RPB_COLS = 2 * NA_WIN_COLS - 1


def _na_band_kernel(rpb_ref, o_ref):
    h = pl.program_id(0)
    base = h * (RPB_ROWS * RPB_COLS)
    shape = (GRID_W, 2 * GRID_W)
    q = lax.broadcasted_iota(jnp.int32, shape, 0)
    lane = lax.broadcasted_iota(jnp.int32, shape, 1)
    kc = lane & (GRID_W - 1)
    second = lane >> 6
    code = kc - q + (NA_WIN_COLS - 1) + RPB_COLS * second
    start = jnp.clip(q - NA_WIN_COLS // 2, 0, GRID_W - NA_WIN_COLS)
    valid = (kc >= start) & (kc < start + NA_WIN_COLS)
    pairs = []
    for ri in range(RPB_ROWS - 1):
        acc = jnp.full(shape, NEG_BIG, F32)
        for i in range(2 * RPB_COLS):
            acc = jnp.where(code == i, rpb_ref[base + ri * RPB_COLS + i], acc)
        pairs.append(jnp.where(valid, acc, NEG_BIG))
    for r in range(N_BAND):
        for mm in range(NA_WIN_ROWS // 2):
            o_ref[0, r, :, mm * 2 * GRID_W:(mm + 1) * 2 * GRID_W] = pairs[r + 2 * mm]


def _na_band(rpb):
    return pl.pallas_call(
        _na_band_kernel,
        grid=(NA_HEADS,),
        in_specs=[pl.BlockSpec(memory_space=pltpu.SMEM)],
        out_specs=pl.BlockSpec((1, N_BAND, GRID_W, NA_WIN_ROWS * GRID_W), lambda h: (h, 0, 0, 0)),
        out_shape=jax.ShapeDtypeStruct((NA_HEADS, N_BAND, GRID_W, NA_WIN_ROWS * GRID_W), F32),
        compiler_params=_params(("arbitrary",)),
        name="na_band",
    )(rpb.reshape(-1))


def _na_kernel(q_ref, k_ref, v_ref, band_ref, o_ref, kb_ref, vb_ref):
    t = q_ref.shape[1]
    rows = t // GRID_W
    kh = NA_WIN_ROWS
    band_len = kh * GRID_W
    kb_ref[...] = k_ref[0].astype(BF16)
    vb_ref[...] = v_ref[0].astype(BF16)
    scale = HEAD_DIM ** -0.5

    def row_step(r, carry):
        r0 = jnp.clip(r - kh // 2, 0, rows - kh)
        band_idx = r0 - r + (kh - 1)
        qoff = pl.multiple_of(r * GRID_W, GRID_W)
        koff = pl.multiple_of(r0 * GRID_W, GRID_W)
        q = (q_ref[0, pl.ds(qoff, GRID_W), :] * scale).astype(BF16)
        s = _dot_nt(q, kb_ref[pl.ds(koff, band_len), :]) + band_ref[0, band_idx]
        m = jnp.max(s, axis=-1, keepdims=True)
        p = jnp.exp(s - m)
        l = jnp.sum(p, axis=-1, keepdims=True)
        o = _dot(p.astype(BF16), vb_ref[pl.ds(koff, band_len), :]) / l
        o_ref[0, pl.ds(qoff, GRID_W), :] = o.astype(o_ref.dtype)
        return carry

    lax.fori_loop(0, rows, row_step, 0)


def _na_attention(proj, band):
    b, t, _ = proj.shape
    nh = NA_HEADS
    blk = (1, t, HEAD_DIM)
    return pl.pallas_call(
        _na_kernel,
        grid=(nh, b),
        in_specs=[pl.BlockSpec(blk, lambda h, i: (i, 0, h)),
                  pl.BlockSpec(blk, lambda h, i: (i, 0, nh + h)),
                  pl.BlockSpec(blk, lambda h, i: (i, 0, 2 * nh + h)),
                  pl.BlockSpec((1, N_BAND, GRID_W, NA_WIN_ROWS * GRID_W), lambda h, i: (h, 0, 0, 0))],
        out_specs=pl.BlockSpec(blk, lambda h, i: (i, 0, h)),
        out_shape=jax.ShapeDtypeStruct((b, t, HALF), BF16),
        scratch_shapes=[pltpu.VMEM((t, HEAD_DIM), BF16), pltpu.VMEM((t, HEAD_DIM), BF16)],
        compiler_params=_params(("parallel", "parallel")),
        name="na_attention",
    )(proj, proj, proj, band)


def _split3(x):
    hi = x.astype(BF16)
    r1 = x - hi.astype(F32)
    mid = r1.astype(BF16)
    lo = (r1 - mid.astype(F32)).astype(BF16)
    return hi, mid, lo


def _pivot_rows(gc, m, rev):
    c = gc.shape[0]
    blk = 2 * m
    if blk >= 8:
        piv = m if rev else m - 1
        parts = []
        for s in range(0, c, blk):
            parts.append(jnp.broadcast_to(gc[s + piv:s + piv + 1, :], (blk, gc.shape[1])))
        return parts[0] if len(parts) == 1 else jnp.concatenate(parts, axis=0)
    pos = lax.broadcasted_iota(jnp.int32, gc.shape, 0) & (blk - 1)
    piv = m if rev else m - 1
    out = gc
    for p in range(blk):
        if p == piv:
            continue
        shift = (p - piv) % c
        out = jnp.where(pos == p, pltpu.roll(gc, shift, 0), out)
    return out


def _hgrn_chunk(z, q, v, st, lb, tri, rev):
    c = z.shape[0]
    f = lb + (1.0 - lb) * _sigmoid(z)
    k = (1.0 - lb) * _sigmoid(-z)
    g = jnp.log(f)
    hi, mid, lo = _split3(g)
    gc = _dot(tri, hi) + _dot(tri, mid) + _dot(tri, lo)
    g_tot = gc[0:1, :] if rev else gc[c - 1:c, :]
    row = lax.broadcasted_iota(jnp.int32, (c, c), 0)
    col = lax.broadcasted_iota(jnp.int32, (c, c), 1)
    pos = lax.broadcasted_iota(jnp.int32, z.shape, 0)

    a = jnp.where(row == col, jnp.sum(q * k, axis=-1, keepdims=True), 0.0)
    m = c // 2
    while m >= 1:
        piv = _pivot_rows(gc, m, rev)
        is_q = ((pos & m) == 0) if rev else ((pos & m) != 0)
        d = jnp.where(is_q, gc - piv, piv - gc)
        xl = (jnp.where(is_q, q, k) * jnp.exp(d)).astype(BF16)
        pl_ = _dot_nt(xl, xl)
        q_side = ((row & m) == 0) if rev else ((row & m) != 0)
        same_blk = (row & ~(2 * m - 1)) == (col & ~(2 * m - 1))
        k_side = ((col & m) != 0) if rev else ((col & m) == 0)
        a = jnp.where(q_side & same_blk & k_side, pl_, a)
        m //= 2

    vb = v.astype(BF16)
    o = _dot(a.astype(BF16), vb)
    o = o + _dot_nt((q * jnp.exp(gc)).astype(BF16), st.astype(BF16))
    kd = (k * jnp.exp(g_tot - gc)).astype(BF16)
    st_new = st * jnp.exp(g_tot) + _dot_tn(vb, kd)
    return o, st_new


def _hgrn_kernel(q_ref, ff_ref, fb_ref, i_ref, g_ref, lbl_ref, nw_ref, o_ref, of_ref, ob_ref,
                 *, layer):
    t = q_ref.shape[1]
    c = HG_CHUNK
    n = t // c
    dk = q_ref.shape[2]
    dv = i_ref.shape[2]

    lg = lbl_ref[...]
    e = jnp.exp(lg - jnp.max(lg, axis=0, keepdims=True))
    p = e / jnp.sum(e, axis=0, keepdims=True)
    lb = jnp.sum(p[:layer + 1, :], axis=0, keepdims=True) - p[0:1, :]

    row = lax.broadcasted_iota(jnp.int32, (c, c), 0)
    col = lax.broadcasted_iota(jnp.int32, (c, c), 1)
    tril = jnp.where(col <= row, 1.0, 0.0).astype(BF16)
    triu = jnp.where(col >= row, 1.0, 0.0).astype(BF16)

    def body(i, carry):
        st_f, st_b = carry
        off_f = pl.multiple_of(i * c, c)
        off_b = pl.multiple_of((n - 1 - i) * c, c)
        o_f, st_f = _hgrn_chunk(ff_ref[0, pl.ds(off_f, c), :], _silu(q_ref[0, pl.ds(off_f, c), :]),
                                i_ref[0, pl.ds(off_f, c), :], st_f, lb, tril, False)
        o_b, st_b = _hgrn_chunk(fb_ref[0, pl.ds(off_b, c), :], _silu(q_ref[0, pl.ds(off_b, c), :]),
                                i_ref[0, pl.ds(off_b, c), :], st_b, lb, triu, True)
        of_ref[pl.ds(off_f, c), :] = o_f
        ob_ref[pl.ds(off_b, c), :] = o_b
        return st_f, st_b

    zero = jnp.zeros((dv, dk), F32)
    lax.fori_loop(0, n, body, (zero, zero))

    nw = nw_ref[...]
    rows = 256

    def finish(i, carry):
        off = pl.multiple_of(i * rows, rows)
        o = of_ref[pl.ds(off, rows), :] + ob_ref[pl.ds(off, rows), :]
        ms = jnp.mean(o * o, axis=-1, keepdims=True)
        y = o * lax.rsqrt(ms + RMS_EPS) * nw
        o_ref[0, pl.ds(off, rows), :] = (y * _silu(g_ref[0, pl.ds(off, rows), :])).astype(o_ref.dtype)
        return carry

    lax.fori_loop(0, t // rows, finish, 0)


def _hgrn(proj, lb_logits, norm_w, layer):
    b, t, _ = proj.shape
    nh = HG_HEADS
    n_layers = lb_logits.shape[0]
    blk = (1, t, HEAD_DIM)
    base = 3 * HALF // HEAD_DIM

    def col(k):
        return lambda i, h: (i, 0, base + k * nh + h)

    return pl.pallas_call(
        functools.partial(_hgrn_kernel, layer=layer),
        grid=(b, nh),
        in_specs=[pl.BlockSpec(blk, col(0)), pl.BlockSpec(blk, col(1)), pl.BlockSpec(blk, col(2)),
                  pl.BlockSpec(blk, col(3)), pl.BlockSpec(blk, col(4)),
                  pl.BlockSpec((n_layers, HEAD_DIM), lambda i, h: (0, h)),
                  pl.BlockSpec((1, HEAD_DIM), lambda i, h: (0, h))],
        out_specs=pl.BlockSpec(blk, lambda i, h: (i, 0, h)),
        out_shape=jax.ShapeDtypeStruct((b, t, HALF), BF16),
        scratch_shapes=[pltpu.VMEM((t, HEAD_DIM), F32), pltpu.VMEM((t, HEAD_DIM), F32)],
        compiler_params=_params(("parallel", "parallel")),
        name="hgrn2",
    )(proj, proj, proj, proj, proj, lb_logits, norm_w.reshape(1, HALF))


POOL_PAD = 8


def _pool_kernel(x_ref, w_ref, s_ref, o_ref, xp_ref):
    t = x_ref.shape[1]
    width = x_ref.shape[2]
    rows = 256
    xp_ref[0:POOL_PAD, :] = jnp.zeros((POOL_PAD, width), F32)
    xp_ref[POOL_PAD + t:, :] = jnp.zeros((POOL_PAD, width), F32)
    xp_ref[POOL_PAD:POOL_PAD + t, :] = x_ref[0]
    for ci in range(t // rows):
        t0 = ci * rows
        pos = t0 + lax.broadcasted_iota(jnp.int32, (rows, POOL_GROUP_DIM), 0)
        for gi, win in enumerate(POOL_WINDOWS):
            hw = win // 2
            c0 = gi * POOL_GROUP_DIM
            c1 = c0 + POOL_GROUP_DIM
            tot = None
            for o in range(-hw, hw):
                piece = xp_ref[POOL_PAD + t0 + o:POOL_PAD + t0 + o + rows, c0:c1]
                tot = piece if tot is None else tot + piece
            cnt = (jnp.minimum(pos + hw, t) - jnp.maximum(pos - hw, 0)).astype(F32)
            pooled = tot / cnt - x_ref[0, t0:t0 + rows, c0:c1]
            y = _dot(pooled.astype(BF16), w_ref[gi]) * s_ref[:, c0:c1]
            o_ref[0, t0:t0 + rows, c0:c1] = y.astype(o_ref.dtype)


def _pool(proj, pool_w, scale):
    b, t, _ = proj.shape
    ng = len(POOL_WINDOWS)
    return pl.pallas_call(
        _pool_kernel,
        grid=(b,),
        in_specs=[pl.BlockSpec((1, t, HALF), lambda i: (i, 0, 0)),
                  pl.BlockSpec((ng, POOL_GROUP_DIM, POOL_GROUP_DIM), lambda i: (0, 0, 0)),
                  pl.BlockSpec((1, HALF), lambda i: (0, 0))],
        out_specs=pl.BlockSpec((1, t, HALF), lambda i: (i, 0, 0)),
        out_shape=jax.ShapeDtypeStruct((b, t, HALF), BF16),
        scratch_shapes=[pltpu.VMEM((t + 2 * POOL_PAD, HALF), F32)],
        compiler_params=_params(("parallel",)),
        name="pool",
    )(proj, pool_w, scale.reshape(1, HALF))


def _rope_tables(t):
    pos = jnp.arange(t)
    row = (pos // GRID_W).astype(F32)
    col = (pos % GRID_W).astype(F32)
    n_freq = HEAD_DIM // 4
    inv = ROPE_THETA ** (-jnp.arange(n_freq, dtype=F32) / n_freq)
    ang = jnp.concatenate([row[:, None] * inv, col[:, None] * inv], axis=-1)
    cos = jnp.repeat(jnp.cos(ang), 2, axis=-1)
    sin = jnp.repeat(jnp.sin(ang), 2, axis=-1)
    sign = jnp.where(jnp.arange(HEAD_DIM) % 2 == 0, -1.0, 1.0).astype(F32)
    return cos, sin * sign


def _norm_rope(x, g, cos, sin_signed, scale):
    ms = jnp.mean(x * x, axis=-1, keepdims=True)
    xn = x * lax.rsqrt(ms + RMS_EPS) * g
    lane = lax.broadcasted_iota(jnp.int32, x.shape, 1)
    partner = jnp.where((lane & 1) == 0, pltpu.roll(xn, HEAD_DIM - 1, 1), pltpu.roll(xn, 1, 1))
    return (xn * cos + partner * sin_signed) * scale


def _qkv_prep_kernel(q_ref, k_ref, v_ref, qn_ref, kn_ref, cos_ref, sin_ref, qo_ref, ko_ref, vo_ref):
    cos = cos_ref[...]
    sin = sin_ref[...]
    for h in range(GQA_Q_HEADS):
        sl = slice(h * HEAD_DIM, (h + 1) * HEAD_DIM)
        qo_ref[:, sl] = _norm_rope(q_ref[:, sl], qn_ref[...], cos, sin,
                                   HEAD_DIM ** -0.5).astype(qo_ref.dtype)
    for h in range(GQA_KV_HEADS):
        sl = slice(h * HEAD_DIM, (h + 1) * HEAD_DIM)
        ko_ref[:, sl] = _norm_rope(k_ref[:, sl], kn_ref[...], cos, sin, 1.0).astype(ko_ref.dtype)
    vo_ref[...] = v_ref[...].astype(vo_ref.dtype)


def _qkv_prep(proj2d, q_norm, k_norm, cos, sin, t, tm):
    m = proj2d.shape[0]
    tb = t // tm
    kv_blk = (2 * HALF) // KV_W
    return pl.pallas_call(
        _qkv_prep_kernel,
        grid=(m // tm,),
        in_specs=[pl.BlockSpec((tm, HALF), lambda i: (i, 1)),
                  pl.BlockSpec((tm, KV_W), lambda i: (i, kv_blk)),
                  pl.BlockSpec((tm, KV_W), lambda i: (i, kv_blk + 1)),
                  pl.BlockSpec((1, HEAD_DIM), lambda i: (0, 0)),
                  pl.BlockSpec((1, HEAD_DIM), lambda i: (0, 0)),
                  pl.BlockSpec((tm, HEAD_DIM), lambda i: (i % tb, 0)),
                  pl.BlockSpec((tm, HEAD_DIM), lambda i: (i % tb, 0))],
        out_specs=[pl.BlockSpec((tm, HALF), lambda i: (i, 0)),
                   pl.BlockSpec((tm, KV_W), lambda i: (i, 0)),
                   pl.BlockSpec((tm, KV_W), lambda i: (i, 0))],
        out_shape=[jax.ShapeDtypeStruct((m, HALF), BF16),
                   jax.ShapeDtypeStruct((m, KV_W), BF16),
                   jax.ShapeDtypeStruct((m, KV_W), BF16)],
        compiler_params=_params(("parallel",)),
        name="qkv_prep",
    )(proj2d, proj2d, proj2d, q_norm.reshape(1, HEAD_DIM), k_norm.reshape(1, HEAD_DIM), cos, sin)


def _gqa_kernel(q_ref, k_ref, v_ref, o_ref):
    s = _dot_nt(q_ref[0], k_ref[0])
    m = jnp.max(s, axis=-1, keepdims=True)
    p = jnp.exp(s - m)
    l = jnp.sum(p, axis=-1, keepdims=True)
    o = _dot(p.astype(BF16), v_ref[0]) / l
    o_ref[0] = o.astype(o_ref.dtype)


def _gqa(q, k, v, tq):
    b, t, _ = q.shape
    grp = GQA_Q_HEADS // GQA_KV_HEADS
    return pl.pallas_call(
        _gqa_kernel,
        grid=(b, GQA_Q_HEADS, t // tq),
        in_specs=[pl.BlockSpec((1, tq, HEAD_DIM), lambda i, h, j: (i, j, h)),
                  pl.BlockSpec((1, t, HEAD_DIM), lambda i, h, j: (i, 0, h // grp)),
                  pl.BlockSpec((1, t, HEAD_DIM), lambda i, h, j: (i, 0, h // grp))],
        out_specs=pl.BlockSpec((1, tq, HEAD_DIM), lambda i, h, j: (i, j, h)),
        out_shape=jax.ShapeDtypeStruct((b, t, HALF), BF16),
        compiler_params=_params(("parallel", "parallel", "parallel")),
        name="gqa",
    )(q, k, v)


def kernel(x, ab_w_in, ab_w_out, na_rpb, hg_lb_logits, hg_norm_w, cd_w_in, cd_w_out, pool_w,
           pool_scale, d_q_norm, d_k_norm, ln_mix_g, ln_mix_b, ffn_w_gate, ffn_w_up, ffn_w_down,
           ln_ffn_g, ln_ffn_b):
    b, t, d = x.shape
    m = b * t
    cos, sin = _rope_tables(t)
    xf = x.reshape(m, d)
    xb = xf.astype(BF16)
    for layer in range(DEPTH):
        j = layer // 2
        if layer % 2 == 0:
            proj = _matmul(xb, ab_w_in[j].astype(BF16), F32, 1024, 1024, "ab_in_proj")
            proj = proj.reshape(b, t, -1)
            y1 = _na_attention(proj, _na_band(na_rpb[j]))
            y2 = _hgrn(proj, hg_lb_logits, hg_norm_w[j], j)
            w_out = ab_w_out[j]
        else:
            proj = _matmul(xb, cd_w_in[j].astype(BF16), F32, 1024, 512, "cd_in_proj")
            y1 = _pool(proj.reshape(b, t, -1), pool_w[j].astype(BF16), pool_scale[j])
            qn, kn, vn = _qkv_prep(proj, d_q_norm[j], d_k_norm[j], cos, sin, t, 512)
            y2 = _gqa(qn.reshape(b, t, HALF), kn.reshape(b, t, KV_W), vn.reshape(b, t, KV_W), 256)
            w_out = cd_w_out[j]
        xf, xb = _mix_out(y1.reshape(m, HALF), y2.reshape(m, HALF), w_out.astype(BF16), xf,
                          ln_mix_g[layer], ln_mix_b[layer], 512)
        h = _ffn_up(xb, ffn_w_gate[layer].astype(BF16), ffn_w_up[layer].astype(BF16), 1024, 512)
        xf, xb = _ffn_down(h, ffn_w_down[layer].astype(BF16), xf, ln_ffn_g[layer], ln_ffn_b[layer],
                           512, 512)
    return xf.reshape(b, t, d)
```

```python
import functools

import jax
import jax.numpy as jnp
from jax import lax
from jax.experimental import pallas as pl
from jax.experimental.pallas import tpu as pltpu

D_MODEL = 2048
DEPTH = 4
GRID_W = 64
HALF = 1024
HEAD_DIM = 128
NA_HEADS = 8
NA_WIN_ROWS = 8
NA_WIN_COLS = 16
HG_HEADS = 8
HG_CHUNK = 128
POOL_WINDOWS = (2, 4, 8, 16)
POOL_GROUP_DIM = 256
GQA_Q_HEADS = 8
GQA_KV_HEADS = 2
KV_W = GQA_KV_HEADS * HEAD_DIM
ROPE_THETA = 10000.0
FFN_HIDDEN = 5632
DN_ALPHA = (2 * DEPTH) ** 0.25
LN_EPS = 1e-5
RMS_EPS = 1e-6
NEG_BIG = -1e30

V7X_VMEM_LIMIT_BYTES = 56 * 1024 * 1024

F32 = jnp.float32
BF16 = jnp.bfloat16


def _params(semantics):
    return pltpu.CompilerParams(dimension_semantics=semantics,
                                vmem_limit_bytes=V7X_VMEM_LIMIT_BYTES)


def _sigmoid(x):
    return 1.0 / (1.0 + jnp.exp(-x))


def _silu(x):
    return x * _sigmoid(x)


def _dot(a, b):
    return jnp.dot(a, b, preferred_element_type=F32)


def _dot_nt(a, b):
    return lax.dot_general(a, b, (((1,), (1,)), ((), ())), preferred_element_type=F32)


def _dot_tn(a, b):
    return lax.dot_general(a, b, (((0,), (0,)), ((), ())), preferred_element_type=F32)


def _matmul_kernel(a_ref, w_ref, o_ref, wb_ref):
    @pl.when(pl.program_id(1) == 0)
    def _():
        wb_ref[...] = w_ref[...].astype(BF16)

    o_ref[...] = _dot(a_ref[...], wb_ref[...]).astype(o_ref.dtype)


def _matmul(a, w, out_dtype, tm, tn, name):
    m, k = a.shape
    n = w.shape[1]
    return pl.pallas_call(
        _matmul_kernel,
        grid=(n // tn, m // tm),
        in_specs=[pl.BlockSpec((tm, k), lambda j, i: (i, 0)),
                  pl.BlockSpec((k, tn), lambda j, i: (0, j))],
        out_specs=pl.BlockSpec((tm, tn), lambda j, i: (i, j)),
        out_shape=jax.ShapeDtypeStruct((m, n), out_dtype),
        scratch_shapes=[pltpu.VMEM((k, tn), BF16)],
        compiler_params=_params(("arbitrary", "arbitrary")),
        name=name,
    )(a, w)


def _res_ln(acc, x, g, b):
    z = DN_ALPHA * x + acc
    mu = jnp.mean(z, axis=-1, keepdims=True)
    zc = z - mu
    var = jnp.mean(zc * zc, axis=-1, keepdims=True)
    return zc * lax.rsqrt(var + LN_EPS) * g + b


def _mix_out_kernel(y1_ref, y2_ref, w_ref, x_ref, g_ref, b_ref, of_ref, ob_ref):
    half = y1_ref.shape[1]
    acc = _dot(y1_ref[...], w_ref[:half, :]) + _dot(y2_ref[...], w_ref[half:, :])
    out = _res_ln(acc, x_ref[...], g_ref[...], b_ref[...])
    of_ref[...] = out
    ob_ref[...] = out.astype(BF16)


def _mix_out(y1, y2, w, x, g, b, tm):
    m, half = y1.shape
    d = w.shape[1]
    row = lambda i: (i, 0)
    fixed = lambda i: (0, 0)
    return pl.pallas_call(
        _mix_out_kernel,
        grid=(m // tm,),
        in_specs=[pl.BlockSpec((tm, half), row), pl.BlockSpec((tm, half), row),
                  pl.BlockSpec((2 * half, d), fixed, pipeline_mode=pl.Buffered(1)),
                  pl.BlockSpec((tm, d), row),
                  pl.BlockSpec((1, d), fixed), pl.BlockSpec((1, d), fixed)],
        out_specs=[pl.BlockSpec((tm, d), row), pl.BlockSpec((tm, d), row)],
        out_shape=[jax.ShapeDtypeStruct((m, d), F32), jax.ShapeDtypeStruct((m, d), BF16)],
        compiler_params=_params(("parallel",)),
        name="mix_out_ln",
    )(y1, y2, w, x, g.reshape(1, d), b.reshape(1, d))


def _ffn_up_kernel(x_ref, wg_ref, wu_ref, h_ref, wgb_ref, wub_ref):
    @pl.when(pl.program_id(1) == 0)
    def _():
        wgb_ref[...] = wg_ref[...].astype(BF16)
        wub_ref[...] = wu_ref[...].astype(BF16)

    x = x_ref[...]
    gate = _dot(x, wgb_ref[...])
    up = _dot(x, wub_ref[...])
    h_ref[...] = (_silu(gate) * up).astype(h_ref.dtype)


def _ffn_up(xb, wg, wu, tm, tn):
    m, k = xb.shape
    n = wg.shape[1]
    return pl.pallas_call(
        _ffn_up_kernel,
        grid=(n // tn, m // tm),
        in_specs=[pl.BlockSpec((tm, k), lambda j, i: (i, 0)),
                  pl.BlockSpec((k, tn), lambda j, i: (0, j)),
                  pl.BlockSpec((k, tn), lambda j, i: (0, j))],
        out_specs=pl.BlockSpec((tm, tn), lambda j, i: (i, j)),
        out_shape=jax.ShapeDtypeStruct((m, n), BF16),
        scratch_shapes=[pltpu.VMEM((k, tn), BF16), pltpu.VMEM((k, tn), BF16)],
        compiler_params=_params(("arbitrary", "arbitrary")),
        name="ffn_up",
    )(xb, wg, wu)


def _ffn_down_kernel(h_ref, w_ref, x_ref, g_ref, b_ref, of_ref, ob_ref):
    out = _res_ln(_dot(h_ref[...], w_ref[...]), x_ref[...], g_ref[...], b_ref[...])
    of_ref[...] = out
    ob_ref[...] = out.astype(BF16)


def _ffn_down(h, w, x, g, b, tm):
    m, kk = h.shape
    d = w.shape[1]
    row = lambda i: (i, 0)
    fixed = lambda i: (0, 0)
    return pl.pallas_call(
        _ffn_down_kernel,
        grid=(m // tm,),
        in_specs=[pl.BlockSpec((tm, kk), row),
                  pl.BlockSpec((kk, d), fixed, pipeline_mode=pl.Buffered(1)),
                  pl.BlockSpec((tm, d), row),
                  pl.BlockSpec((1, d), fixed), pl.BlockSpec((1, d), fixed)],
        out_specs=[pl.BlockSpec((tm, d), row), pl.BlockSpec((tm, d), row)],
        out_shape=[jax.ShapeDtypeStruct((m, d), F32), jax.ShapeDtypeStruct((m, d), BF16)],
        compiler_params=_params(("parallel",)),
        name="ffn_down_ln",
    )(h, w, x, g.reshape(1, d), b.reshape(1, d))


N_BAND = NA_WIN_ROWS
RPB_ROWS = 2 * NA_WIN_ROWS - 1
RPB_COLS = 2 * NA_WIN_COLS - 1


def _na_band_kernel(rpb_ref, o_ref):
    h = pl.program_id(0)
    base = h * (RPB_ROWS * RPB_COLS)
    shape = (GRID_W, 2 * GRID_W)
    q = lax.broadcasted_iota(jnp.int32, shape, 0)
    lane = lax.broadcasted_iota(jnp.int32, shape, 1)
    kc = lane & (GRID_W - 1)
    second = lane >> 6
    code = kc - q + (NA_WIN_COLS - 1) + RPB_COLS * second
    start = jnp.clip(q - NA_WIN_COLS // 2, 0, GRID_W - NA_WIN_COLS)
    valid = (kc >= start) & (kc < start + NA_WIN_COLS)
    pairs = []
    for ri in range(RPB_ROWS - 1):
        acc = jnp.full(shape, NEG_BIG, F32)
        for i in range(2 * RPB_COLS):
            acc = jnp.where(code == i, rpb_ref[base + ri * RPB_COLS + i], acc)
        pairs.append(jnp.where(valid, acc, NEG_BIG))
    for r in range(N_BAND):
        for mm in range(NA_WIN_ROWS // 2):
            o_ref[0, r, :, mm * 2 * GRID_W:(mm + 1) * 2 * GRID_W] = pairs[r + 2 * mm]


def _na_band(rpb):
    return pl.pallas_call(
        _na_band_kernel,
        grid=(NA_HEADS,),
        in_specs=[pl.BlockSpec(memory_space=pltpu.SMEM)],
        out_specs=pl.BlockSpec((1, N_BAND, GRID_W, NA_WIN_ROWS * GRID_W), lambda h: (h, 0, 0, 0)),
        out_shape=jax.ShapeDtypeStruct((NA_HEADS, N_BAND, GRID_W, NA_WIN_ROWS * GRID_W), F32),
        compiler_params=_params(("arbitrary",)),
        name="na_band",
    )(rpb.reshape(-1))


def _na_kernel(q_ref, k_ref, v_ref, band_ref, o_ref, kb_ref, vb_ref):
    t = q_ref.shape[1]
    rows = t // GRID_W
    kh = NA_WIN_ROWS
    band_len = kh * GRID_W
    kb_ref[...] = k_ref[0].astype(BF16)
    vb_ref[...] = v_ref[0].astype(BF16)
    scale = HEAD_DIM ** -0.5

    for r in range(rows):
        r0 = min(max(r - kh // 2, 0), rows - kh)
        band_idx = r0 - r + (kh - 1)
        qoff = r * GRID_W
        koff = r0 * GRID_W
        q = (q_ref[0, qoff:qoff + GRID_W, :] * scale).astype(BF16)
        s = _dot_nt(q, kb_ref[koff:koff + band_len, :]) + band_ref[0, band_idx]
        m = jnp.max(s, axis=-1, keepdims=True)
        p = jnp.exp(s - m)
        l = jnp.sum(p, axis=-1, keepdims=True)
        o = _dot(p.astype(BF16), vb_ref[koff:koff + band_len, :]) / l
        o_ref[0, qoff:qoff + GRID_W, :] = o.astype(o_ref.dtype)


def _na_attention(proj, band):
    b, t, _ = proj.shape
    nh = NA_HEADS
    blk = (1, t, HEAD_DIM)
    return pl.pallas_call(
        _na_kernel,
        grid=(nh, b),
        in_specs=[pl.BlockSpec(blk, lambda h, i: (i, 0, h)),
                  pl.BlockSpec(blk, lambda h, i: (i, 0, nh + h)),
                  pl.BlockSpec(blk, lambda h, i: (i, 0, 2 * nh + h)),
                  pl.BlockSpec((1, N_BAND, GRID_W, NA_WIN_ROWS * GRID_W), lambda h, i: (h, 0, 0, 0))],
        out_specs=pl.BlockSpec(blk, lambda h, i: (i, 0, h)),
        out_shape=jax.ShapeDtypeStruct((b, t, HALF), BF16),
        scratch_shapes=[pltpu.VMEM((t, HEAD_DIM), BF16), pltpu.VMEM((t, HEAD_DIM), BF16)],
        compiler_params=_params(("parallel", "parallel")),
        name="na_attention",
    )(proj, proj, proj, band)


HG_LEVELS = tuple(HG_CHUNK >> (i + 1) for i in range(HG_CHUNK.bit_length() - 1))
HG_DIAG = len(HG_LEVELS)


def _split2(x):
    hi = x.astype(BF16)
    mid = (x - hi.astype(F32)).astype(BF16)
    return hi, mid


def _level_codes(c, rev):
    t = lax.broadcasted_iota(jnp.int32, (c, c), 0)
    s = lax.broadcasted_iota(jnp.int32, (c, c), 1)
    x = t ^ s
    code = jnp.full((c, c), HG_DIAG, jnp.int32)
    for li in reversed(range(len(HG_LEVELS))):
        code = jnp.where(x >= HG_LEVELS[li], li, code)
    return jnp.where((s >= t) if rev else (s <= t), code, -1)


def _pivot_rows(gc, m, rev):
    c = gc.shape[0]
    blk = 2 * m
    if blk >= 8:
        piv = m if rev else m - 1
        parts = []
        for s in range(0, c, blk):
            parts.append(jnp.broadcast_to(gc[s + piv:s + piv + 1, :], (blk, gc.shape[1])))
        return parts[0] if len(parts) == 1 else jnp.concatenate(parts, axis=0)
    pos = lax.broadcasted_iota(jnp.int32, gc.shape, 0) & (blk - 1)
    piv = m if rev else m - 1
    out = gc
    for p in range(blk):
        if p == piv:
            continue
        shift = (p - piv) % c
        out = jnp.where(pos == p, pltpu.roll(gc, shift, 0), out)
    return out


def _hgrn_chunk(z, q, v, st, lb, tri, codes, rev):
    c = z.shape[0]
    f = lb + (1.0 - lb) * _sigmoid(z)
    k = 1.0 - f
    hi, mid = _split2(jnp.log2(f))
    gc = _dot(tri, hi) + _dot(tri, mid)
    g_tot = gc[0:1, :] if rev else gc[c - 1:c, :]
    pos = lax.broadcasted_iota(jnp.int32, z.shape, 0)

    a = jnp.where(codes == HG_DIAG, jnp.sum(q * k, axis=-1, keepdims=True), 0.0)
    for li, m in enumerate(HG_LEVELS):
        piv = _pivot_rows(gc, m, rev)
        is_q = ((pos & m) == 0) if rev else ((pos & m) != 0)
        d = jnp.where(is_q, gc - piv, piv - gc)
        xl = (jnp.where(is_q, q, k) * jnp.exp2(d)).astype(BF16)
        a = jnp.where(codes == li, _dot_nt(xl, xl), a)

    vb = v.astype(BF16)
    o = _dot(a.astype(BF16), vb)
    o = o + _dot_nt((q * jnp.exp2(gc)).astype(BF16), st.astype(BF16))
    kd = (k * jnp.exp2(g_tot - gc)).astype(BF16)
    st_new = st * jnp.exp2(g_tot) + _dot_tn(vb, kd)
    return o, st_new


def _hgrn_kernel(q_ref, ff_ref, fb_ref, i_ref, g_ref, lbl_ref, nw_ref, o_ref, of_ref, ob_ref,
                 *, layer):
    t = q_ref.shape[1]
    c = HG_CHUNK
    n = t // c
    dk = q_ref.shape[2]
    dv = i_ref.shape[2]

    lg = lbl_ref[...]
    e = jnp.exp(lg - jnp.max(lg, axis=0, keepdims=True))
    p = e / jnp.sum(e, axis=0, keepdims=True)
    lb = jnp.sum(p[:layer + 1, :], axis=0, keepdims=True) - p[0:1, :]

    row = lax.broadcasted_iota(jnp.int32, (c, c), 0)
    col = lax.broadcasted_iota(jnp.int32, (c, c), 1)
    tril = jnp.where(col <= row, 1.0, 0.0).astype(BF16)
    triu = jnp.where(col >= row, 1.0, 0.0).astype(BF16)
    codes_f = _level_codes(c, False)
    codes_b = _level_codes(c, True)

    def body(i, carry):
        st_f, st_b = carry
        off_f = pl.multiple_of(i * c, c)
        off_b = pl.multiple_of((n - 1 - i) * c, c)
        o_f, st_f = _hgrn_chunk(ff_ref[0, pl.ds(off_f, c), :], _silu(q_ref[0, pl.ds(off_f, c), :]),
                                i_ref[0, pl.ds(off_f, c), :], st_f, lb, tril, codes_f, False)
        o_b, st_b = _hgrn_chunk(fb_ref[0, pl.ds(off_b, c), :], _silu(q_ref[0, pl.ds(off_b, c), :]),
                                i_ref[0, pl.ds(off_b, c), :], st_b, lb, triu, codes_b, True)
        of_ref[pl.ds(off_f, c), :] = o_f
        ob_ref[pl.ds(off_b, c), :] = o_b
        return st_f, st_b

    zero = jnp.zeros((dv, dk), F32)
    lax.fori_loop(0, n, body, (zero, zero))

    nw = nw_ref[...]
    rows = 256

    def finish(i, carry):
        off = pl.multiple_of(i * rows, rows)
        o = of_ref[pl.ds(off, rows), :] + ob_ref[pl.ds(off, rows), :]
        ms = jnp.mean(o * o, axis=-1, keepdims=True)
        y = o * lax.rsqrt(ms + RMS_EPS) * nw
        o_ref[0, pl.ds(off, rows), :] = (y * _silu(g_ref[0, pl.ds(off, rows), :])).astype(o_ref.dtype)
        return carry

    lax.fori_loop(0, t // rows, finish, 0)


def _hgrn(proj, lb_logits, norm_w, layer):
    b, t, _ = proj.shape
    nh = HG_HEADS
    n_layers = lb_logits.shape[0]
    blk = (1, t, HEAD_DIM)
    base = 3 * HALF // HEAD_DIM

    def col(k):
        return lambda i, h: (i, 0, base + k * nh + h)

    return pl.pallas_call(
        functools.partial(_hgrn_kernel, layer=layer),
        grid=(b, nh),
        in_specs=[pl.BlockSpec(blk, col(0)), pl.BlockSpec(blk, col(1)), pl.BlockSpec(blk, col(2)),
                  pl.BlockSpec(blk, col(3)), pl.BlockSpec(blk, col(4)),
                  pl.BlockSpec((n_layers, HEAD_DIM), lambda i, h: (0, h)),
                  pl.BlockSpec((1, HEAD_DIM), lambda i, h: (0, h))],
        out_specs=pl.BlockSpec(blk, lambda i, h: (i, 0, h)),
        out_shape=jax.ShapeDtypeStruct((b, t, HALF), BF16),
        scratch_shapes=[pltpu.VMEM((t, HEAD_DIM), F32), pltpu.VMEM((t, HEAD_DIM), F32)],
        compiler_params=_params(("parallel", "parallel")),
        name="hgrn2",
    )(proj, proj, proj, proj, proj, lb_logits, norm_w.reshape(1, HALF))


POOL_PAD = 8


def _pool_kernel(x_ref, w_ref, s_ref, o_ref, xp_ref):
    t = x_ref.shape[1]
    width = x_ref.shape[2]
    rows = 256
    xp_ref[0:POOL_PAD, :] = jnp.zeros((POOL_PAD, width), F32)
    xp_ref[POOL_PAD + t:, :] = jnp.zeros((POOL_PAD, width), F32)
    xp_ref[POOL_PAD:POOL_PAD + t, :] = x_ref[0]
    for ci in range(t // rows):
        t0 = ci * rows
        pos = t0 + lax.broadcasted_iota(jnp.int32, (rows, POOL_GROUP_DIM), 0)
        for gi, win in enumerate(POOL_WINDOWS):
            hw = win // 2
            c0 = gi * POOL_GROUP_DIM
            c1 = c0 + POOL_GROUP_DIM
            tot = None
            for o in range(-hw, hw):
                piece = xp_ref[POOL_PAD + t0 + o:POOL_PAD + t0 + o + rows, c0:c1]
                tot = piece if tot is None else tot + piece
            cnt = (jnp.minimum(pos + hw, t) - jnp.maximum(pos - hw, 0)).astype(F32)
            pooled = tot / cnt - x_ref[0, t0:t0 + rows, c0:c1]
            y = _dot(pooled.astype(BF16), w_ref[gi]) * s_ref[:, c0:c1]
            o_ref[0, t0:t0 + rows, c0:c1] = y.astype(o_ref.dtype)


def _pool(proj, pool_w, scale):
    b, t, _ = proj.shape
    ng = len(POOL_WINDOWS)
    return pl.pallas_call(
        _pool_kernel,
        grid=(b,),
        in_specs=[pl.BlockSpec((1, t, HALF), lambda i: (i, 0, 0)),
                  pl.BlockSpec((ng, POOL_GROUP_DIM, POOL_GROUP_DIM), lambda i: (0, 0, 0)),
                  pl.BlockSpec((1, HALF), lambda i: (0, 0))],
        out_specs=pl.BlockSpec((1, t, HALF), lambda i: (i, 0, 0)),
        out_shape=jax.ShapeDtypeStruct((b, t, HALF), BF16),
        scratch_shapes=[pltpu.VMEM((t + 2 * POOL_PAD, HALF), F32)],
        compiler_params=_params(("parallel",)),
        name="pool",
    )(proj, pool_w, scale.reshape(1, HALF))


def _rope_tables(t):
    pos = jnp.arange(t)
    row = (pos // GRID_W).astype(F32)
    col = (pos % GRID_W).astype(F32)
    n_freq = HEAD_DIM // 4
    inv = ROPE_THETA ** (-jnp.arange(n_freq, dtype=F32) / n_freq)
    ang = jnp.concatenate([row[:, None] * inv, col[:, None] * inv], axis=-1)
    cos = jnp.repeat(jnp.cos(ang), 2, axis=-1)
    sin = jnp.repeat(jnp.sin(ang), 2, axis=-1)
    sign = jnp.where(jnp.arange(HEAD_DIM) % 2 == 0, -1.0, 1.0).astype(F32)
    return cos, sin * sign


def _norm_rope(x, g, cos, sin_signed, scale):
    ms = jnp.mean(x * x, axis=-1, keepdims=True)
    xn = x * lax.rsqrt(ms + RMS_EPS) * g
    lane = lax.broadcasted_iota(jnp.int32, x.shape, 1)
    partner = jnp.where((lane & 1) == 0, pltpu.roll(xn, HEAD_DIM - 1, 1), pltpu.roll(xn, 1, 1))
    return (xn * cos + partner * sin_signed) * scale


def _qkv_prep_kernel(q_ref, k_ref, v_ref, qn_ref, kn_ref, cos_ref, sin_ref, qo_ref, ko_ref, vo_ref):
    cos = cos_ref[...]
    sin = sin_ref[...]
    for h in range(GQA_Q_HEADS):
        sl = slice(h * HEAD_DIM, (h + 1) * HEAD_DIM)
        qo_ref[:, sl] = _norm_rope(q_ref[:, sl], qn_ref[...], cos, sin,
                                   HEAD_DIM ** -0.5).astype(qo_ref.dtype)
    for h in range(GQA_KV_HEADS):
        sl = slice(h * HEAD_DIM, (h + 1) * HEAD_DIM)
        ko_ref[:, sl] = _norm_rope(k_ref[:, sl], kn_ref[...], cos, sin, 1.0).astype(ko_ref.dtype)
    vo_ref[...] = v_ref[...].astype(vo_ref.dtype)


def _qkv_prep(proj2d, q_norm, k_norm, cos, sin, t, tm):
    m = proj2d.shape[0]
    tb = t // tm
    kv_blk = (2 * HALF) // KV_W
    return pl.pallas_call(
        _qkv_prep_kernel,
        grid=(m // tm,),
        in_specs=[pl.BlockSpec((tm, HALF), lambda i: (i, 1)),
                  pl.BlockSpec((tm, KV_W), lambda i: (i, kv_blk)),
                  pl.BlockSpec((tm, KV_W), lambda i: (i, kv_blk + 1)),
                  pl.BlockSpec((1, HEAD_DIM), lambda i: (0, 0)),
                  pl.BlockSpec((1, HEAD_DIM), lambda i: (0, 0)),
                  pl.BlockSpec((tm, HEAD_DIM), lambda i: (i % tb, 0)),
                  pl.BlockSpec((tm, HEAD_DIM), lambda i: (i % tb, 0))],
        out_specs=[pl.BlockSpec((tm, HALF), lambda i: (i, 0)),
                   pl.BlockSpec((tm, KV_W), lambda i: (i, 0)),
                   pl.BlockSpec((tm, KV_W), lambda i: (i, 0))],
        out_shape=[jax.ShapeDtypeStruct((m, HALF), BF16),
                   jax.ShapeDtypeStruct((m, KV_W), BF16),
                   jax.ShapeDtypeStruct((m, KV_W), BF16)],
        compiler_params=_params(("parallel",)),
        name="qkv_prep",
    )(proj2d, proj2d, proj2d, q_norm.reshape(1, HEAD_DIM), k_norm.reshape(1, HEAD_DIM), cos, sin)


def _gqa_kernel(q_ref, k_ref, v_ref, o_ref):
    s = _dot_nt(q_ref[0], k_ref[0])
    m = jnp.max(s, axis=-1, keepdims=True)
    p = jnp.exp(s - m)
    l = jnp.sum(p, axis=-1, keepdims=True)
    o = _dot(p.astype(BF16), v_ref[0]) / l
    o_ref[0] = o.astype(o_ref.dtype)


def _gqa(q, k, v, tq):
    b, t, _ = q.shape
    grp = GQA_Q_HEADS // GQA_KV_HEADS
    return pl.pallas_call(
        _gqa_kernel,
        grid=(b, GQA_Q_HEADS, t // tq),
        in_specs=[pl.BlockSpec((1, tq, HEAD_DIM), lambda i, h, j: (i, j, h)),
                  pl.BlockSpec((1, t, HEAD_DIM), lambda i, h, j: (i, 0, h // grp)),
                  pl.BlockSpec((1, t, HEAD_DIM), lambda i, h, j: (i, 0, h // grp))],
        out_specs=pl.BlockSpec((1, tq, HEAD_DIM), lambda i, h, j: (i, j, h)),
        out_shape=jax.ShapeDtypeStruct((b, t, HALF), BF16),
        compiler_params=_params(("parallel", "parallel", "parallel")),
        name="gqa",
    )(q, k, v)


def kernel(x, ab_w_in, ab_w_out, na_rpb, hg_lb_logits, hg_norm_w, cd_w_in, cd_w_out, pool_w,
           pool_scale, d_q_norm, d_k_norm, ln_mix_g, ln_mix_b, ffn_w_gate, ffn_w_up, ffn_w_down,
           ln_ffn_g, ln_ffn_b):
    b, t, d = x.shape
    m = b * t
    cos, sin = _rope_tables(t)
    xf = x.reshape(m, d)
    xb = xf.astype(BF16)
    for layer in range(DEPTH):
        j = layer // 2
        if layer % 2 == 0:
            proj = _matmul(xb, ab_w_in[j], F32, 1024, 1024, "ab_in_proj")
            proj = proj.reshape(b, t, -1)
            y1 = _na_attention(proj, _na_band(na_rpb[j]))
            y2 = _hgrn(proj, hg_lb_logits, hg_norm_w[j], j)
            w_out = ab_w_out[j]
        else:
            proj = _matmul(xb, cd_w_in[j], F32, 1024, 512, "cd_in_proj")
            y1 = _pool(proj.reshape(b, t, -1), pool_w[j].astype(BF16), pool_scale[j])
            qn, kn, vn = _qkv_prep(proj, d_q_norm[j], d_k_norm[j], cos, sin, t, 512)
            y2 = _gqa(qn.reshape(b, t, HALF), kn.reshape(b, t, KV_W), vn.reshape(b, t, KV_W), 256)
            w_out = cd_w_out[j]
        xf, xb = _mix_out(y1.reshape(m, HALF), y2.reshape(m, HALF), w_out.astype(BF16), xf,
                          ln_mix_g[layer], ln_mix_b[layer], 512)
        h = _ffn_up(xb, ffn_w_gate[layer], ffn_w_up[layer], 1024, 512)
        xf, xb = _ffn_down(h, ffn_w_down[layer].astype(BF16), xf, ln_ffn_g[layer], ln_ffn_b[layer],
                           256)
    return xf.reshape(b, t, d)
```

```python
import functools

import jax
import jax.numpy as jnp
from jax import lax
from jax.experimental import pallas as pl
from jax.experimental.pallas import tpu as pltpu

D_MODEL = 2048
DEPTH = 4
GRID_W = 64
HALF = 1024
HEAD_DIM = 128
NA_HEADS = 8
NA_WIN_ROWS = 8
NA_WIN_COLS = 16
HG_HEADS = 8
HG_CHUNK = 128
POOL_WINDOWS = (2, 4, 8, 16)
POOL_GROUP_DIM = 256
GQA_Q_HEADS = 8
GQA_KV_HEADS = 2
KV_W = GQA_KV_HEADS * HEAD_DIM
ROPE_THETA = 10000.0
FFN_HIDDEN = 5632
DN_ALPHA = (2 * DEPTH) ** 0.25
LN_EPS = 1e-5
RMS_EPS = 1e-6
NEG_BIG = -1e30
LOG2E = 1.4426950408889634
ATTN_SCALE_LOG2 = HEAD_DIM ** -0.5 * LOG2E

V7X_VMEM_LIMIT_BYTES = 56 * 1024 * 1024

F32 = jnp.float32
BF16 = jnp.bfloat16


def _params(semantics):
    return pltpu.CompilerParams(dimension_semantics=semantics,
                                vmem_limit_bytes=V7X_VMEM_LIMIT_BYTES)


def _sigmoid(x):
    return 1.0 / (1.0 + jnp.exp(-x))


def _silu(x):
    return x * _sigmoid(x)


def _dot(a, b):
    return jnp.dot(a, b, preferred_element_type=F32)


def _dot_nt(a, b):
    return lax.dot_general(a, b, (((1,), (1,)), ((), ())), preferred_element_type=F32)


def _dot_tn(a, b):
    return lax.dot_general(a, b, (((0,), (0,)), ((), ())), preferred_element_type=F32)


def _matmul_kernel(a_ref, w_ref, o_ref, wb_ref):
    @pl.when(pl.program_id(1) == 0)
    def _():
        wb_ref[...] = w_ref[...].astype(BF16)

    o_ref[...] = _dot(a_ref[...], wb_ref[...]).astype(o_ref.dtype)


def _matmul(a, w, layer, out_dtype, tm, tn, name):
    m, k = a.shape
    n = w.shape[2]
    return pl.pallas_call(
        _matmul_kernel,
        grid=(n // tn, m // tm),
        in_specs=[pl.BlockSpec((tm, k), lambda j, i: (i, 0)),
                  pl.BlockSpec((None, k, tn), lambda j, i: (layer, 0, j))],
        out_specs=pl.BlockSpec((tm, tn), lambda j, i: (i, j)),
        out_shape=jax.ShapeDtypeStruct((m, n), out_dtype),
        scratch_shapes=[pltpu.VMEM((k, tn), BF16)],
        compiler_params=_params(("arbitrary", "arbitrary")),
        name=name,
    )(a, w)


def _res_ln(acc, x, g, b):
    z = DN_ALPHA * x + acc
    mu = jnp.mean(z, axis=-1, keepdims=True)
    zc = z - mu
    var = jnp.mean(zc * zc, axis=-1, keepdims=True)
    return zc * lax.rsqrt(var + LN_EPS) * g + b


def _mix_out_kernel(y1_ref, y2_ref, w_ref, x_ref, g_ref, b_ref, of_ref, ob_ref):
    half = y1_ref.shape[1]
    acc = _dot(y1_ref[...], w_ref[:half, :]) + _dot(y2_ref[...], w_ref[half:, :])
    out = _res_ln(acc, x_ref[...], g_ref[...], b_ref[...])
    of_ref[...] = out
    ob_ref[...] = out.astype(BF16)


def _mix_out(y1, y2, w, layer, x, g, b, tm):
    m, half = y1.shape
    d = w.shape[2]
    row = lambda i: (i, 0)
    fixed = lambda i: (0, 0)
    return pl.pallas_call(
        _mix_out_kernel,
        grid=(m // tm,),
        in_specs=[pl.BlockSpec((tm, half), row), pl.BlockSpec((tm, half), row),
                  pl.BlockSpec((None, 2 * half, d), lambda i: (layer, 0, 0),
                               pipeline_mode=pl.Buffered(1)),
                  pl.BlockSpec((tm, d), row),
                  pl.BlockSpec((1, d), fixed), pl.BlockSpec((1, d), fixed)],
        out_specs=[pl.BlockSpec((tm, d), row), pl.BlockSpec((tm, d), row)],
        out_shape=[jax.ShapeDtypeStruct((m, d), F32), jax.ShapeDtypeStruct((m, d), BF16)],
        compiler_params=_params(("parallel",)),
        name="mix_out_ln",
    )(y1, y2, w, x, g.reshape(1, d), b.reshape(1, d))


def _ffn_up_kernel(x_ref, wg_ref, wu_ref, h_ref, wgb_ref, wub_ref):
    @pl.when(pl.program_id(1) == 0)
    def _():
        wgb_ref[...] = wg_ref[...].astype(BF16)
        wub_ref[...] = wu_ref[...].astype(BF16)

    x = x_ref[...]
    gate = _dot(x, wgb_ref[...])
    up = _dot(x, wub_ref[...])
    h_ref[...] = (_silu(gate) * up).astype(h_ref.dtype)


def _ffn_up(xb, wg, wu, layer, tm, tn):
    m, k = xb.shape
    n = wg.shape[2]
    return pl.pallas_call(
        _ffn_up_kernel,
        grid=(n // tn, m // tm),
        in_specs=[pl.BlockSpec((tm, k), lambda j, i: (i, 0)),
                  pl.BlockSpec((None, k, tn), lambda j, i: (layer, 0, j)),
                  pl.BlockSpec((None, k, tn), lambda j, i: (layer, 0, j))],
        out_specs=pl.BlockSpec((tm, tn), lambda j, i: (i, j)),
        out_shape=jax.ShapeDtypeStruct((m, n), BF16),
        scratch_shapes=[pltpu.VMEM((k, tn), BF16), pltpu.VMEM((k, tn), BF16)],
        compiler_params=_params(("arbitrary", "arbitrary")),
        name="ffn_up",
    )(xb, wg, wu)


def _ffn_down_kernel(h_ref, w_ref, x_ref, g_ref, b_ref, of_ref, ob_ref):
    out = _res_ln(_dot(h_ref[...], w_ref[...]), x_ref[...], g_ref[...], b_ref[...])
    of_ref[...] = out
    ob_ref[...] = out.astype(BF16)


def _ffn_down(h, w, layer, x, g, b, tm):
    m, kk = h.shape
    d = w.shape[2]
    row = lambda i: (i, 0)
    fixed = lambda i: (0, 0)
    return pl.pallas_call(
        _ffn_down_kernel,
        grid=(m // tm,),
        in_specs=[pl.BlockSpec((tm, kk), row),
                  pl.BlockSpec((None, kk, d), lambda i: (layer, 0, 0), pipeline_mode=pl.Buffered(1)),
                  pl.BlockSpec((tm, d), row),
                  pl.BlockSpec((1, d), fixed), pl.BlockSpec((1, d), fixed)],
        out_specs=[pl.BlockSpec((tm, d), row), pl.BlockSpec((tm, d), row)],
        out_shape=[jax.ShapeDtypeStruct((m, d), F32), jax.ShapeDtypeStruct((m, d), BF16)],
        compiler_params=_params(("parallel",)),
        name="ffn_down_ln",
    )(h, w, x, g.reshape(1, d), b.reshape(1, d))


NA_GROUP = 4
NA_WINDOW = NA_GROUP + NA_WIN_ROWS
RPB_ROWS = 2 * NA_WIN_ROWS - 1
RPB_COLS = 2 * NA_WIN_COLS - 1


def _na_window_start(g, rows):
    return min(max(g * NA_GROUP - NA_WIN_ROWS // 2, 0), rows - NA_WINDOW)


def _na_patterns(rows):
    pats, ids = [], []
    for r in range(rows):
        w0 = _na_window_start(r // NA_GROUP, rows)
        r0 = min(max(r - NA_WIN_ROWS // 2, 0), rows - NA_WIN_ROWS)
        key = (r0 - w0, w0 - r + NA_WIN_ROWS - 1)
        if key not in pats:
            pats.append(key)
        ids.append(pats.index(key))
    return tuple(pats), tuple(ids)


def _na_band_kernel(rpb_ref, o_ref, *, patterns):
    h = pl.program_id(0)
    base = h * (RPB_ROWS * RPB_COLS)
    shape = (GRID_W, 2 * GRID_W)
    q = lax.broadcasted_iota(jnp.int32, shape, 0)
    lane = lax.broadcasted_iota(jnp.int32, shape, 1)
    kc = lane & (GRID_W - 1)
    second = lane >> 6
    code = kc - q + (NA_WIN_COLS - 1) + RPB_COLS * second
    start = jnp.clip(q - NA_WIN_COLS // 2, 0, GRID_W - NA_WIN_COLS)
    valid = (kc >= start) & (kc < start + NA_WIN_COLS)
    neg = jnp.full(shape, NEG_BIG, F32)
    pairs = {}

    def pair(ri):
        if ri not in pairs:
            acc = neg
            for i in range(2 * RPB_COLS):
                if 0 <= ri + i // RPB_COLS < RPB_ROWS:
                    acc = jnp.where(code == i, rpb_ref[base + ri * RPB_COLS + i], acc)
            pairs[ri] = jnp.where(valid, acc * LOG2E, NEG_BIG)
        return pairs[ri]

    for p, (lo, c) in enumerate(patterns):
        for u in range(NA_WINDOW // 2):
            in0 = lo <= 2 * u < lo + NA_WIN_ROWS
            in1 = lo <= 2 * u + 1 < lo + NA_WIN_ROWS
            if in0 and in1:
                tile = pair(2 * u + c)
            elif in0:
                tile = jnp.where(second == 0, pair(2 * u + c), NEG_BIG)
            elif in1:
                tile = jnp.where(second == 1, pair(2 * u + c), NEG_BIG)
            else:
                tile = neg
            o_ref[0, p, :, u * 2 * GRID_W:(u + 1) * 2 * GRID_W] = tile


def _na_band(rpb, patterns):
    shape = (NA_HEADS, len(patterns), GRID_W, NA_WINDOW * GRID_W)
    return pl.pallas_call(
        functools.partial(_na_band_kernel, patterns=patterns),
        grid=(NA_HEADS,),
        in_specs=[pl.BlockSpec(memory_space=pltpu.SMEM)],
        out_specs=pl.BlockSpec((1,) + shape[1:], lambda h: (h, 0, 0, 0)),
        out_shape=jax.ShapeDtypeStruct(shape, F32),
        compiler_params=_params(("arbitrary",)),
        name="na_band",
    )(rpb.reshape(-1))


def _na_kernel(q_ref, k_ref, v_ref, band_ref, o_ref, kt_ref, vb_ref, *, pattern_ids):
    t = q_ref.shape[1]
    rows = t // GRID_W
    gq = NA_GROUP * GRID_W
    wlen = NA_WINDOW * GRID_W
    n_groups = rows // NA_GROUP
    for c in range(0, t, gq):
        kt_ref[:, c:c + gq] = k_ref[0, c:c + gq, :].T.astype(BF16)
    vb_ref[...] = v_ref[0].astype(BF16)
    scale = ATTN_SCALE_LOG2

    def scores(g):
        w0 = _na_window_start(g, rows) * GRID_W
        q = (q_ref[0, g * gq:(g + 1) * gq, :] * scale).astype(BF16)
        bias = jnp.concatenate(
            [band_ref[0, pattern_ids[g * NA_GROUP + i]] for i in range(NA_GROUP)], axis=0)
        return _dot(q, kt_ref[:, w0:w0 + wlen]) + bias

    s_next = scores(0)
    for g in range(n_groups):
        s = s_next
        if g + 1 < n_groups:
            s_next = scores(g + 1)
        w0 = _na_window_start(g, rows) * GRID_W
        m = jnp.max(s, axis=-1, keepdims=True)
        p = jnp.exp2(s - m)
        l = jnp.sum(p, axis=-1, keepdims=True)
        o = _dot(p.astype(BF16), vb_ref[w0:w0 + wlen, :]) / l
        o_ref[0, g * gq:(g + 1) * gq, :] = o.astype(o_ref.dtype)


def _na_attention(proj, rpb):
    b, t, _ = proj.shape
    nh = NA_HEADS
    patterns, pattern_ids = _na_patterns(t // GRID_W)
    band = _na_band(rpb, patterns)
    blk = (1, t, HEAD_DIM)
    return pl.pallas_call(
        functools.partial(_na_kernel, pattern_ids=pattern_ids),
        grid=(nh, b),
        in_specs=[pl.BlockSpec(blk, lambda h, i: (i, 0, h)),
                  pl.BlockSpec(blk, lambda h, i: (i, 0, nh + h)),
                  pl.BlockSpec(blk, lambda h, i: (i, 0, 2 * nh + h)),
                  pl.BlockSpec((1,) + band.shape[1:], lambda h, i: (h, 0, 0, 0))],
        out_specs=pl.BlockSpec(blk, lambda h, i: (i, 0, h)),
        out_shape=jax.ShapeDtypeStruct((b, t, HALF), BF16),
        scratch_shapes=[pltpu.VMEM((HEAD_DIM, t), BF16), pltpu.VMEM((t, HEAD_DIM), BF16)],
        compiler_params=_params(("parallel", "parallel")),
        name="na_attention",
    )(proj, proj, proj, band)


HG_LEVELS = tuple(HG_CHUNK >> (i + 1) for i in range(HG_CHUNK.bit_length() - 1))
HG_DIAG = len(HG_LEVELS)


def _split2(x):
    hi = x.astype(BF16)
    mid = (x - hi.astype(F32)).astype(BF16)
    return hi, mid


def _level_codes(c, rev):
    t = lax.broadcasted_iota(jnp.int32, (c, c), 0)
    s = lax.broadcasted_iota(jnp.int32, (c, c), 1)
    x = t ^ s
    code = jnp.full((c, c), HG_DIAG, jnp.int32)
    for li in reversed(range(len(HG_LEVELS))):
        code = jnp.where(x >= HG_LEVELS[li], li, code)
    return jnp.where((s >= t) if rev else (s <= t), code, -1)


def _pivot_rows(gc, m, rev):
    c = gc.shape[0]
    blk = 2 * m
    if blk >= 8:
        piv = m if rev else m - 1
        parts = []
        for s in range(0, c, blk):
            parts.append(jnp.broadcast_to(gc[s + piv:s + piv + 1, :], (blk, gc.shape[1])))
        return parts[0] if len(parts) == 1 else jnp.concatenate(parts, axis=0)
    pos = lax.broadcasted_iota(jnp.int32, gc.shape, 0) & (blk - 1)
    piv = m if rev else m - 1
    out = gc
    for p in range(blk):
        if p == piv:
            continue
        shift = (p - piv) % c
        out = jnp.where(pos == p, pltpu.roll(gc, shift, 0), out)
    return out


def _hgrn_chunk(z, q, v, st, lb, tri, codes, rev):
    c = z.shape[0]
    f = lb + (1.0 - lb) * _sigmoid(z)
    k = 1.0 - f
    hi, mid = _split2(jnp.log2(f))
    gc = _dot(tri, hi) + _dot(tri, mid)
    g_tot = gc[0:1, :] if rev else gc[c - 1:c, :]
    pos = lax.broadcasted_iota(jnp.int32, z.shape, 0)

    a = jnp.where(codes == HG_DIAG, jnp.sum(q * k, axis=-1, keepdims=True), 0.0)
    for li, m in enumerate(HG_LEVELS):
        piv = _pivot_rows(gc, m, rev)
        is_q = ((pos & m) == 0) if rev else ((pos & m) != 0)
        d = jnp.where(is_q, gc - piv, piv - gc)
        xl = (jnp.where(is_q, q, k) * jnp.exp2(d)).astype(BF16)
        a = jnp.where(codes == li, _dot_nt(xl, xl), a)

    vb = v.astype(BF16)
    o = _dot(a.astype(BF16), vb)
    o = o + _dot_nt((q * jnp.exp2(gc)).astype(BF16), st.astype(BF16))
    kd = (k * jnp.exp2(g_tot - gc)).astype(BF16)
    st_new = st * jnp.exp2(g_tot) + _dot_tn(vb, kd)
    return o, st_new


def _hgrn_kernel(q_ref, ff_ref, fb_ref, i_ref, g_ref, lbl_ref, nw_ref, o_ref, of_ref, ob_ref,
                 *, layer):
    t = q_ref.shape[1]
    c = HG_CHUNK
    n = t // c
    dk = q_ref.shape[2]
    dv = i_ref.shape[2]

    lg = lbl_ref[...]
    e = jnp.exp(lg - jnp.max(lg, axis=0, keepdims=True))
    p = e / jnp.sum(e, axis=0, keepdims=True)
    lb = jnp.sum(p[:layer + 1, :], axis=0, keepdims=True) - p[0:1, :]

    row = lax.broadcasted_iota(jnp.int32, (c, c), 0)
    col = lax.broadcasted_iota(jnp.int32, (c, c), 1)
    tril = jnp.where(col <= row, 1.0, 0.0).astype(BF16)
    triu = jnp.where(col >= row, 1.0, 0.0).astype(BF16)
    codes_f = _level_codes(c, False)
    codes_b = _level_codes(c, True)

    def body(i, carry):
        st_f, st_b = carry
        off_f = pl.multiple_of(i * c, c)
        off_b = pl.multiple_of((n - 1 - i) * c, c)
        o_f, st_f = _hgrn_chunk(ff_ref[0, pl.ds(off_f, c), :], _silu(q_ref[0, pl.ds(off_f, c), :]),
                                i_ref[0, pl.ds(off_f, c), :], st_f, lb, tril, codes_f, False)
        o_b, st_b = _hgrn_chunk(fb_ref[0, pl.ds(off_b, c), :], _silu(q_ref[0, pl.ds(off_b, c), :]),
                                i_ref[0, pl.ds(off_b, c), :], st_b, lb, triu, codes_b, True)
        of_ref[pl.ds(off_f, c), :] = o_f
        ob_ref[pl.ds(off_b, c), :] = o_b
        return st_f, st_b

    zero = jnp.zeros((dv, dk), F32)
    lax.fori_loop(0, n, body, (zero, zero))

    nw = nw_ref[...]
    rows = 256

    def finish(i, carry):
        off = pl.multiple_of(i * rows, rows)
        o = of_ref[pl.ds(off, rows), :] + ob_ref[pl.ds(off, rows), :]
        ms = jnp.mean(o * o, axis=-1, keepdims=True)
        y = o * lax.rsqrt(ms + RMS_EPS) * nw
        o_ref[0, pl.ds(off, rows), :] = (y * _silu(g_ref[0, pl.ds(off, rows), :])).astype(o_ref.dtype)
        return carry

    lax.fori_loop(0, t // rows, finish, 0)


def _hgrn(proj, lb_logits, norm_w, layer):
    b, t, _ = proj.shape
    nh = HG_HEADS
    n_layers = lb_logits.shape[0]
    blk = (1, t, HEAD_DIM)
    base = 3 * HALF // HEAD_DIM

    def col(k):
        return lambda i, h: (i, 0, base + k * nh + h)

    return pl.pallas_call(
        functools.partial(_hgrn_kernel, layer=layer),
        grid=(b, nh),
        in_specs=[pl.BlockSpec(blk, col(0)), pl.BlockSpec(blk, col(1)), pl.BlockSpec(blk, col(2)),
                  pl.BlockSpec(blk, col(3)), pl.BlockSpec(blk, col(4)),
                  pl.BlockSpec((n_layers, HEAD_DIM), lambda i, h: (0, h)),
                  pl.BlockSpec((1, HEAD_DIM), lambda i, h: (0, h))],
        out_specs=pl.BlockSpec(blk, lambda i, h: (i, 0, h)),
        out_shape=jax.ShapeDtypeStruct((b, t, HALF), BF16),
        scratch_shapes=[pltpu.VMEM((t, HEAD_DIM), F32), pltpu.VMEM((t, HEAD_DIM), F32)],
        compiler_params=_params(("parallel", "parallel")),
        name="hgrn2",
    )(proj, proj, proj, proj, proj, lb_logits, norm_w.reshape(1, HALF))


POOL_PAD = 8


def _pool_kernel(x_ref, w_ref, s_ref, o_ref, xp_ref):
    t = x_ref.shape[1]
    width = x_ref.shape[2]
    rows = 256
    xp_ref[0:POOL_PAD, :] = jnp.zeros((POOL_PAD, width), F32)
    xp_ref[POOL_PAD + t:, :] = jnp.zeros((POOL_PAD, width), F32)
    xp_ref[POOL_PAD:POOL_PAD + t, :] = x_ref[0]
    for ci in range(t // rows):
        t0 = ci * rows
        pos = t0 + lax.broadcasted_iota(jnp.int32, (rows, POOL_GROUP_DIM), 0)
        for gi, win in enumerate(POOL_WINDOWS):
            hw = win // 2
            c0 = gi * POOL_GROUP_DIM
            c1 = c0 + POOL_GROUP_DIM
            tot = None
            for o in range(-hw, hw):
                piece = xp_ref[POOL_PAD + t0 + o:POOL_PAD + t0 + o + rows, c0:c1]
                tot = piece if tot is None else tot + piece
            cnt = (jnp.minimum(pos + hw, t) - jnp.maximum(pos - hw, 0)).astype(F32)
            pooled = tot / cnt - x_ref[0, t0:t0 + rows, c0:c1]
            y = _dot(pooled.astype(BF16), w_ref[gi]) * s_ref[:, c0:c1]
            o_ref[0, t0:t0 + rows, c0:c1] = y.astype(o_ref.dtype)


def _pool(proj, pool_w, scale):
    b, t, _ = proj.shape
    ng = len(POOL_WINDOWS)
    return pl.pallas_call(
        _pool_kernel,
        grid=(b,),
        in_specs=[pl.BlockSpec((1, t, HALF), lambda i: (i, 0, 0)),
                  pl.BlockSpec((ng, POOL_GROUP_DIM, POOL_GROUP_DIM), lambda i: (0, 0, 0)),
                  pl.BlockSpec((1, HALF), lambda i: (0, 0))],
        out_specs=pl.BlockSpec((1, t, HALF), lambda i: (i, 0, 0)),
        out_shape=jax.ShapeDtypeStruct((b, t, HALF), BF16),
        scratch_shapes=[pltpu.VMEM((t + 2 * POOL_PAD, HALF), F32)],
        compiler_params=_params(("parallel",)),
        name="pool",
    )(proj, pool_w, scale.reshape(1, HALF))


def _rope_tables(t):
    pos = jnp.arange(t)
    row = (pos // GRID_W).astype(F32)
    col = (pos % GRID_W).astype(F32)
    n_freq = HEAD_DIM // 4
    inv = ROPE_THETA ** (-jnp.arange(n_freq, dtype=F32) / n_freq)
    ang = jnp.concatenate([row[:, None] * inv, col[:, None] * inv], axis=-1)
    cos = jnp.repeat(jnp.cos(ang), 2, axis=-1)
    sin = jnp.repeat(jnp.sin(ang), 2, axis=-1)
    sign = jnp.where(jnp.arange(HEAD_DIM) % 2 == 0, -1.0, 1.0).astype(F32)
    return cos, sin * sign


def _norm_rope(x, g, cos, sin_signed, scale):
    ms = jnp.mean(x * x, axis=-1, keepdims=True)
    xn = x * lax.rsqrt(ms + RMS_EPS) * g
    lane = lax.broadcasted_iota(jnp.int32, x.shape, 1)
    partner = jnp.where((lane & 1) == 0, pltpu.roll(xn, HEAD_DIM - 1, 1), pltpu.roll(xn, 1, 1))
    return (xn * cos + partner * sin_signed) * scale


def _qkv_prep_kernel(q_ref, k_ref, v_ref, qn_ref, kn_ref, cos_ref, sin_ref, qo_ref, kt_ref, vo_ref):
    cos = cos_ref[...]
    sin = sin_ref[...]
    for h in range(GQA_Q_HEADS):
        sl = slice(h * HEAD_DIM, (h + 1) * HEAD_DIM)
        qo_ref[:, sl] = _norm_rope(q_ref[:, sl], qn_ref[...], cos, sin,
                                   ATTN_SCALE_LOG2).astype(qo_ref.dtype)
    for h in range(GQA_KV_HEADS):
        sl = slice(h * HEAD_DIM, (h + 1) * HEAD_DIM)
        kt_ref[0, sl, :] = _norm_rope(k_ref[:, sl], kn_ref[...], cos, sin, 1.0).T.astype(kt_ref.dtype)
    vo_ref[...] = v_ref[...].astype(vo_ref.dtype)


def _qkv_prep(proj2d, q_norm, k_norm, cos, sin, b, t, tm):
    m = proj2d.shape[0]
    tb = t // tm
    kv_blk = (2 * HALF) // KV_W
    return pl.pallas_call(
        _qkv_prep_kernel,
        grid=(m // tm,),
        in_specs=[pl.BlockSpec((tm, HALF), lambda i: (i, 1)),
                  pl.BlockSpec((tm, KV_W), lambda i: (i, kv_blk)),
                  pl.BlockSpec((tm, KV_W), lambda i: (i, kv_blk + 1)),
                  pl.BlockSpec((1, HEAD_DIM), lambda i: (0, 0)),
                  pl.BlockSpec((1, HEAD_DIM), lambda i: (0, 0)),
                  pl.BlockSpec((tm, HEAD_DIM), lambda i: (i % tb, 0)),
                  pl.BlockSpec((tm, HEAD_DIM), lambda i: (i % tb, 0))],
        out_specs=[pl.BlockSpec((tm, HALF), lambda i: (i, 0)),
                   pl.BlockSpec((1, KV_W, tm), lambda i: (i // tb, 0, i % tb)),
                   pl.BlockSpec((tm, KV_W), lambda i: (i, 0))],
        out_shape=[jax.ShapeDtypeStruct((m, HALF), BF16),
                   jax.ShapeDtypeStruct((b, KV_W, t), BF16),
                   jax.ShapeDtypeStruct((m, KV_W), BF16)],
        compiler_params=_params(("parallel",)),
        name="qkv_prep",
    )(proj2d, proj2d, proj2d, q_norm.reshape(1, HEAD_DIM), k_norm.reshape(1, HEAD_DIM), cos, sin)


GQA_GROUP = GQA_Q_HEADS // GQA_KV_HEADS


def _gqa_kernel(q_ref, kt_ref, v_ref, o_ref):
    def scores(h):
        return _dot(q_ref[0, :, h * HEAD_DIM:(h + 1) * HEAD_DIM], kt_ref[0])

    s_next = scores(0)
    for h in range(GQA_GROUP):
        s = s_next
        if h + 1 < GQA_GROUP:
            s_next = scores(h + 1)
        m = jnp.max(s, axis=-1, keepdims=True)
        p = jnp.exp2(s - m)
        l = jnp.sum(p, axis=-1, keepdims=True)
        o = _dot(p.astype(BF16), v_ref[0]) / l
        o_ref[0, :, h * HEAD_DIM:(h + 1) * HEAD_DIM] = o.astype(o_ref.dtype)


def _gqa(q, kt, v, tq):
    b, t, _ = q.shape
    gw = GQA_GROUP * HEAD_DIM
    return pl.pallas_call(
        _gqa_kernel,
        grid=(b, GQA_KV_HEADS, t // tq),
        in_specs=[pl.BlockSpec((1, tq, gw), lambda i, h, j: (i, j, h)),
                  pl.BlockSpec((1, HEAD_DIM, t), lambda i, h, j: (i, h, 0)),
                  pl.BlockSpec((1, t, HEAD_DIM), lambda i, h, j: (i, 0, h))],
        out_specs=pl.BlockSpec((1, tq, gw), lambda i, h, j: (i, j, h)),
        out_shape=jax.ShapeDtypeStruct((b, t, HALF), BF16),
        compiler_params=_params(("parallel", "parallel", "parallel")),
        name="gqa",
    )(q, kt, v)


def kernel(x, ab_w_in, ab_w_out, na_rpb, hg_lb_logits, hg_norm_w, cd_w_in, cd_w_out, pool_w,
           pool_scale, d_q_norm, d_k_norm, ln_mix_g, ln_mix_b, ffn_w_gate, ffn_w_up, ffn_w_down,
           ln_ffn_g, ln_ffn_b):
    b, t, d = x.shape
    m = b * t
    cos, sin = _rope_tables(t)
    xf = x.reshape(m, d)
    xb = xf.astype(BF16)
    ab_w_out_b = ab_w_out.astype(BF16)
    cd_w_out_b = cd_w_out.astype(BF16)
    pool_w_b = pool_w.astype(BF16)
    ffn_w_down_b = ffn_w_down.astype(BF16)
    for layer in range(DEPTH):
        j = layer // 2
        if layer % 2 == 0:
            proj = _matmul(xb, ab_w_in, j, F32, 1024, 1024, "ab_in_proj").reshape(b, t, -1)
            y1 = _na_attention(proj, na_rpb[j])
            y2 = _hgrn(proj, hg_lb_logits, hg_norm_w[j], j)
            w_out = ab_w_out_b
        else:
            proj = _matmul(xb, cd_w_in, j, F32, 1024, 512, "cd_in_proj")
            y1 = _pool(proj.reshape(b, t, -1), pool_w_b[j], pool_scale[j])
            qn, kt, vn = _qkv_prep(proj, d_q_norm[j], d_k_norm[j], cos, sin, b, t, 512)
            y2 = _gqa(qn.reshape(b, t, HALF), kt, vn.reshape(b, t, KV_W), 256)
            w_out = cd_w_out_b
        xf, xb = _mix_out(y1.reshape(m, HALF), y2.reshape(m, HALF), w_out, j, xf,
                          ln_mix_g[layer], ln_mix_b[layer], 512)
        h = _ffn_up(xb, ffn_w_gate, ffn_w_up, layer, 1024, 512)
        xf, xb = _ffn_down(h, ffn_w_down_b, layer, xf, ln_ffn_g[layer], ln_ffn_b[layer], 256)
    return xf.reshape(b, t, d)
```

```python
import functools

import jax
import jax.numpy as jnp
from jax import lax
from jax.experimental import pallas as pl
from jax.experimental.pallas import tpu as pltpu

D_MODEL = 2048
DEPTH = 4
GRID_W = 64
HALF = 1024
HEAD_DIM = 128
NA_HEADS = 8
NA_WIN_ROWS = 8
NA_WIN_COLS = 16
HG_HEADS = 8
HG_CHUNK = 128
POOL_WINDOWS = (2, 4, 8, 16)
POOL_GROUP_DIM = 256
GQA_Q_HEADS = 8
GQA_KV_HEADS = 2
KV_W = GQA_KV_HEADS * HEAD_DIM
ROPE_THETA = 10000.0
FFN_HIDDEN = 5632
DN_ALPHA = (2 * DEPTH) ** 0.25
LN_EPS = 1e-5
RMS_EPS = 1e-6
NEG_BIG = -1e30
LOG2E = 1.4426950408889634
ATTN_SCALE_LOG2 = HEAD_DIM ** -0.5 * LOG2E

V7X_VMEM_LIMIT_BYTES = 56 * 1024 * 1024

F32 = jnp.float32
BF16 = jnp.bfloat16


def _params(semantics):
    return pltpu.CompilerParams(dimension_semantics=semantics,
                                vmem_limit_bytes=V7X_VMEM_LIMIT_BYTES)


def _sigmoid(x):
    return 1.0 / (1.0 + jnp.exp(-x))


def _silu(x):
    return x * _sigmoid(x)


def _dot(a, b):
    return jnp.dot(a, b, preferred_element_type=F32)


def _dot_nt(a, b):
    return lax.dot_general(a, b, (((1,), (1,)), ((), ())), preferred_element_type=F32)


def _dot_tn(a, b):
    return lax.dot_general(a, b, (((0,), (0,)), ((), ())), preferred_element_type=F32)


def _matmul_kernel(a_ref, w_ref, o_ref, wb_ref):
    @pl.when(pl.program_id(1) == 0)
    def _():
        wb_ref[...] = w_ref[...].astype(BF16)

    o_ref[...] = _dot(a_ref[...], wb_ref[...]).astype(o_ref.dtype)


def _matmul(a, w, layer, out_dtype, tm, tn, name):
    m, k = a.shape
    n = w.shape[2]
    return pl.pallas_call(
        _matmul_kernel,
        grid=(n // tn, m // tm),
        in_specs=[pl.BlockSpec((tm, k), lambda j, i: (i, 0)),
                  pl.BlockSpec((None, k, tn), lambda j, i: (layer, 0, j))],
        out_specs=pl.BlockSpec((tm, tn), lambda j, i: (i, j)),
        out_shape=jax.ShapeDtypeStruct((m, n), out_dtype),
        scratch_shapes=[pltpu.VMEM((k, tn), BF16)],
        compiler_params=_params(("arbitrary", "arbitrary")),
        name=name,
    )(a, w)


def _res_ln(acc, x, g, b):
    z = DN_ALPHA * x + acc
    mu = jnp.mean(z, axis=-1, keepdims=True)
    zc = z - mu
    var = jnp.mean(zc * zc, axis=-1, keepdims=True)
    return zc * lax.rsqrt(var + LN_EPS) * g + b


def _mix_out_kernel(y1_ref, y2_ref, w_ref, x_ref, g_ref, b_ref, of_ref, ob_ref):
    half = y1_ref.shape[1]
    acc = _dot(y1_ref[...], w_ref[:half, :]) + _dot(y2_ref[...], w_ref[half:, :])
    out = _res_ln(acc, x_ref[...], g_ref[...], b_ref[...])
    of_ref[...] = out
    ob_ref[...] = out.astype(BF16)


def _mix_out(y1, y2, w, layer, x, g, b, tm):
    m, half = y1.shape
    d = w.shape[2]
    row = lambda i: (i, 0)
    fixed = lambda i: (0, 0)
    return pl.pallas_call(
        _mix_out_kernel,
        grid=(m // tm,),
        in_specs=[pl.BlockSpec((tm, half), row), pl.BlockSpec((tm, half), row),
                  pl.BlockSpec((None, 2 * half, d), lambda i: (layer, 0, 0),
                               pipeline_mode=pl.Buffered(1)),
                  pl.BlockSpec((tm, d), row),
                  pl.BlockSpec((1, d), fixed), pl.BlockSpec((1, d), fixed)],
        out_specs=[pl.BlockSpec((tm, d), row), pl.BlockSpec((tm, d), row)],
        out_shape=[jax.ShapeDtypeStruct((m, d), F32), jax.ShapeDtypeStruct((m, d), BF16)],
        compiler_params=_params(("parallel",)),
        name="mix_out_ln",
    )(y1, y2, w, x, g.reshape(1, d), b.reshape(1, d))


def _ffn_up_kernel(x_ref, wg_ref, wu_ref, h_ref, wgb_ref, wub_ref):
    @pl.when(pl.program_id(1) == 0)
    def _():
        wgb_ref[...] = wg_ref[...].astype(BF16)
        wub_ref[...] = wu_ref[...].astype(BF16)

    x = x_ref[...]
    gate = _dot(x, wgb_ref[...])
    up = _dot(x, wub_ref[...])
    h_ref[...] = (_silu(gate) * up).astype(h_ref.dtype)


def _ffn_up(xb, wg, wu, layer, tm, tn):
    m, k = xb.shape
    n = wg.shape[2]
    return pl.pallas_call(
        _ffn_up_kernel,
        grid=(n // tn, m // tm),
        in_specs=[pl.BlockSpec((tm, k), lambda j, i: (i, 0)),
                  pl.BlockSpec((None, k, tn), lambda j, i: (layer, 0, j)),
                  pl.BlockSpec((None, k, tn), lambda j, i: (layer, 0, j))],
        out_specs=pl.BlockSpec((tm, tn), lambda j, i: (i, j)),
        out_shape=jax.ShapeDtypeStruct((m, n), BF16),
        scratch_shapes=[pltpu.VMEM((k, tn), BF16), pltpu.VMEM((k, tn), BF16)],
        compiler_params=_params(("arbitrary", "arbitrary")),
        name="ffn_up",
    )(xb, wg, wu)


def _ffn_down_kernel(h_ref, w_ref, x_ref, g_ref, b_ref, of_ref, ob_ref):
    out = _res_ln(_dot(h_ref[...], w_ref[...]), x_ref[...], g_ref[...], b_ref[...])
    of_ref[...] = out
    ob_ref[...] = out.astype(BF16)


def _ffn_down(h, w, layer, x, g, b, tm):
    m, kk = h.shape
    d = w.shape[2]
    row = lambda i: (i, 0)
    fixed = lambda i: (0, 0)
    return pl.pallas_call(
        _ffn_down_kernel,
        grid=(m // tm,),
        in_specs=[pl.BlockSpec((tm, kk), row),
                  pl.BlockSpec((None, kk, d), lambda i: (layer, 0, 0), pipeline_mode=pl.Buffered(1)),
                  pl.BlockSpec((tm, d), row),
                  pl.BlockSpec((1, d), fixed), pl.BlockSpec((1, d), fixed)],
        out_specs=[pl.BlockSpec((tm, d), row), pl.BlockSpec((tm, d), row)],
        out_shape=[jax.ShapeDtypeStruct((m, d), F32), jax.ShapeDtypeStruct((m, d), BF16)],
        compiler_params=_params(("parallel",)),
        name="ffn_down_ln",
    )(h, w, x, g.reshape(1, d), b.reshape(1, d))


NA_GROUP = 4
NA_WINDOW = NA_GROUP + NA_WIN_ROWS
RPB_ROWS = 2 * NA_WIN_ROWS - 1
RPB_COLS = 2 * NA_WIN_COLS - 1


def _na_window_start(g, rows):
    return min(max(g * NA_GROUP - NA_WIN_ROWS // 2, 0), rows - NA_WINDOW)


def _na_patterns(rows):
    pats, ids = [], []
    for r in range(rows):
        w0 = _na_window_start(r // NA_GROUP, rows)
        r0 = min(max(r - NA_WIN_ROWS // 2, 0), rows - NA_WIN_ROWS)
        key = (r0 - w0, w0 - r + NA_WIN_ROWS - 1)
        if key not in pats:
            pats.append(key)
        ids.append(pats.index(key))
    return tuple(pats), tuple(ids)


def _na_band_kernel(rpb_ref, o_ref, *, patterns):
    h = pl.program_id(0)
    base = h * (RPB_ROWS * RPB_COLS)
    shape = (GRID_W, 2 * GRID_W)
    q = lax.broadcasted_iota(jnp.int32, shape, 0)
    lane = lax.broadcasted_iota(jnp.int32, shape, 1)
    kc = lane & (GRID_W - 1)
    second = lane >> 6
    code = kc - q + (NA_WIN_COLS - 1) + RPB_COLS * second
    start = jnp.clip(q - NA_WIN_COLS // 2, 0, GRID_W - NA_WIN_COLS)
    valid = (kc >= start) & (kc < start + NA_WIN_COLS)
    neg = jnp.full(shape, NEG_BIG, F32)
    pairs = {}

    def pair(ri):
        if ri not in pairs:
            acc = neg
            for i in range(2 * RPB_COLS):
                if 0 <= ri + i // RPB_COLS < RPB_ROWS:
                    acc = jnp.where(code == i, rpb_ref[base + ri * RPB_COLS + i], acc)
            pairs[ri] = jnp.where(valid, acc * LOG2E, NEG_BIG)
        return pairs[ri]

    for p, (lo, c) in enumerate(patterns):
        for u in range(NA_WINDOW // 2):
            in0 = lo <= 2 * u < lo + NA_WIN_ROWS
            in1 = lo <= 2 * u + 1 < lo + NA_WIN_ROWS
            if in0 and in1:
                tile = pair(2 * u + c)
            elif in0:
                tile = jnp.where(second == 0, pair(2 * u + c), NEG_BIG)
            elif in1:
                tile = jnp.where(second == 1, pair(2 * u + c), NEG_BIG)
            else:
                tile = neg
            o_ref[0, p, :, u * 2 * GRID_W:(u + 1) * 2 * GRID_W] = tile


def _na_band(rpb, patterns):
    shape = (NA_HEADS, len(patterns), GRID_W, NA_WINDOW * GRID_W)
    return pl.pallas_call(
        functools.partial(_na_band_kernel, patterns=patterns),
        grid=(NA_HEADS,),
        in_specs=[pl.BlockSpec(memory_space=pltpu.SMEM)],
        out_specs=pl.BlockSpec((1,) + shape[1:], lambda h: (h, 0, 0, 0)),
        out_shape=jax.ShapeDtypeStruct(shape, F32),
        compiler_params=_params(("arbitrary",)),
        name="na_band",
    )(rpb.reshape(-1))


def _na_kernel(q_ref, k_ref, v_ref, band_ref, o_ref, kt_ref, vb_ref, *, pattern_ids):
    t = q_ref.shape[1]
    rows = t // GRID_W
    gq = NA_GROUP * GRID_W
    wlen = NA_WINDOW * GRID_W
    n_groups = rows // NA_GROUP
    for c in range(0, t, gq):
        kt_ref[:, c:c + gq] = k_ref[0, c:c + gq, :].T.astype(BF16)
    vb_ref[...] = v_ref[0].astype(BF16)
    scale = ATTN_SCALE_LOG2

    def scores(g):
        w0 = _na_window_start(g, rows) * GRID_W
        q = (q_ref[0, g * gq:(g + 1) * gq, :] * scale).astype(BF16)
        bias = jnp.concatenate(
            [band_ref[0, pattern_ids[g * NA_GROUP + i]] for i in range(NA_GROUP)], axis=0)
        return _dot(q, kt_ref[:, w0:w0 + wlen]) + bias

    s_next = scores(0)
    for g in range(n_groups):
        s = s_next
        if g + 1 < n_groups:
            s_next = scores(g + 1)
        w0 = _na_window_start(g, rows) * GRID_W
        m = jnp.max(s, axis=-1, keepdims=True)
        p = jnp.exp2(s - m)
        l = jnp.sum(p, axis=-1, keepdims=True)
        o = _dot(p.astype(BF16), vb_ref[w0:w0 + wlen, :]) / l
        o_ref[0, g * gq:(g + 1) * gq, :] = o.astype(o_ref.dtype)


def _na_attention(proj, rpb):
    b, t, _ = proj.shape
    nh = NA_HEADS
    patterns, pattern_ids = _na_patterns(t // GRID_W)
    band = _na_band(rpb, patterns)
    blk = (1, t, HEAD_DIM)
    return pl.pallas_call(
        functools.partial(_na_kernel, pattern_ids=pattern_ids),
        grid=(nh, b),
        in_specs=[pl.BlockSpec(blk, lambda h, i: (i, 0, h)),
                  pl.BlockSpec(blk, lambda h, i: (i, 0, nh + h)),
                  pl.BlockSpec(blk, lambda h, i: (i, 0, 2 * nh + h)),
                  pl.BlockSpec((1,) + band.shape[1:], lambda h, i: (h, 0, 0, 0))],
        out_specs=pl.BlockSpec(blk, lambda h, i: (i, 0, h)),
        out_shape=jax.ShapeDtypeStruct((b, t, HALF), BF16),
        scratch_shapes=[pltpu.VMEM((HEAD_DIM, t), BF16), pltpu.VMEM((t, HEAD_DIM), BF16)],
        compiler_params=_params(("parallel", "parallel")),
        name="na_attention",
    )(proj, proj, proj, band)


HG_LEVELS = tuple(HG_CHUNK >> (i + 1) for i in range(HG_CHUNK.bit_length() - 1))
HG_DIAG = len(HG_LEVELS)


def _split2(x):
    hi = x.astype(BF16)
    mid = (x - hi.astype(F32)).astype(BF16)
    return hi, mid


def _level_codes(c, rev):
    t = lax.broadcasted_iota(jnp.int32, (c, c), 0)
    s = lax.broadcasted_iota(jnp.int32, (c, c), 1)
    x = t ^ s
    code = jnp.full((c, c), HG_DIAG, jnp.int32)
    for li in reversed(range(len(HG_LEVELS))):
        code = jnp.where(x >= HG_LEVELS[li], li, code)
    return jnp.where((s >= t) if rev else (s <= t), code, -1)


def _pivot_rows(gc, m, rev):
    c = gc.shape[0]
    blk = 2 * m
    if blk >= 8:
        piv = m if rev else m - 1
        parts = []
        for s in range(0, c, blk):
            parts.append(jnp.broadcast_to(gc[s + piv:s + piv + 1, :], (blk, gc.shape[1])))
        return parts[0] if len(parts) == 1 else jnp.concatenate(parts, axis=0)
    pos = lax.broadcasted_iota(jnp.int32, gc.shape, 0) & (blk - 1)
    piv = m if rev else m - 1
    out = gc
    for p in range(blk):
        if p == piv:
            continue
        shift = (p - piv) % c
        out = jnp.where(pos == p, pltpu.roll(gc, shift, 0), out)
    return out


def _neg_abs(x):
    bits = lax.bitcast_convert_type(x, jnp.uint32) | jnp.uint32(0x80000000)
    return lax.bitcast_convert_type(bits, F32)


def _hgrn_prepare(zs, qs, lb, tri_ref, codes_ref):
    c = qs[0].shape[0]
    sub = 8
    dirs = (False, True)
    f = [lb + (1.0 - lb) * _sigmoid(z) for z in zs]
    k = [1.0 - x for x in f]
    parts = [_split2(jnp.log2(x)) for x in f]
    gc = [_dot(tri_ref[d], parts[d][0]) + _dot(tri_ref[d], parts[d][1]) for d in range(2)]
    pos = lax.broadcasted_iota(jnp.int32, qs[0].shape, 0)

    def tiles(x):
        return [x[i:i + sub] for i in range(0, c, sub)]

    def code_tiles(d):
        return [codes_ref[d, i:i + sub, :] for i in range(0, c, sub)]

    a = []
    for d in range(2):
        diag = jnp.sum(qs[d] * k[d], axis=-1, keepdims=True)
        a.append([jnp.where(ct == HG_DIAG, dt, 0.0) for ct, dt in zip(code_tiles(d), tiles(diag))])

    def merge(d, li, prod):
        a[d] = [jnp.where(ct == li, pt, at) for ct, pt, at in zip(code_tiles(d), tiles(prod), a[d])]

    def small_level(m, decay):
        li = HG_LEVELS.index(m)
        for d, rev in enumerate(dirs):
            is_q = ((pos & m) == 0) if rev else ((pos & m) != 0)
            xl = (jnp.where(is_q, qs[d], k[d]) * decay[d]).astype(BF16)
            merge(d, li, _dot_nt(xl, xl))

    if 1 in HG_LEVELS:
        ones = jnp.ones_like(qs[0])
        dec1 = []
        for d, rev in enumerate(dirs):
            is_q = ((pos & 1) == 0) if rev else ((pos & 1) != 0)
            dec1.append(jnp.where(is_q, f[d], ones))
        small_level(1, dec1)
    if 2 in HG_LEVELS:
        p4 = pos & 3
        dec2 = []
        for d, rev in enumerate(dirs):
            nxt = pltpu.roll(f[d], c - 1, 0)
            prv = pltpu.roll(f[d], 1, 0)
            if rev:
                e = jnp.where(p4 == 0, f[d] * nxt, jnp.where(p4 == 1, f[d], jnp.where(p4 == 2, 1.0, prv)))
            else:
                e = jnp.where(p4 == 0, nxt, jnp.where(p4 == 1, 1.0, jnp.where(p4 == 2, f[d], f[d] * prv)))
            dec2.append(e)
        small_level(2, dec2)

    for li, m in enumerate(HG_LEVELS):
        if m <= 2:
            continue
        e = [jnp.exp2(_neg_abs(gc[d] - _pivot_rows(gc[d], m, rev))) for d, rev in enumerate(dirs)]
        if m < sub:
            small_level(m, e)
            continue
        for d, rev in enumerate(dirs):
            q_rows, rhs = [], []
            for s in range(0, c, 2 * m):
                lo, hi = slice(s, s + m), slice(s + m, s + 2 * m)
                q_sl, k_sl = (lo, hi) if rev else (hi, lo)
                q_rows.append(q_sl)
                kx = k[d][k_sl] * e[d][k_sl]
                rhs += [k[d][lo], kx] if rev else [kx, k[d][hi]]
            lhs = jnp.concatenate([qs[d][r] * e[d][r] for r in q_rows], axis=0).astype(BF16)
            prod = tiles(_dot_nt(lhs, jnp.concatenate(rhs, axis=0).astype(BF16)))
            ct = code_tiles(d)
            j = 0
            for r in q_rows:
                for i in range(r.start // sub, r.stop // sub):
                    a[d][i] = jnp.where(ct[i] == li, prod[j], a[d][i])
                    j += 1

    out = []
    for d, rev in enumerate(dirs):
        g_tot = gc[d][0:1, :] if rev else gc[d][c - 1:c, :]
        qg = (qs[d] * jnp.exp2(gc[d])).astype(BF16)
        kd = (k[d] * jnp.exp2(g_tot - gc[d])).astype(BF16)
        out.append((jnp.concatenate(a[d], axis=0).astype(BF16), qg, kd, jnp.exp2(g_tot)))
    return out


HG_DEC_ROWS = 8


def _row_block(idx, size):
    start = idx * size
    return pl.ds(start if isinstance(start, int) else pl.multiple_of(start, size), size)


def _hgrn_kernel(q_ref, ff_ref, fb_ref, i_ref, g_ref, lbl_ref, nw_ref, o_ref,
                 tri_ref, codes_ref, a_ref, qg_ref, kd_ref, dec_ref, qa_ref, vb_ref, of_ref, ob_ref,
                 *, layer):
    t = q_ref.shape[1]
    c = HG_CHUNK
    n = t // c
    dk = q_ref.shape[2]
    dv = i_ref.shape[2]

    lg = lbl_ref[...]
    e = jnp.exp(lg - jnp.max(lg, axis=0, keepdims=True))
    p = e / jnp.sum(e, axis=0, keepdims=True)
    lb = jnp.sum(p[:layer + 1, :], axis=0, keepdims=True) - p[0:1, :]

    row = lax.broadcasted_iota(jnp.int32, (c, c), 0)
    col = lax.broadcasted_iota(jnp.int32, (c, c), 1)
    tri_ref[0] = jnp.where(col <= row, 1.0, 0.0).astype(BF16)
    tri_ref[1] = jnp.where(col >= row, 1.0, 0.0).astype(BF16)
    codes_ref[0] = _level_codes(c, False)
    codes_ref[1] = _level_codes(c, True)

    act_rows = 256

    def activate(i, carry):
        rows = _row_block(i, act_rows)
        qa_ref[rows, :] = _silu(q_ref[0, rows, :])
        vb_ref[rows, :] = i_ref[0, rows, :].astype(BF16)
        return carry

    lax.fori_loop(0, t // act_rows, activate, 0)

    def chunk_rows(i):
        return [_row_block(i, c), _row_block(n - 1 - i, c)]

    def dec_rows(i, size):
        return [pl.ds(r.start, size) for r in (_row_block(i, HG_DEC_ROWS), _row_block(n - 1 - i, HG_DEC_ROWS))]

    def prepare(i):
        rows = chunk_rows(i)
        both = _hgrn_prepare([ff_ref[0, rows[0], :], fb_ref[0, rows[1], :]],
                             [qa_ref[rows[0], :], qa_ref[rows[1], :]], lb, tri_ref, codes_ref)
        for d, (a, qg, kd, dec) in enumerate(both):
            a_ref[d, rows[d], :] = a
            qg_ref[d, rows[d], :] = qg
            kd_ref[d, rows[d], :] = kd
            dec_ref[d, dec_rows(i, HG_DEC_ROWS)[d], :] = jnp.broadcast_to(dec, (HG_DEC_ROWS, dk))

    def recur(i, states):
        rows = chunk_rows(i)
        new = []
        for d, out_ref in enumerate((of_ref, ob_ref)):
            vb = vb_ref[rows[d], :]
            out_ref[rows[d], :] = (_dot(a_ref[d, rows[d], :], vb)
                                   + _dot_nt(qg_ref[d, rows[d], :], states[d].astype(BF16)))
            dec = dec_ref[d, dec_rows(i, 1)[d], :]
            new.append(states[d] * dec + _dot_tn(vb, kd_ref[d, rows[d], :]))
        return tuple(new)

    def body(i, states):
        states = recur(i, states)
        prepare(i + 1)
        return states

    zero = jnp.zeros((dv, dk), F32)
    prepare(0)
    recur(n - 1, lax.fori_loop(0, n - 1, body, (zero, zero), unroll=3))

    nw = nw_ref[...]
    rows = 256

    def finish(i, carry):
        off = pl.multiple_of(i * rows, rows)
        o = of_ref[pl.ds(off, rows), :] + ob_ref[pl.ds(off, rows), :]
        ms = jnp.mean(o * o, axis=-1, keepdims=True)
        y = o * lax.rsqrt(ms + RMS_EPS) * nw
        o_ref[0, pl.ds(off, rows), :] = (y * _silu(g_ref[0, pl.ds(off, rows), :])).astype(o_ref.dtype)
        return carry

    lax.fori_loop(0, t // rows, finish, 0)


def _hgrn(proj, lb_logits, norm_w, layer):
    b, t, _ = proj.shape
    nh = HG_HEADS
    n_layers = lb_logits.shape[0]
    blk = (1, t, HEAD_DIM)
    base = 3 * HALF // HEAD_DIM

    def col(k):
        return lambda i, h: (i, 0, base + k * nh + h)

    return pl.pallas_call(
        functools.partial(_hgrn_kernel, layer=layer),
        grid=(b, nh),
        in_specs=[pl.BlockSpec(blk, col(0)), pl.BlockSpec(blk, col(1)), pl.BlockSpec(blk, col(2)),
                  pl.BlockSpec(blk, col(3)), pl.BlockSpec(blk, col(4)),
                  pl.BlockSpec((n_layers, HEAD_DIM), lambda i, h: (0, h)),
                  pl.BlockSpec((1, HEAD_DIM), lambda i, h: (0, h))],
        out_specs=pl.BlockSpec(blk, lambda i, h: (i, 0, h)),
        out_shape=jax.ShapeDtypeStruct((b, t, HALF), BF16),
        scratch_shapes=[pltpu.VMEM((2, HG_CHUNK, HG_CHUNK), BF16),
                        pltpu.VMEM((2, HG_CHUNK, HG_CHUNK), jnp.int32),
                        pltpu.VMEM((2, t, HG_CHUNK), BF16),
                        pltpu.VMEM((2, t, HEAD_DIM), BF16),
                        pltpu.VMEM((2, t, HEAD_DIM), BF16),
                        pltpu.VMEM((2, t // HG_CHUNK * HG_DEC_ROWS, HEAD_DIM), F32),
                        pltpu.VMEM((t, HEAD_DIM), F32),
                        pltpu.VMEM((t, HEAD_DIM), BF16),
                        pltpu.VMEM((t, HEAD_DIM), F32), pltpu.VMEM((t, HEAD_DIM), F32)],
        compiler_params=_params(("parallel", "parallel")),
        name="hgrn2",
    )(proj, proj, proj, proj, proj, lb_logits, norm_w.reshape(1, HALF))


POOL_PAD = 8


def _pool_kernel(x_ref, w_ref, s_ref, o_ref, xp_ref):
    t = x_ref.shape[1]
    width = x_ref.shape[2]
    rows = 256
    xp_ref[0:POOL_PAD, :] = jnp.zeros((POOL_PAD, width), F32)
    xp_ref[POOL_PAD + t:, :] = jnp.zeros((POOL_PAD, width), F32)
    xp_ref[POOL_PAD:POOL_PAD + t, :] = x_ref[0]
    for ci in range(t // rows):
        t0 = ci * rows
        pos = t0 + lax.broadcasted_iota(jnp.int32, (rows, POOL_GROUP_DIM), 0)
        for gi, win in enumerate(POOL_WINDOWS):
            hw = win // 2
            c0 = gi * POOL_GROUP_DIM
            c1 = c0 + POOL_GROUP_DIM
            tot = None
            for o in range(-hw, hw):
                piece = xp_ref[POOL_PAD + t0 + o:POOL_PAD + t0 + o + rows, c0:c1]
                tot = piece if tot is None else tot + piece
            cnt = (jnp.minimum(pos + hw, t) - jnp.maximum(pos - hw, 0)).astype(F32)
            pooled = tot / cnt - x_ref[0, t0:t0 + rows, c0:c1]
            y = _dot(pooled.astype(BF16), w_ref[gi]) * s_ref[:, c0:c1]
            o_ref[0, t0:t0 + rows, c0:c1] = y.astype(o_ref.dtype)


def _pool(proj, pool_w, scale):
    b, t, _ = proj.shape
    ng = len(POOL_WINDOWS)
    return pl.pallas_call(
        _pool_kernel,
        grid=(b,),
        in_specs=[pl.BlockSpec((1, t, HALF), lambda i: (i, 0, 0)),
                  pl.BlockSpec((ng, POOL_GROUP_DIM, POOL_GROUP_DIM), lambda i: (0, 0, 0)),
                  pl.BlockSpec((1, HALF), lambda i: (0, 0))],
        out_specs=pl.BlockSpec((1, t, HALF), lambda i: (i, 0, 0)),
        out_shape=jax.ShapeDtypeStruct((b, t, HALF), BF16),
        scratch_shapes=[pltpu.VMEM((t + 2 * POOL_PAD, HALF), F32)],
        compiler_params=_params(("parallel",)),
        name="pool",
    )(proj, pool_w, scale.reshape(1, HALF))


def _rope_tables(t):
    pos = jnp.arange(t)
    row = (pos // GRID_W).astype(F32)
    col = (pos % GRID_W).astype(F32)
    n_freq = HEAD_DIM // 4
    inv = ROPE_THETA ** (-jnp.arange(n_freq, dtype=F32) / n_freq)
    ang = jnp.concatenate([row[:, None] * inv, col[:, None] * inv], axis=-1)
    cos = jnp.repeat(jnp.cos(ang), 2, axis=-1)
    sin = jnp.repeat(jnp.sin(ang), 2, axis=-1)
    sign = jnp.where(jnp.arange(HEAD_DIM) % 2 == 0, -1.0, 1.0).astype(F32)
    return cos, sin * sign


def _norm_rope(x, g, cos, sin_signed, scale):
    ms = jnp.mean(x * x, axis=-1, keepdims=True)
    xn = x * lax.rsqrt(ms + RMS_EPS) * g
    lane = lax.broadcasted_iota(jnp.int32, x.shape, 1)
    partner = jnp.where((lane & 1) == 0, pltpu.roll(xn, HEAD_DIM - 1, 1), pltpu.roll(xn, 1, 1))
    return (xn * cos + partner * sin_signed) * scale


def _qkv_prep_kernel(q_ref, k_ref, v_ref, qn_ref, kn_ref, cos_ref, sin_ref, qo_ref, kt_ref, vo_ref):
    cos = cos_ref[...]
    sin = sin_ref[...]
    for h in range(GQA_Q_HEADS):
        sl = slice(h * HEAD_DIM, (h + 1) * HEAD_DIM)
        qo_ref[:, sl] = _norm_rope(q_ref[:, sl], qn_ref[...], cos, sin,
                                   ATTN_SCALE_LOG2).astype(qo_ref.dtype)
    for h in range(GQA_KV_HEADS):
        sl = slice(h * HEAD_DIM, (h + 1) * HEAD_DIM)
        kt_ref[0, sl, :] = _norm_rope(k_ref[:, sl], kn_ref[...], cos, sin, 1.0).T.astype(kt_ref.dtype)
    vo_ref[...] = v_ref[...].astype(vo_ref.dtype)


def _qkv_prep(proj2d, q_norm, k_norm, cos, sin, b, t, tm):
    m = proj2d.shape[0]
    tb = t // tm
    kv_blk = (2 * HALF) // KV_W
    return pl.pallas_call(
        _qkv_prep_kernel,
        grid=(m // tm,),
        in_specs=[pl.BlockSpec((tm, HALF), lambda i: (i, 1)),
                  pl.BlockSpec((tm, KV_W), lambda i: (i, kv_blk)),
                  pl.BlockSpec((tm, KV_W), lambda i: (i, kv_blk + 1)),
                  pl.BlockSpec((1, HEAD_DIM), lambda i: (0, 0)),
                  pl.BlockSpec((1, HEAD_DIM), lambda i: (0, 0)),
                  pl.BlockSpec((tm, HEAD_DIM), lambda i: (i % tb, 0)),
                  pl.BlockSpec((tm, HEAD_DIM), lambda i: (i % tb, 0))],
        out_specs=[pl.BlockSpec((tm, HALF), lambda i: (i, 0)),
                   pl.BlockSpec((1, KV_W, tm), lambda i: (i // tb, 0, i % tb)),
                   pl.BlockSpec((tm, KV_W), lambda i: (i, 0))],
        out_shape=[jax.ShapeDtypeStruct((m, HALF), BF16),
                   jax.ShapeDtypeStruct((b, KV_W, t), BF16),
                   jax.ShapeDtypeStruct((m, KV_W), BF16)],
        compiler_params=_params(("parallel",)),
        name="qkv_prep",
    )(proj2d, proj2d, proj2d, q_norm.reshape(1, HEAD_DIM), k_norm.reshape(1, HEAD_DIM), cos, sin)


GQA_GROUP = GQA_Q_HEADS // GQA_KV_HEADS


def _gqa_kernel(q_ref, kt_ref, v_ref, o_ref):
    def scores(h):
        return _dot(q_ref[0, :, h * HEAD_DIM:(h + 1) * HEAD_DIM], kt_ref[0])

    s_next = scores(0)
    for h in range(GQA_GROUP):
        s = s_next
        if h + 1 < GQA_GROUP:
            s_next = scores(h + 1)
        m = jnp.max(s, axis=-1, keepdims=True)
        p = jnp.exp2(s - m)
        l = jnp.sum(p, axis=-1, keepdims=True)
        o = _dot(p.astype(BF16), v_ref[0]) / l
        o_ref[0, :, h * HEAD_DIM:(h + 1) * HEAD_DIM] = o.astype(o_ref.dtype)


def _gqa(q, kt, v, tq):
    b, t, _ = q.shape
    gw = GQA_GROUP * HEAD_DIM
    return pl.pallas_call(
        _gqa_kernel,
        grid=(b, GQA_KV_HEADS, t // tq),
        in_specs=[pl.BlockSpec((1, tq, gw), lambda i, h, j: (i, j, h)),
                  pl.BlockSpec((1, HEAD_DIM, t), lambda i, h, j: (i, h, 0)),
                  pl.BlockSpec((1, t, HEAD_DIM), lambda i, h, j: (i, 0, h))],
        out_specs=pl.BlockSpec((1, tq, gw), lambda i, h, j: (i, j, h)),
        out_shape=jax.ShapeDtypeStruct((b, t, HALF), BF16),
        compiler_params=_params(("parallel", "parallel", "parallel")),
        name="gqa",
    )(q, kt, v)


def kernel(x, ab_w_in, ab_w_out, na_rpb, hg_lb_logits, hg_norm_w, cd_w_in, cd_w_out, pool_w,
           pool_scale, d_q_norm, d_k_norm, ln_mix_g, ln_mix_b, ffn_w_gate, ffn_w_up, ffn_w_down,
           ln_ffn_g, ln_ffn_b):
    b, t, d = x.shape
    m = b * t
    cos, sin = _rope_tables(t)
    xf = x.reshape(m, d)
    xb = xf.astype(BF16)
    ab_w_out_b = ab_w_out.astype(BF16)
    cd_w_out_b = cd_w_out.astype(BF16)
    pool_w_b = pool_w.astype(BF16)
    ffn_w_down_b = ffn_w_down.astype(BF16)
    for layer in range(DEPTH):
        j = layer // 2
        if layer % 2 == 0:
            proj = _matmul(xb, ab_w_in, j, F32, 1024, 1024, "ab_in_proj").reshape(b, t, -1)
            y1 = _na_attention(proj, na_rpb[j])
            y2 = _hgrn(proj, hg_lb_logits, hg_norm_w[j], j)
            w_out = ab_w_out_b
        else:
            proj = _matmul(xb, cd_w_in, j, F32, 1024, 512, "cd_in_proj")
            y1 = _pool(proj.reshape(b, t, -1), pool_w_b[j], pool_scale[j])
            qn, kt, vn = _qkv_prep(proj, d_q_norm[j], d_k_norm[j], cos, sin, b, t, 512)
            y2 = _gqa(qn.reshape(b, t, HALF), kt, vn.reshape(b, t, KV_W), 256)
            w_out = cd_w_out_b
        xf, xb = _mix_out(y1.reshape(m, HALF), y2.reshape(m, HALF), w_out, j, xf,
                          ln_mix_g[layer], ln_mix_b[layer], 512)
        h = _ffn_up(xb, ffn_w_gate, ffn_w_up, layer, 1024, 512)
        xf, xb = _ffn_down(h, ffn_w_down_b, layer, xf, ln_ffn_g[layer], ln_ffn_b[layer], 256)
    return xf.reshape(b, t, d)
```

```python
import functools

import jax
import jax.numpy as jnp
from jax import lax
from jax.experimental import pallas as pl
from jax.experimental.pallas import tpu as pltpu

D_MODEL = 2048
DEPTH = 4
GRID_W = 64
HALF = 1024
HEAD_DIM = 128
NA_HEADS = 8
NA_WIN_ROWS = 8
NA_WIN_COLS = 16
HG_HEADS = 8
HG_CHUNK = 128
POOL_WINDOWS = (2, 4, 8, 16)
POOL_GROUP_DIM = 256
GQA_Q_HEADS = 8
GQA_KV_HEADS = 2
KV_W = GQA_KV_HEADS * HEAD_DIM
ROPE_THETA = 10000.0
FFN_HIDDEN = 5632
DN_ALPHA = (2 * DEPTH) ** 0.25
LN_EPS = 1e-5
RMS_EPS = 1e-6
NEG_BIG = -1e30
LOG2E = 1.4426950408889634
ATTN_SCALE_LOG2 = HEAD_DIM ** -0.5 * LOG2E

V7X_VMEM_LIMIT_BYTES = 56 * 1024 * 1024

F32 = jnp.float32
BF16 = jnp.bfloat16


def _params(semantics):
    return pltpu.CompilerParams(dimension_semantics=semantics,
                                vmem_limit_bytes=V7X_VMEM_LIMIT_BYTES)


def _sigmoid(x):
    return 1.0 / (1.0 + jnp.exp(-x))


def _silu(x):
    return x * _sigmoid(x)


def _dot(a, b):
    return jnp.dot(a, b, preferred_element_type=F32)


def _dot_nt(a, b):
    return lax.dot_general(a, b, (((1,), (1,)), ((), ())), preferred_element_type=F32)


def _dot_tn(a, b):
    return lax.dot_general(a, b, (((0,), (0,)), ((), ())), preferred_element_type=F32)


def _matmul_kernel(a_ref, w_ref, o_ref, wb_ref):
    @pl.when(pl.program_id(1) == 0)
    def _():
        wb_ref[...] = w_ref[...].astype(BF16)

    o_ref[...] = _dot(a_ref[...], wb_ref[...]).astype(o_ref.dtype)


def _matmul(a, w, layer, out_dtype, tm, tn, name):
    m, k = a.shape
    n = w.shape[2]
    return pl.pallas_call(
        _matmul_kernel,
        grid=(n // tn, m // tm),
        in_specs=[pl.BlockSpec((tm, k), lambda j, i: (i, 0)),
                  pl.BlockSpec((None, k, tn), lambda j, i: (layer, 0, j))],
        out_specs=pl.BlockSpec((tm, tn), lambda j, i: (i, j)),
        out_shape=jax.ShapeDtypeStruct((m, n), out_dtype),
        scratch_shapes=[pltpu.VMEM((k, tn), BF16)],
        compiler_params=_params(("arbitrary", "arbitrary")),
        name=name,
    )(a, w)


def _res_ln(acc, x, g, b):
    z = DN_ALPHA * x + acc
    mu = jnp.mean(z, axis=-1, keepdims=True)
    zc = z - mu
    var = jnp.mean(zc * zc, axis=-1, keepdims=True)
    return zc * lax.rsqrt(var + LN_EPS) * g + b


def _row_parts(tm, parts):
    size = tm // parts
    return [slice(p * size, (p + 1) * size) for p in range(parts)]


def _mix_out_kernel(y1_ref, y2_ref, w_ref, x_ref, g_ref, b_ref, of_ref, ob_ref, *, parts):
    half = y1_ref.shape[1]
    rows = _row_parts(y1_ref.shape[0], parts)
    accs = [_dot(y1_ref[r, :], w_ref[:half, :]) + _dot(y2_ref[r, :], w_ref[half:, :]) for r in rows]
    for r, acc in zip(rows, accs):
        out = _res_ln(acc, x_ref[r, :], g_ref[...], b_ref[...])
        of_ref[r, :] = out
        ob_ref[r, :] = out.astype(BF16)


def _mix_out(y1, y2, w, layer, x, g, b, tm):
    m, half = y1.shape
    d = w.shape[2]
    row = lambda i: (i, 0)
    fixed = lambda i: (0, 0)
    return pl.pallas_call(
        functools.partial(_mix_out_kernel, parts=4),
        grid=(m // tm,),
        in_specs=[pl.BlockSpec((tm, half), row), pl.BlockSpec((tm, half), row),
                  pl.BlockSpec((None, 2 * half, d), lambda i: (layer, 0, 0),
                               pipeline_mode=pl.Buffered(1)),
                  pl.BlockSpec((tm, d), row),
                  pl.BlockSpec((1, d), fixed), pl.BlockSpec((1, d), fixed)],
        out_specs=[pl.BlockSpec((tm, d), row), pl.BlockSpec((tm, d), row)],
        out_shape=[jax.ShapeDtypeStruct((m, d), F32), jax.ShapeDtypeStruct((m, d), BF16)],
        compiler_params=_params(("parallel",)),
        name="mix_out_ln",
    )(y1, y2, w, x, g.reshape(1, d), b.reshape(1, d))


def _ffn_up_kernel(x_ref, wg_ref, wu_ref, h_ref, wgb_ref, wub_ref):
    @pl.when(pl.program_id(1) == 0)
    def _():
        wgb_ref[...] = wg_ref[...].astype(BF16)
        wub_ref[...] = wu_ref[...].astype(BF16)

    x = x_ref[...]
    gate = _dot(x, wgb_ref[...])
    up = _dot(x, wub_ref[...])
    h_ref[...] = (_silu(gate) * up).astype(h_ref.dtype)


def _ffn_up(xb, wg, wu, layer, tm, tn):
    m, k = xb.shape
    n = wg.shape[2]
    return pl.pallas_call(
        _ffn_up_kernel,
        grid=(n // tn, m // tm),
        in_specs=[pl.BlockSpec((tm, k), lambda j, i: (i, 0)),
                  pl.BlockSpec((None, k, tn), lambda j, i: (layer, 0, j)),
                  pl.BlockSpec((None, k, tn), lambda j, i: (layer, 0, j))],
        out_specs=pl.BlockSpec((tm, tn), lambda j, i: (i, j)),
        out_shape=jax.ShapeDtypeStruct((m, n), BF16),
        scratch_shapes=[pltpu.VMEM((k, tn), BF16), pltpu.VMEM((k, tn), BF16)],
        compiler_params=_params(("arbitrary", "arbitrary")),
        name="ffn_up",
    )(xb, wg, wu)


def _ffn_down_kernel(h_ref, w_ref, x_ref, g_ref, b_ref, of_ref, ob_ref, *, parts):
    rows = _row_parts(h_ref.shape[0], parts)
    accs = [_dot(h_ref[r, :], w_ref[...]) for r in rows]
    for r, acc in zip(rows, accs):
        out = _res_ln(acc, x_ref[r, :], g_ref[...], b_ref[...])
        of_ref[r, :] = out
        ob_ref[r, :] = out.astype(BF16)


def _ffn_down(h, w, layer, x, g, b, tm):
    m, kk = h.shape
    d = w.shape[2]
    row = lambda i: (i, 0)
    fixed = lambda i: (0, 0)
    return pl.pallas_call(
        functools.partial(_ffn_down_kernel, parts=1),
        grid=(m // tm,),
        in_specs=[pl.BlockSpec((tm, kk), row),
                  pl.BlockSpec((None, kk, d), lambda i: (layer, 0, 0), pipeline_mode=pl.Buffered(1)),
                  pl.BlockSpec((tm, d), row),
                  pl.BlockSpec((1, d), fixed), pl.BlockSpec((1, d), fixed)],
        out_specs=[pl.BlockSpec((tm, d), row), pl.BlockSpec((tm, d), row)],
        out_shape=[jax.ShapeDtypeStruct((m, d), F32), jax.ShapeDtypeStruct((m, d), BF16)],
        compiler_params=_params(("parallel",)),
        name="ffn_down_ln",
    )(h, w, x, g.reshape(1, d), b.reshape(1, d))


NA_GROUP = 4
NA_WINDOW = NA_GROUP + NA_WIN_ROWS
RPB_ROWS = 2 * NA_WIN_ROWS - 1
RPB_COLS = 2 * NA_WIN_COLS - 1


def _na_window_start(g, rows):
    return min(max(g * NA_GROUP - NA_WIN_ROWS // 2, 0), rows - NA_WINDOW)


def _na_patterns(rows):
    pats, ids = [], []
    for r in range(rows):
        w0 = _na_window_start(r // NA_GROUP, rows)
        r0 = min(max(r - NA_WIN_ROWS // 2, 0), rows - NA_WIN_ROWS)
        key = (r0 - w0, w0 - r + NA_WIN_ROWS - 1)
        if key not in pats:
            pats.append(key)
        ids.append(pats.index(key))
    return tuple(pats), tuple(ids)


def _na_band_kernel(rpb_ref, o_ref, *, patterns):
    h = pl.program_id(0)
    base = h * (RPB_ROWS * RPB_COLS)
    shape = (GRID_W, 2 * GRID_W)
    q = lax.broadcasted_iota(jnp.int32, shape, 0)
    lane = lax.broadcasted_iota(jnp.int32, shape, 1)
    kc = lane & (GRID_W - 1)
    second = lane >> 6
    code = kc - q + (NA_WIN_COLS - 1) + RPB_COLS * second
    start = jnp.clip(q - NA_WIN_COLS // 2, 0, GRID_W - NA_WIN_COLS)
    valid = (kc >= start) & (kc < start + NA_WIN_COLS)
    neg = jnp.full(shape, NEG_BIG, F32)
    pairs = {}

    def pair(ri):
        if ri not in pairs:
            acc = neg
            for i in range(2 * RPB_COLS):
                if 0 <= ri + i // RPB_COLS < RPB_ROWS:
                    acc = jnp.where(code == i, rpb_ref[base + ri * RPB_COLS + i], acc)
            pairs[ri] = jnp.where(valid, acc * LOG2E, NEG_BIG)
        return pairs[ri]

    for p, (lo, c) in enumerate(patterns):
        for u in range(NA_WINDOW // 2):
            in0 = lo <= 2 * u < lo + NA_WIN_ROWS
            in1 = lo <= 2 * u + 1 < lo + NA_WIN_ROWS
            if in0 and in1:
                tile = pair(2 * u + c)
            elif in0:
                tile = jnp.where(second == 0, pair(2 * u + c), NEG_BIG)
            elif in1:
                tile = jnp.where(second == 1, pair(2 * u + c), NEG_BIG)
            else:
                tile = neg
            o_ref[0, p, :, u * 2 * GRID_W:(u + 1) * 2 * GRID_W] = tile


def _na_band(rpb, patterns):
    shape = (NA_HEADS, len(patterns), GRID_W, NA_WINDOW * GRID_W)
    return pl.pallas_call(
        functools.partial(_na_band_kernel, patterns=patterns),
        grid=(NA_HEADS,),
        in_specs=[pl.BlockSpec(memory_space=pltpu.SMEM)],
        out_specs=pl.BlockSpec((1,) + shape[1:], lambda h: (h, 0, 0, 0)),
        out_shape=jax.ShapeDtypeStruct(shape, F32),
        compiler_params=_params(("arbitrary",)),
        name="na_band",
    )(rpb.reshape(-1))


def _na_kernel(q_ref, k_ref, v_ref, band_ref, o_ref, kt_ref, vb_ref, *, pattern_ids):
    t = q_ref.shape[1]
    rows = t // GRID_W
    gq = NA_GROUP * GRID_W
    wlen = NA_WINDOW * GRID_W
    n_groups = rows // NA_GROUP
    for c in range(0, t, gq):
        kt_ref[:, c:c + gq] = k_ref[0, c:c + gq, :].T.astype(BF16)
    vb_ref[:, :HEAD_DIM] = v_ref[0].astype(BF16)
    vb_ref[:, HEAD_DIM:] = jnp.ones((t, HEAD_DIM), BF16)
    scale = ATTN_SCALE_LOG2

    def scores(g):
        w0 = _na_window_start(g, rows) * GRID_W
        q = (q_ref[0, g * gq:(g + 1) * gq, :] * scale).astype(BF16)
        bias = jnp.concatenate(
            [band_ref[0, pattern_ids[g * NA_GROUP + i]] for i in range(NA_GROUP)], axis=0)
        return _dot(q, kt_ref[:, w0:w0 + wlen]) + bias

    s_next = scores(0)
    for g in range(n_groups):
        s = s_next
        if g + 1 < n_groups:
            s_next = scores(g + 1)
        w0 = _na_window_start(g, rows) * GRID_W
        m = jnp.max(s, axis=-1, keepdims=True)
        p = jnp.exp2(s - m).astype(BF16)
        ol = _dot(p, vb_ref[w0:w0 + wlen, :])
        o = ol[:, :HEAD_DIM] / ol[:, HEAD_DIM:]
        o_ref[0, g * gq:(g + 1) * gq, :] = o.astype(o_ref.dtype)


def _na_attention(proj, rpb):
    b, t, _ = proj.shape
    nh = NA_HEADS
    patterns, pattern_ids = _na_patterns(t // GRID_W)
    band = _na_band(rpb, patterns)
    blk = (1, t, HEAD_DIM)
    return pl.pallas_call(
        functools.partial(_na_kernel, pattern_ids=pattern_ids),
        grid=(nh, b),
        in_specs=[pl.BlockSpec(blk, lambda h, i: (i, 0, h)),
                  pl.BlockSpec(blk, lambda h, i: (i, 0, nh + h)),
                  pl.BlockSpec(blk, lambda h, i: (i, 0, 2 * nh + h)),
                  pl.BlockSpec((1,) + band.shape[1:], lambda h, i: (h, 0, 0, 0))],
        out_specs=pl.BlockSpec(blk, lambda h, i: (i, 0, h)),
        out_shape=jax.ShapeDtypeStruct((b, t, HALF), BF16),
        scratch_shapes=[pltpu.VMEM((HEAD_DIM, t), BF16), pltpu.VMEM((t, 2 * HEAD_DIM), BF16)],
        compiler_params=_params(("parallel", "parallel")),
        name="na_attention",
    )(proj, proj, proj, band)


HG_LEVELS = tuple(HG_CHUNK >> (i + 1) for i in range(HG_CHUNK.bit_length() - 1))
HG_DIAG = len(HG_LEVELS)


def _split2(x):
    hi = x.astype(BF16)
    mid = (x - hi.astype(F32)).astype(BF16)
    return hi, mid


def _level_codes(c, rev):
    t = lax.broadcasted_iota(jnp.int32, (c, c), 0)
    s = lax.broadcasted_iota(jnp.int32, (c, c), 1)
    x = t ^ s
    code = jnp.full((c, c), HG_DIAG, jnp.int32)
    for li in reversed(range(len(HG_LEVELS))):
        code = jnp.where(x >= HG_LEVELS[li], li, code)
    return jnp.where((s >= t) if rev else (s <= t), code, -1)


def _pivot_rows(gc, m, rev):
    c = gc.shape[0]
    blk = 2 * m
    if blk >= 8:
        piv = m if rev else m - 1
        parts = []
        for s in range(0, c, blk):
            parts.append(jnp.broadcast_to(gc[s + piv:s + piv + 1, :], (blk, gc.shape[1])))
        return parts[0] if len(parts) == 1 else jnp.concatenate(parts, axis=0)
    pos = lax.broadcasted_iota(jnp.int32, gc.shape, 0) & (blk - 1)
    piv = m if rev else m - 1
    out = gc
    for p in range(blk):
        if p == piv:
            continue
        shift = (p - piv) % c
        out = jnp.where(pos == p, pltpu.roll(gc, shift, 0), out)
    return out


def _neg_abs(x):
    bits = lax.bitcast_convert_type(x, jnp.uint32) | jnp.uint32(0x80000000)
    return lax.bitcast_convert_type(bits, F32)


def _hgrn_prepare(zs, qs, lb, tri_ref, codes_ref):
    c = qs[0].shape[0]
    sub = 8
    dirs = (False, True)
    f = [lb + (1.0 - lb) * _sigmoid(z) for z in zs]
    k = [1.0 - x for x in f]
    parts = [_split2(jnp.log2(x)) for x in f]
    gc = [_dot(tri_ref[d], parts[d][0]) + _dot(tri_ref[d], parts[d][1]) for d in range(2)]
    pos = lax.broadcasted_iota(jnp.int32, qs[0].shape, 0)

    def tiles(x):
        return [x[i:i + sub] for i in range(0, c, sub)]

    def code_tiles(d):
        return [codes_ref[d, i:i + sub, :] for i in range(0, c, sub)]

    a = []
    for d in range(2):
        diag = jnp.sum(qs[d] * k[d], axis=-1, keepdims=True)
        a.append([jnp.where(ct == HG_DIAG, dt, 0.0) for ct, dt in zip(code_tiles(d), tiles(diag))])

    def merge(d, li, prod):
        a[d] = [jnp.where(ct == li, pt, at) for ct, pt, at in zip(code_tiles(d), tiles(prod), a[d])]

    def small_level(m, decay):
        li = HG_LEVELS.index(m)
        for d, rev in enumerate(dirs):
            is_q = ((pos & m) == 0) if rev else ((pos & m) != 0)
            xl = (jnp.where(is_q, qs[d], k[d]) * decay[d]).astype(BF16)
            merge(d, li, _dot_nt(xl, xl))

    if 1 in HG_LEVELS:
        ones = jnp.ones_like(qs[0])
        dec1 = []
        for d, rev in enumerate(dirs):
            is_q = ((pos & 1) == 0) if rev else ((pos & 1) != 0)
            dec1.append(jnp.where(is_q, f[d], ones))
        small_level(1, dec1)
    if 2 in HG_LEVELS:
        p4 = pos & 3
        dec2 = []
        for d, rev in enumerate(dirs):
            nxt = pltpu.roll(f[d], c - 1, 0)
            prv = pltpu.roll(f[d], 1, 0)
            if rev:
                e = jnp.where(p4 == 0, f[d] * nxt, jnp.where(p4 == 1, f[d], jnp.where(p4 == 2, 1.0, prv)))
            else:
                e = jnp.where(p4 == 0, nxt, jnp.where(p4 == 1, 1.0, jnp.where(p4 == 2, f[d], f[d] * prv)))
            dec2.append(e)
        small_level(2, dec2)

    for li, m in enumerate(HG_LEVELS):
        if m <= 2:
            continue
        e = [jnp.exp2(_neg_abs(gc[d] - _pivot_rows(gc[d], m, rev))) for d, rev in enumerate(dirs)]
        if m < sub:
            small_level(m, e)
            continue
        for d, rev in enumerate(dirs):
            q_rows, rhs = [], []
            for s in range(0, c, 2 * m):
                lo, hi = slice(s, s + m), slice(s + m, s + 2 * m)
                q_sl, k_sl = (lo, hi) if rev else (hi, lo)
                q_rows.append(q_sl)
                kx = k[d][k_sl] * e[d][k_sl]
                rhs += [k[d][lo], kx] if rev else [kx, k[d][hi]]
            lhs = jnp.concatenate([qs[d][r] * e[d][r] for r in q_rows], axis=0).astype(BF16)
            prod = tiles(_dot_nt(lhs, jnp.concatenate(rhs, axis=0).astype(BF16)))
            ct = code_tiles(d)
            j = 0
            for r in q_rows:
                for i in range(r.start // sub, r.stop // sub):
                    a[d][i] = jnp.where(ct[i] == li, prod[j], a[d][i])
                    j += 1

    out = []
    for d, rev in enumerate(dirs):
        g_tot = gc[d][0:1, :] if rev else gc[d][c - 1:c, :]
        qg = (qs[d] * jnp.exp2(gc[d])).astype(BF16)
        kd = (k[d] * jnp.exp2(g_tot - gc[d])).astype(BF16)
        out.append((jnp.concatenate(a[d], axis=0).astype(BF16), qg, kd, jnp.exp2(g_tot)))
    return out


HG_DEC_ROWS = 8


def _row_block(idx, size):
    start = idx * size
    return pl.ds(start if isinstance(start, int) else pl.multiple_of(start, size), size)


def _hgrn_kernel(q_ref, ff_ref, fb_ref, i_ref, g_ref, lbl_ref, nw_ref, o_ref,
                 tri_ref, codes_ref, a_ref, qg_ref, kd_ref, dec_ref, qa_ref, vb_ref, of_ref, ob_ref,
                 *, layer):
    t = q_ref.shape[1]
    c = HG_CHUNK
    n = t // c
    dk = q_ref.shape[2]
    dv = i_ref.shape[2]

    lg = lbl_ref[...]
    e = jnp.exp(lg - jnp.max(lg, axis=0, keepdims=True))
    p = e / jnp.sum(e, axis=0, keepdims=True)
    lb = jnp.sum(p[:layer + 1, :], axis=0, keepdims=True) - p[0:1, :]

    row = lax.broadcasted_iota(jnp.int32, (c, c), 0)
    col = lax.broadcasted_iota(jnp.int32, (c, c), 1)
    tri_ref[0] = jnp.where(col <= row, 1.0, 0.0).astype(BF16)
    tri_ref[1] = jnp.where(col >= row, 1.0, 0.0).astype(BF16)
    codes_ref[0] = _level_codes(c, False)
    codes_ref[1] = _level_codes(c, True)

    act_rows = 256

    def activate(i, carry):
        rows = _row_block(i, act_rows)
        qa_ref[rows, :] = _silu(q_ref[0, rows, :])
        vb_ref[rows, :] = i_ref[0, rows, :].astype(BF16)
        return carry

    lax.fori_loop(0, t // act_rows, activate, 0)

    def chunk_rows(i):
        return [_row_block(i, c), _row_block(n - 1 - i, c)]

    def dec_rows(i, size):
        return [pl.ds(r.start, size) for r in (_row_block(i, HG_DEC_ROWS), _row_block(n - 1 - i, HG_DEC_ROWS))]

    def prepare(i):
        rows = chunk_rows(i)
        both = _hgrn_prepare([ff_ref[0, rows[0], :], fb_ref[0, rows[1], :]],
                             [qa_ref[rows[0], :], qa_ref[rows[1], :]], lb, tri_ref, codes_ref)
        for d, (a, qg, kd, dec) in enumerate(both):
            a_ref[d, rows[d], :] = a
            qg_ref[d, rows[d], :] = qg
            kd_ref[d, rows[d], :] = kd
            dec_ref[d, dec_rows(i, HG_DEC_ROWS)[d], :] = jnp.broadcast_to(dec, (HG_DEC_ROWS, dk))

    def recur(i, states):
        rows = chunk_rows(i)
        new = []
        for d, out_ref in enumerate((of_ref, ob_ref)):
            vb = vb_ref[rows[d], :]
            out_ref[rows[d], :] = (_dot(a_ref[d, rows[d], :], vb)
                                   + _dot_nt(qg_ref[d, rows[d], :], states[d].astype(BF16)))
            dec = dec_ref[d, dec_rows(i, 1)[d], :]
            new.append(states[d] * dec + _dot_tn(vb, kd_ref[d, rows[d], :]))
        return tuple(new)

    def body(i, states):
        states = recur(i, states)
        prepare(i + 1)
        return states

    zero = jnp.zeros((dv, dk), F32)
    prepare(0)
    recur(n - 1, lax.fori_loop(0, n - 1, body, (zero, zero), unroll=3))

    nw = nw_ref[...]
    rows = 256

    def finish(i, carry):
        off = pl.multiple_of(i * rows, rows)
        o = of_ref[pl.ds(off, rows), :] + ob_ref[pl.ds(off, rows), :]
        ms = jnp.mean(o * o, axis=-1, keepdims=True)
        y = o * lax.rsqrt(ms + RMS_EPS) * nw
        o_ref[0, pl.ds(off, rows), :] = (y * _silu(g_ref[0, pl.ds(off, rows), :])).astype(o_ref.dtype)
        return carry

    lax.fori_loop(0, t // rows, finish, 0, unroll=4)


def _hgrn(proj, lb_logits, norm_w, layer):
    b, t, _ = proj.shape
    nh = HG_HEADS
    n_layers = lb_logits.shape[0]
    blk = (1, t, HEAD_DIM)
    base = 3 * HALF // HEAD_DIM

    def col(k):
        return lambda i, h: (i, 0, base + k * nh + h)

    return pl.pallas_call(
        functools.partial(_hgrn_kernel, layer=layer),
        grid=(b, nh),
        in_specs=[pl.BlockSpec(blk, col(0)), pl.BlockSpec(blk, col(1)), pl.BlockSpec(blk, col(2)),
                  pl.BlockSpec(blk, col(3)), pl.BlockSpec(blk, col(4)),
                  pl.BlockSpec((n_layers, HEAD_DIM), lambda i, h: (0, h)),
                  pl.BlockSpec((1, HEAD_DIM), lambda i, h: (0, h))],
        out_specs=pl.BlockSpec(blk, lambda i, h: (i, 0, h)),
        out_shape=jax.ShapeDtypeStruct((b, t, HALF), BF16),
        scratch_shapes=[pltpu.VMEM((2, HG_CHUNK, HG_CHUNK), BF16),
                        pltpu.VMEM((2, HG_CHUNK, HG_CHUNK), jnp.int32),
                        pltpu.VMEM((2, t, HG_CHUNK), BF16),
                        pltpu.VMEM((2, t, HEAD_DIM), BF16),
                        pltpu.VMEM((2, t, HEAD_DIM), BF16),
                        pltpu.VMEM((2, t // HG_CHUNK * HG_DEC_ROWS, HEAD_DIM), F32),
                        pltpu.VMEM((t, HEAD_DIM), F32),
                        pltpu.VMEM((t, HEAD_DIM), BF16),
                        pltpu.VMEM((t, HEAD_DIM), F32), pltpu.VMEM((t, HEAD_DIM), F32)],
        compiler_params=_params(("parallel", "parallel")),
        name="hgrn2",
    )(proj, proj, proj, proj, proj, lb_logits, norm_w.reshape(1, HALF))


POOL_PAD = 8


def _pool_kernel(x_ref, w_ref, s_ref, o_ref, xp_ref):
    t = x_ref.shape[1]
    width = x_ref.shape[2]
    rows = 256
    xp_ref[0:POOL_PAD, :] = jnp.zeros((POOL_PAD, width), F32)
    xp_ref[POOL_PAD + t:, :] = jnp.zeros((POOL_PAD, width), F32)
    xp_ref[POOL_PAD:POOL_PAD + t, :] = x_ref[0]
    for ci in range(t // rows):
        t0 = ci * rows
        pos = t0 + lax.broadcasted_iota(jnp.int32, (rows, POOL_GROUP_DIM), 0)
        for gi, win in enumerate(POOL_WINDOWS):
            hw = win // 2
            c0 = gi * POOL_GROUP_DIM
            c1 = c0 + POOL_GROUP_DIM
            tot = None
            for o in range(-hw, hw):
                piece = xp_ref[POOL_PAD + t0 + o:POOL_PAD + t0 + o + rows, c0:c1]
                tot = piece if tot is None else tot + piece
            cnt = (jnp.minimum(pos + hw, t) - jnp.maximum(pos - hw, 0)).astype(F32)
            pooled = tot / cnt - x_ref[0, t0:t0 + rows, c0:c1]
            y = _dot(pooled.astype(BF16), w_ref[gi]) * s_ref[:, c0:c1]
            o_ref[0, t0:t0 + rows, c0:c1] = y.astype(o_ref.dtype)


def _pool(proj, pool_w, scale):
    b, t, _ = proj.shape
    ng = len(POOL_WINDOWS)
    return pl.pallas_call(
        _pool_kernel,
        grid=(b,),
        in_specs=[pl.BlockSpec((1, t, HALF), lambda i: (i, 0, 0)),
                  pl.BlockSpec((ng, POOL_GROUP_DIM, POOL_GROUP_DIM), lambda i: (0, 0, 0)),
                  pl.BlockSpec((1, HALF), lambda i: (0, 0))],
        out_specs=pl.BlockSpec((1, t, HALF), lambda i: (i, 0, 0)),
        out_shape=jax.ShapeDtypeStruct((b, t, HALF), BF16),
        scratch_shapes=[pltpu.VMEM((t + 2 * POOL_PAD, HALF), F32)],
        compiler_params=_params(("parallel",)),
        name="pool",
    )(proj, pool_w, scale.reshape(1, HALF))


def _rope_tables(t):
    pos = jnp.arange(t)
    row = (pos // GRID_W).astype(F32)
    col = (pos % GRID_W).astype(F32)
    n_freq = HEAD_DIM // 4
    inv = ROPE_THETA ** (-jnp.arange(n_freq, dtype=F32) / n_freq)
    ang = jnp.concatenate([row[:, None] * inv, col[:, None] * inv], axis=-1)
    cos = jnp.repeat(jnp.cos(ang), 2, axis=-1)
    sin = jnp.repeat(jnp.sin(ang), 2, axis=-1)
    sign = jnp.where(jnp.arange(HEAD_DIM) % 2 == 0, -1.0, 1.0).astype(F32)
    return cos, sin * sign


def _norm_rope(x, g, cos, sin_signed, scale):
    ms = jnp.mean(x * x, axis=-1, keepdims=True)
    xn = x * lax.rsqrt(ms + RMS_EPS) * g
    lane = lax.broadcasted_iota(jnp.int32, x.shape, 1)
    partner = jnp.where((lane & 1) == 0, pltpu.roll(xn, HEAD_DIM - 1, 1), pltpu.roll(xn, 1, 1))
    return (xn * cos + partner * sin_signed) * scale


def _qkv_prep_kernel(q_ref, k_ref, v_ref, qn_ref, kn_ref, cos_ref, sin_ref, qo_ref, kt_ref, vo_ref):
    cos = cos_ref[...]
    sin = sin_ref[...]
    for h in range(GQA_Q_HEADS):
        sl = slice(h * HEAD_DIM, (h + 1) * HEAD_DIM)
        qo_ref[:, sl] = _norm_rope(q_ref[:, sl], qn_ref[...], cos, sin,
                                   ATTN_SCALE_LOG2).astype(qo_ref.dtype)
    for h in range(GQA_KV_HEADS):
        sl = slice(h * HEAD_DIM, (h + 1) * HEAD_DIM)
        kt_ref[0, sl, :] = _norm_rope(k_ref[:, sl], kn_ref[...], cos, sin, 1.0).T.astype(kt_ref.dtype)
    ones = jnp.ones((v_ref.shape[0], HEAD_DIM), vo_ref.dtype)
    for h in range(GQA_KV_HEADS):
        vo_ref[:, 2 * h * HEAD_DIM:(2 * h + 1) * HEAD_DIM] = (
            v_ref[:, h * HEAD_DIM:(h + 1) * HEAD_DIM].astype(vo_ref.dtype))
        vo_ref[:, (2 * h + 1) * HEAD_DIM:(2 * h + 2) * HEAD_DIM] = ones


def _qkv_prep(proj2d, q_norm, k_norm, cos, sin, b, t, tm):
    m = proj2d.shape[0]
    tb = t // tm
    kv_blk = (2 * HALF) // KV_W
    return pl.pallas_call(
        _qkv_prep_kernel,
        grid=(m // tm,),
        in_specs=[pl.BlockSpec((tm, HALF), lambda i: (i, 1)),
                  pl.BlockSpec((tm, KV_W), lambda i: (i, kv_blk)),
                  pl.BlockSpec((tm, KV_W), lambda i: (i, kv_blk + 1)),
                  pl.BlockSpec((1, HEAD_DIM), lambda i: (0, 0)),
                  pl.BlockSpec((1, HEAD_DIM), lambda i: (0, 0)),
                  pl.BlockSpec((tm, HEAD_DIM), lambda i: (i % tb, 0)),
                  pl.BlockSpec((tm, HEAD_DIM), lambda i: (i % tb, 0))],
        out_specs=[pl.BlockSpec((tm, HALF), lambda i: (i, 0)),
                   pl.BlockSpec((1, KV_W, tm), lambda i: (i // tb, 0, i % tb)),
                   pl.BlockSpec((tm, 2 * KV_W), lambda i: (i, 0))],
        out_shape=[jax.ShapeDtypeStruct((m, HALF), BF16),
                   jax.ShapeDtypeStruct((b, KV_W, t), BF16),
                   jax.ShapeDtypeStruct((m, 2 * KV_W), BF16)],
        compiler_params=_params(("parallel",)),
        name="qkv_prep",
    )(proj2d, proj2d, proj2d, q_norm.reshape(1, HEAD_DIM), k_norm.reshape(1, HEAD_DIM), cos, sin)


GQA_GROUP = GQA_Q_HEADS // GQA_KV_HEADS


def _gqa_kernel(q_ref, kt_ref, v_ref, o_ref):
    def scores(h):
        return _dot(q_ref[0, :, h * HEAD_DIM:(h + 1) * HEAD_DIM], kt_ref[0])

    s_next = scores(0)
    for h in range(GQA_GROUP):
        s = s_next
        if h + 1 < GQA_GROUP:
            s_next = scores(h + 1)
        m = jnp.max(s, axis=-1, keepdims=True)
        p = jnp.exp2(s - m).astype(BF16)
        ol = _dot(p, v_ref[0])
        o = ol[:, :HEAD_DIM] / ol[:, HEAD_DIM:]
        o_ref[0, :, h * HEAD_DIM:(h + 1) * HEAD_DIM] = o.astype(o_ref.dtype)


def _gqa(q, kt, v, tq):
    b, t, _ = q.shape
    gw = GQA_GROUP * HEAD_DIM
    return pl.pallas_call(
        _gqa_kernel,
        grid=(b, GQA_KV_HEADS, t // tq),
        in_specs=[pl.BlockSpec((1, tq, gw), lambda i, h, j: (i, j, h)),
                  pl.BlockSpec((1, HEAD_DIM, t), lambda i, h, j: (i, h, 0)),
                  pl.BlockSpec((1, t, 2 * HEAD_DIM), lambda i, h, j: (i, 0, h))],
        out_specs=pl.BlockSpec((1, tq, gw), lambda i, h, j: (i, j, h)),
        out_shape=jax.ShapeDtypeStruct((b, t, HALF), BF16),
        compiler_params=_params(("parallel", "parallel", "parallel")),
        name="gqa",
    )(q, kt, v)


def kernel(x, ab_w_in, ab_w_out, na_rpb, hg_lb_logits, hg_norm_w, cd_w_in, cd_w_out, pool_w,
           pool_scale, d_q_norm, d_k_norm, ln_mix_g, ln_mix_b, ffn_w_gate, ffn_w_up, ffn_w_down,
           ln_ffn_g, ln_ffn_b):
    b, t, d = x.shape
    m = b * t
    cos, sin = _rope_tables(t)
    xf = x.reshape(m, d)
    xb = xf.astype(BF16)
    ab_w_out_b = ab_w_out.astype(BF16)
    cd_w_out_b = cd_w_out.astype(BF16)
    pool_w_b = pool_w.astype(BF16)
    ffn_w_down_b = ffn_w_down.astype(BF16)
    for layer in range(DEPTH):
        j = layer // 2
        if layer % 2 == 0:
            proj = _matmul(xb, ab_w_in, j, F32, 1024, 1024, "ab_in_proj").reshape(b, t, -1)
            y1 = _na_attention(proj, na_rpb[j])
            y2 = _hgrn(proj, hg_lb_logits, hg_norm_w[j], j)
            w_out = ab_w_out_b
        else:
            proj = _matmul(xb, cd_w_in, j, F32, 1024, 1280, "cd_in_proj")
            y1 = _pool(proj.reshape(b, t, -1), pool_w_b[j], pool_scale[j])
            qn, kt, vn = _qkv_prep(proj, d_q_norm[j], d_k_norm[j], cos, sin, b, t, 512)
            y2 = _gqa(qn.reshape(b, t, HALF), kt, vn.reshape(b, t, 2 * KV_W), 512)
            w_out = cd_w_out_b
        xf, xb = _mix_out(y1.reshape(m, HALF), y2.reshape(m, HALF), w_out, j, xf,
                          ln_mix_g[layer], ln_mix_b[layer], 512)
        h = _ffn_up(xb, ffn_w_gate, ffn_w_up, layer, 2048, 512)
        xf, xb = _ffn_down(h, ffn_w_down_b, layer, xf, ln_ffn_g[layer], ln_ffn_b[layer], 256)
    return xf.reshape(b, t, d)
```

```python
import functools

import jax
import jax.numpy as jnp
from jax import lax
from jax.experimental import pallas as pl
from jax.experimental.pallas import tpu as pltpu

D_MODEL = 2048
DEPTH = 4
GRID_W = 64
HALF = 1024
HEAD_DIM = 128
NA_HEADS = 8
NA_WIN_ROWS = 8
NA_WIN_COLS = 16
HG_HEADS = 8
HG_CHUNK = 128
POOL_WINDOWS = (2, 4, 8, 16)
POOL_GROUP_DIM = 256
GQA_Q_HEADS = 8
GQA_KV_HEADS = 2
KV_W = GQA_KV_HEADS * HEAD_DIM
ROPE_THETA = 10000.0
FFN_HIDDEN = 5632
DN_ALPHA = (2 * DEPTH) ** 0.25
LN_EPS = 1e-5
RMS_EPS = 1e-6
NEG_BIG = -1e30
LOG2E = 1.4426950408889634
ATTN_SCALE_LOG2 = HEAD_DIM ** -0.5 * LOG2E

V7X_VMEM_LIMIT_BYTES = 56 * 1024 * 1024

F32 = jnp.float32
BF16 = jnp.bfloat16


def _params(semantics):
    return pltpu.CompilerParams(dimension_semantics=semantics,
                                vmem_limit_bytes=V7X_VMEM_LIMIT_BYTES)


def _sigmoid(x):
    return 1.0 / (1.0 + jnp.exp(-x))


def _silu(x):
    return x * _sigmoid(x)


def _dot(a, b):
    return jnp.dot(a, b, preferred_element_type=F32)


def _dot_nt(a, b):
    return lax.dot_general(a, b, (((1,), (1,)), ((), ())), preferred_element_type=F32)


def _dot_tn(a, b):
    return lax.dot_general(a, b, (((0,), (0,)), ((), ())), preferred_element_type=F32)


def _matmul_kernel(a_ref, w_ref, o_ref, wb_ref):
    @pl.when(pl.program_id(1) == 0)
    def _():
        wb_ref[...] = w_ref[...].astype(BF16)

    o_ref[...] = _dot(a_ref[...], wb_ref[...]).astype(o_ref.dtype)


def _matmul(a, w, layer, out_dtype, tm, tn, name):
    m, k = a.shape
    n = w.shape[2]
    return pl.pallas_call(
        _matmul_kernel,
        grid=(n // tn, m // tm),
        in_specs=[pl.BlockSpec((tm, k), lambda j, i: (i, 0)),
                  pl.BlockSpec((None, k, tn), lambda j, i: (layer, 0, j))],
        out_specs=pl.BlockSpec((tm, tn), lambda j, i: (i, j)),
        out_shape=jax.ShapeDtypeStruct((m, n), out_dtype),
        scratch_shapes=[pltpu.VMEM((k, tn), BF16)],
        compiler_params=_params(("arbitrary", "arbitrary")),
        name=name,
    )(a, w)


def _res_ln(acc, x, g, b):
    z = DN_ALPHA * x + acc
    mu = jnp.mean(z, axis=-1, keepdims=True)
    zc = z - mu
    var = jnp.mean(zc * zc, axis=-1, keepdims=True)
    return zc * lax.rsqrt(var + LN_EPS) * g + b


def _row_parts(tm, parts):
    size = tm // parts
    return [slice(p * size, (p + 1) * size) for p in range(parts)]


def _mix_out_kernel(y1_ref, y2_ref, w_ref, x_ref, g_ref, b_ref, of_ref, ob_ref, *, parts):
    half = y1_ref.shape[1]
    rows = _row_parts(y1_ref.shape[0], parts)
    accs = [_dot(y1_ref[r, :], w_ref[:half, :]) + _dot(y2_ref[r, :], w_ref[half:, :]) for r in rows]
    for r, acc in zip(rows, accs):
        out = _res_ln(acc, x_ref[r, :], g_ref[...], b_ref[...])
        of_ref[r, :] = out
        ob_ref[r, :] = out.astype(BF16)


def _mix_out(y1, y2, w, layer, x, g, b, tm):
    m, half = y1.shape
    d = w.shape[2]
    row = lambda i: (i, 0)
    fixed = lambda i: (0, 0)
    return pl.pallas_call(
        functools.partial(_mix_out_kernel, parts=4),
        grid=(m // tm,),
        in_specs=[pl.BlockSpec((tm, half), row), pl.BlockSpec((tm, half), row),
                  pl.BlockSpec((None, 2 * half, d), lambda i: (layer, 0, 0),
                               pipeline_mode=pl.Buffered(1)),
                  pl.BlockSpec((tm, d), row),
                  pl.BlockSpec((1, d), fixed), pl.BlockSpec((1, d), fixed)],
        out_specs=[pl.BlockSpec((tm, d), row), pl.BlockSpec((tm, d), row)],
        out_shape=[jax.ShapeDtypeStruct((m, d), F32), jax.ShapeDtypeStruct((m, d), BF16)],
        compiler_params=_params(("parallel",)),
        name="mix_out_ln",
    )(y1, y2, w, x, g.reshape(1, d), b.reshape(1, d))


def _ffn_up_kernel(x_ref, wg_ref, wu_ref, h_ref, wgb_ref, wub_ref):
    @pl.when(pl.program_id(1) == 0)
    def _():
        wgb_ref[...] = wg_ref[...].astype(BF16)
        wub_ref[...] = wu_ref[...].astype(BF16)

    x = x_ref[...]
    gate = _dot(x, wgb_ref[...])
    up = _dot(x, wub_ref[...])
    h_ref[...] = (_silu(gate) * up).astype(h_ref.dtype)


def _ffn_up(xb, wg, wu, layer, tm, tn):
    m, k = xb.shape
    n = wg.shape[2]
    return pl.pallas_call(
        _ffn_up_kernel,
        grid=(n // tn, m // tm),
        in_specs=[pl.BlockSpec((tm, k), lambda j, i: (i, 0)),
                  pl.BlockSpec((None, k, tn), lambda j, i: (layer, 0, j)),
                  pl.BlockSpec((None, k, tn), lambda j, i: (layer, 0, j))],
        out_specs=pl.BlockSpec((tm, tn), lambda j, i: (i, j)),
        out_shape=jax.ShapeDtypeStruct((m, n), BF16),
        scratch_shapes=[pltpu.VMEM((k, tn), BF16), pltpu.VMEM((k, tn), BF16)],
        compiler_params=_params(("arbitrary", "arbitrary")),
        name="ffn_up",
    )(xb, wg, wu)


def _ffn_down_kernel(h_ref, w_ref, x_ref, g_ref, b_ref, of_ref, ob_ref, *, parts):
    rows = _row_parts(h_ref.shape[0], parts)
    accs = [_dot(h_ref[r, :], w_ref[...]) for r in rows]
    for r, acc in zip(rows, accs):
        out = _res_ln(acc, x_ref[r, :], g_ref[...], b_ref[...])
        of_ref[r, :] = out
        ob_ref[r, :] = out.astype(BF16)


def _ffn_down(h, w, layer, x, g, b, tm):
    m, kk = h.shape
    d = w.shape[2]
    row = lambda i: (i, 0)
    fixed = lambda i: (0, 0)
    return pl.pallas_call(
        functools.partial(_ffn_down_kernel, parts=1),
        grid=(m // tm,),
        in_specs=[pl.BlockSpec((tm, kk), row),
                  pl.BlockSpec((None, kk, d), lambda i: (layer, 0, 0), pipeline_mode=pl.Buffered(1)),
                  pl.BlockSpec((tm, d), row),
                  pl.BlockSpec((1, d), fixed), pl.BlockSpec((1, d), fixed)],
        out_specs=[pl.BlockSpec((tm, d), row), pl.BlockSpec((tm, d), row)],
        out_shape=[jax.ShapeDtypeStruct((m, d), F32), jax.ShapeDtypeStruct((m, d), BF16)],
        compiler_params=_params(("parallel",)),
        name="ffn_down_ln",
    )(h, w, x, g.reshape(1, d), b.reshape(1, d))


NA_GROUP = 4
NA_WINDOW = NA_GROUP + NA_WIN_ROWS
RPB_ROWS = 2 * NA_WIN_ROWS - 1
RPB_COLS = 2 * NA_WIN_COLS - 1


def _na_window_start(g, rows):
    return min(max(g * NA_GROUP - NA_WIN_ROWS // 2, 0), rows - NA_WINDOW)


def _na_patterns(rows):
    pats, ids = [], []
    for r in range(rows):
        w0 = _na_window_start(r // NA_GROUP, rows)
        r0 = min(max(r - NA_WIN_ROWS // 2, 0), rows - NA_WIN_ROWS)
        key = (r0 - w0, w0 - r + NA_WIN_ROWS - 1)
        if key not in pats:
            pats.append(key)
        ids.append(pats.index(key))
    return tuple(pats), tuple(ids)


def _na_band_kernel(rpb_ref, o_ref, *, patterns):
    h = pl.program_id(0)
    base = h * (RPB_ROWS * RPB_COLS)
    shape = (GRID_W, 2 * GRID_W)
    q = lax.broadcasted_iota(jnp.int32, shape, 0)
    lane = lax.broadcasted_iota(jnp.int32, shape, 1)
    kc = lane & (GRID_W - 1)
    second = lane >> 6
    code = kc - q + (NA_WIN_COLS - 1) + RPB_COLS * second
    start = jnp.clip(q - NA_WIN_COLS // 2, 0, GRID_W - NA_WIN_COLS)
    valid = (kc >= start) & (kc < start + NA_WIN_COLS)
    neg = jnp.full(shape, NEG_BIG, F32)
    pairs = {}

    def pair(ri):
        if ri not in pairs:
            acc = neg
            for i in range(2 * RPB_COLS):
                if 0 <= ri + i // RPB_COLS < RPB_ROWS:
                    acc = jnp.where(code == i, rpb_ref[base + ri * RPB_COLS + i], acc)
            pairs[ri] = jnp.where(valid, acc * LOG2E, NEG_BIG)
        return pairs[ri]

    for p, (lo, c) in enumerate(patterns):
        for u in range(NA_WINDOW // 2):
            in0 = lo <= 2 * u < lo + NA_WIN_ROWS
            in1 = lo <= 2 * u + 1 < lo + NA_WIN_ROWS
            if in0 and in1:
                tile = pair(2 * u + c)
            elif in0:
                tile = jnp.where(second == 0, pair(2 * u + c), NEG_BIG)
            elif in1:
                tile = jnp.where(second == 1, pair(2 * u + c), NEG_BIG)
            else:
                tile = neg
            o_ref[0, p, :, u * 2 * GRID_W:(u + 1) * 2 * GRID_W] = tile


def _na_band(rpb, patterns):
    shape = (NA_HEADS, len(patterns), GRID_W, NA_WINDOW * GRID_W)
    return pl.pallas_call(
        functools.partial(_na_band_kernel, patterns=patterns),
        grid=(NA_HEADS,),
        in_specs=[pl.BlockSpec(memory_space=pltpu.SMEM)],
        out_specs=pl.BlockSpec((1,) + shape[1:], lambda h: (h, 0, 0, 0)),
        out_shape=jax.ShapeDtypeStruct(shape, F32),
        compiler_params=_params(("arbitrary",)),
        name="na_band",
    )(rpb.reshape(-1))


def _na_kernel(q_ref, k_ref, v_ref, band_ref, o_ref, kt_ref, vb_ref, *, pattern_ids):
    t = q_ref.shape[1]
    rows = t // GRID_W
    gq = NA_GROUP * GRID_W
    wlen = NA_WINDOW * GRID_W
    n_groups = rows // NA_GROUP
    for c in range(0, t, gq):
        kt_ref[:, c:c + gq] = k_ref[0, c:c + gq, :].T.astype(BF16)
    vb_ref[:, :HEAD_DIM] = v_ref[0].astype(BF16)
    vb_ref[:, HEAD_DIM:] = jnp.ones((t, HEAD_DIM), BF16)
    scale = ATTN_SCALE_LOG2

    def scores(g):
        w0 = _na_window_start(g, rows) * GRID_W
        q = (q_ref[0, g * gq:(g + 1) * gq, :] * scale).astype(BF16)
        bias = jnp.concatenate(
            [band_ref[0, pattern_ids[g * NA_GROUP + i]] for i in range(NA_GROUP)], axis=0)
        return _dot(q, kt_ref[:, w0:w0 + wlen]) + bias

    s_next = scores(0)
    for g in range(n_groups):
        s = s_next
        if g + 1 < n_groups:
            s_next = scores(g + 1)
        w0 = _na_window_start(g, rows) * GRID_W
        m = jnp.max(s, axis=-1, keepdims=True)
        p = jnp.exp2(s - m).astype(BF16)
        ol = _dot(p, vb_ref[w0:w0 + wlen, :])
        o = ol[:, :HEAD_DIM] / ol[:, HEAD_DIM:]
        o_ref[0, g * gq:(g + 1) * gq, :] = o.astype(o_ref.dtype)


def _na_attention(proj, rpb):
    b, t, _ = proj.shape
    nh = NA_HEADS
    patterns, pattern_ids = _na_patterns(t // GRID_W)
    band = _na_band(rpb, patterns)
    blk = (1, t, HEAD_DIM)
    return pl.pallas_call(
        functools.partial(_na_kernel, pattern_ids=pattern_ids),
        grid=(nh, b),
        in_specs=[pl.BlockSpec(blk, lambda h, i: (i, 0, h)),
                  pl.BlockSpec(blk, lambda h, i: (i, 0, nh + h)),
                  pl.BlockSpec(blk, lambda h, i: (i, 0, 2 * nh + h)),
                  pl.BlockSpec((1,) + band.shape[1:], lambda h, i: (h, 0, 0, 0))],
        out_specs=pl.BlockSpec(blk, lambda h, i: (i, 0, h)),
        out_shape=jax.ShapeDtypeStruct((b, t, HALF), BF16),
        scratch_shapes=[pltpu.VMEM((HEAD_DIM, t), BF16), pltpu.VMEM((t, 2 * HEAD_DIM), BF16)],
        compiler_params=_params(("parallel", "parallel")),
        name="na_attention",
    )(proj, proj, proj, band)


HG_LEVELS = tuple(HG_CHUNK >> (i + 1) for i in range(HG_CHUNK.bit_length() - 1))
HG_DIAG = len(HG_LEVELS)


def _split2(x):
    hi = x.astype(BF16)
    mid = (x - hi.astype(F32)).astype(BF16)
    return hi, mid


def _level_codes(c, rev):
    t = lax.broadcasted_iota(jnp.int32, (c, c), 0)
    s = lax.broadcasted_iota(jnp.int32, (c, c), 1)
    x = t ^ s
    code = jnp.full((c, c), HG_DIAG, jnp.int32)
    for li in reversed(range(len(HG_LEVELS))):
        code = jnp.where(x >= HG_LEVELS[li], li, code)
    return jnp.where((s >= t) if rev else (s <= t), code, -1)


def _pivot_rows(gc, m, rev):
    c = gc.shape[0]
    blk = 2 * m
    if blk >= 8:
        piv = m if rev else m - 1
        parts = []
        for s in range(0, c, blk):
            parts.append(jnp.broadcast_to(gc[s + piv:s + piv + 1, :], (blk, gc.shape[1])))
        return parts[0] if len(parts) == 1 else jnp.concatenate(parts, axis=0)
    pos = lax.broadcasted_iota(jnp.int32, gc.shape, 0) & (blk - 1)
    piv = m if rev else m - 1
    out = gc
    for p in range(blk):
        if p == piv:
            continue
        shift = (p - piv) % c
        out = jnp.where(pos == p, pltpu.roll(gc, shift, 0), out)
    return out


def _neg_abs(x):
    bits = lax.bitcast_convert_type(x, jnp.uint32) | jnp.uint32(0x80000000)
    return lax.bitcast_convert_type(bits, F32)


def _hgrn_prepare(zs, qs, lb, tri_ref, codes_ref):
    c = qs[0].shape[0]
    sub = 8
    dirs = (False, True)
    f = [lb + (1.0 - lb) * _sigmoid(z) for z in zs]
    k = [1.0 - x for x in f]
    parts = [_split2(jnp.log2(x)) for x in f]
    gc = [_dot(tri_ref[d], parts[d][0]) + _dot(tri_ref[d], parts[d][1]) for d in range(2)]
    pos = lax.broadcasted_iota(jnp.int32, qs[0].shape, 0)

    def tiles(x):
        return [x[i:i + sub] for i in range(0, c, sub)]

    def code_tiles(d):
        return [codes_ref[d, i:i + sub, :] for i in range(0, c, sub)]

    a = []
    for d in range(2):
        diag = jnp.sum(qs[d] * k[d], axis=-1, keepdims=True)
        a.append([jnp.where(ct == HG_DIAG, dt, 0.0) for ct, dt in zip(code_tiles(d), tiles(diag))])

    def merge(d, li, prod):
        a[d] = [jnp.where(ct == li, pt, at) for ct, pt, at in zip(code_tiles(d), tiles(prod), a[d])]

    def small_level(m, decay):
        li = HG_LEVELS.index(m)
        for d, rev in enumerate(dirs):
            is_q = ((pos & m) == 0) if rev else ((pos & m) != 0)
            xl = (jnp.where(is_q, qs[d], k[d]) * decay[d]).astype(BF16)
            merge(d, li, _dot_nt(xl, xl))

    if 1 in HG_LEVELS:
        ones = jnp.ones_like(qs[0])
        dec1 = []
        for d, rev in enumerate(dirs):
            is_q = ((pos & 1) == 0) if rev else ((pos & 1) != 0)
            dec1.append(jnp.where(is_q, f[d], ones))
        small_level(1, dec1)
    if 2 in HG_LEVELS:
        p4 = pos & 3
        dec2 = []
        for d, rev in enumerate(dirs):
            nxt = pltpu.roll(f[d], c - 1, 0)
            prv = pltpu.roll(f[d], 1, 0)
            if rev:
                e = jnp.where(p4 == 0, f[d] * nxt, jnp.where(p4 == 1, f[d], jnp.where(p4 == 2, 1.0, prv)))
            else:
                e = jnp.where(p4 == 0, nxt, jnp.where(p4 == 1, 1.0, jnp.where(p4 == 2, f[d], f[d] * prv)))
            dec2.append(e)
        small_level(2, dec2)

    for li, m in enumerate(HG_LEVELS):
        if m <= 2:
            continue
        e = [jnp.exp2(_neg_abs(gc[d] - _pivot_rows(gc[d], m, rev))) for d, rev in enumerate(dirs)]
        if m < sub:
            small_level(m, e)
            continue
        for d, rev in enumerate(dirs):
            q_rows, rhs = [], []
            for s in range(0, c, 2 * m):
                lo, hi = slice(s, s + m), slice(s + m, s + 2 * m)
                q_sl, k_sl = (lo, hi) if rev else (hi, lo)
                q_rows.append(q_sl)
                kx = k[d][k_sl] * e[d][k_sl]
                rhs += [k[d][lo], kx] if rev else [kx, k[d][hi]]
            lhs = jnp.concatenate([qs[d][r] * e[d][r] for r in q_rows], axis=0).astype(BF16)
            prod = tiles(_dot_nt(lhs, jnp.concatenate(rhs, axis=0).astype(BF16)))
            ct = code_tiles(d)
            j = 0
            for r in q_rows:
                for i in range(r.start // sub, r.stop // sub):
                    a[d][i] = jnp.where(ct[i] == li, prod[j], a[d][i])
                    j += 1

    out = []
    for d, rev in enumerate(dirs):
        g_tot = gc[d][0:1, :] if rev else gc[d][c - 1:c, :]
        qg = (qs[d] * jnp.exp2(gc[d])).astype(BF16)
        kd = (k[d] * jnp.exp2(g_tot - gc[d])).astype(BF16)
        out.append((jnp.concatenate(a[d], axis=0).astype(BF16), qg, kd, jnp.exp2(g_tot)))
    return out


HG_DEC_ROWS = 8


def _row_block(idx, size):
    start = idx * size
    return pl.ds(start if isinstance(start, int) else pl.multiple_of(start, size), size)


def _hgrn_kernel(q_ref, ff_ref, fb_ref, i_ref, g_ref, lbl_ref, nw_ref, o_ref,
                 tri_ref, codes_ref, a_ref, qg_ref, kd_ref, dec_ref, qa_ref, vb_ref, of_ref, ob_ref,
                 *, layer):
    t = q_ref.shape[1]
    c = HG_CHUNK
    n = t // c
    dk = q_ref.shape[2]
    dv = i_ref.shape[2]

    lg = lbl_ref[...]
    e = jnp.exp(lg - jnp.max(lg, axis=0, keepdims=True))
    p = e / jnp.sum(e, axis=0, keepdims=True)
    lb = jnp.sum(p[:layer + 1, :], axis=0, keepdims=True) - p[0:1, :]

    row = lax.broadcasted_iota(jnp.int32, (c, c), 0)
    col = lax.broadcasted_iota(jnp.int32, (c, c), 1)
    tri_ref[0] = jnp.where(col <= row, 1.0, 0.0).astype(BF16)
    tri_ref[1] = jnp.where(col >= row, 1.0, 0.0).astype(BF16)
    codes_ref[0] = _level_codes(c, False)
    codes_ref[1] = _level_codes(c, True)

    act_rows = 256

    def activate(i, carry):
        rows = _row_block(i, act_rows)
        qa_ref[rows, :] = _silu(q_ref[0, rows, :])
        vb_ref[rows, :] = i_ref[0, rows, :].astype(BF16)
        return carry

    lax.fori_loop(0, t // act_rows, activate, 0)

    def chunk_rows(i):
        return [_row_block(i, c), _row_block(n - 1 - i, c)]

    def dec_rows(i, size):
        return [pl.ds(r.start, size) for r in (_row_block(i, HG_DEC_ROWS), _row_block(n - 1 - i, HG_DEC_ROWS))]

    def prepare(i):
        rows = chunk_rows(i)
        both = _hgrn_prepare([ff_ref[0, rows[0], :], fb_ref[0, rows[1], :]],
                             [qa_ref[rows[0], :], qa_ref[rows[1], :]], lb, tri_ref, codes_ref)
        for d, (a, qg, kd, dec) in enumerate(both):
            a_ref[d, rows[d], :] = a
            qg_ref[d, rows[d], :] = qg
            kd_ref[d, rows[d], :] = kd
            dec_ref[d, dec_rows(i, HG_DEC_ROWS)[d], :] = jnp.broadcast_to(dec, (HG_DEC_ROWS, dk))

    def recur(i, states):
        rows = chunk_rows(i)
        new = []
        for d, out_ref in enumerate((of_ref, ob_ref)):
            vb = vb_ref[rows[d], :]
            out_ref[rows[d], :] = (_dot(a_ref[d, rows[d], :], vb)
                                   + _dot_nt(qg_ref[d, rows[d], :], states[d].astype(BF16)))
            dec = dec_ref[d, dec_rows(i, 1)[d], :]
            new.append(states[d] * dec + _dot_tn(vb, kd_ref[d, rows[d], :]))
        return tuple(new)

    def body(i, states):
        states = recur(i, states)
        prepare(i + 1)
        return states

    zero = jnp.zeros((dv, dk), F32)
    prepare(0)
    recur(n - 1, lax.fori_loop(0, n - 1, body, (zero, zero)))

    nw = nw_ref[...]
    rows = 256

    def finish(i, carry):
        off = pl.multiple_of(i * rows, rows)
        o = of_ref[pl.ds(off, rows), :] + ob_ref[pl.ds(off, rows), :]
        ms = jnp.mean(o * o, axis=-1, keepdims=True)
        y = o * lax.rsqrt(ms + RMS_EPS) * nw
        o_ref[0, pl.ds(off, rows), :] = (y * _silu(g_ref[0, pl.ds(off, rows), :])).astype(o_ref.dtype)
        return carry

    lax.fori_loop(0, t // rows, finish, 0, unroll=4)


def _hgrn(proj, lb_logits, norm_w, layer):
    b, t, _ = proj.shape
    nh = HG_HEADS
    n_layers = lb_logits.shape[0]
    blk = (1, t, HEAD_DIM)
    base = 3 * HALF // HEAD_DIM

    def col(k):
        return lambda i, h: (i, 0, base + k * nh + h)

    return pl.pallas_call(
        functools.partial(_hgrn_kernel, layer=layer),
        grid=(b, nh),
        in_specs=[pl.BlockSpec(blk, col(0)), pl.BlockSpec(blk, col(1)), pl.BlockSpec(blk, col(2)),
                  pl.BlockSpec(blk, col(3)), pl.BlockSpec(blk, col(4)),
                  pl.BlockSpec((n_layers, HEAD_DIM), lambda i, h: (0, h)),
                  pl.BlockSpec((1, HEAD_DIM), lambda i, h: (0, h))],
        out_specs=pl.BlockSpec(blk, lambda i, h: (i, 0, h)),
        out_shape=jax.ShapeDtypeStruct((b, t, HALF), BF16),
        scratch_shapes=[pltpu.VMEM((2, HG_CHUNK, HG_CHUNK), BF16),
                        pltpu.VMEM((2, HG_CHUNK, HG_CHUNK), jnp.int32),
                        pltpu.VMEM((2, t, HG_CHUNK), BF16),
                        pltpu.VMEM((2, t, HEAD_DIM), BF16),
                        pltpu.VMEM((2, t, HEAD_DIM), BF16),
                        pltpu.VMEM((2, t // HG_CHUNK * HG_DEC_ROWS, HEAD_DIM), F32),
                        pltpu.VMEM((t, HEAD_DIM), F32),
                        pltpu.VMEM((t, HEAD_DIM), BF16),
                        pltpu.VMEM((t, HEAD_DIM), F32), pltpu.VMEM((t, HEAD_DIM), F32)],
        compiler_params=_params(("parallel", "parallel")),
        name="hgrn2",
    )(proj, proj, proj, proj, proj, lb_logits, norm_w.reshape(1, HALF))


POOL_PAD = 8


def _pool_kernel(x_ref, w_ref, s_ref, o_ref, xp_ref):
    t = x_ref.shape[1]
    width = x_ref.shape[2]
    rows = 256
    xp_ref[0:POOL_PAD, :] = jnp.zeros((POOL_PAD, width), F32)
    xp_ref[POOL_PAD + t:, :] = jnp.zeros((POOL_PAD, width), F32)
    xp_ref[POOL_PAD:POOL_PAD + t, :] = x_ref[0]
    for ci in range(t // rows):
        t0 = ci * rows
        pos = t0 + lax.broadcasted_iota(jnp.int32, (rows, POOL_GROUP_DIM), 0)
        for gi, win in enumerate(POOL_WINDOWS):
            hw = win // 2
            c0 = gi * POOL_GROUP_DIM
            c1 = c0 + POOL_GROUP_DIM
            tot = None
            for o in range(-hw, hw):
                piece = xp_ref[POOL_PAD + t0 + o:POOL_PAD + t0 + o + rows, c0:c1]
                tot = piece if tot is None else tot + piece
            cnt = (jnp.minimum(pos + hw, t) - jnp.maximum(pos - hw, 0)).astype(F32)
            pooled = tot / cnt - x_ref[0, t0:t0 + rows, c0:c1]
            y = _dot(pooled.astype(BF16), w_ref[gi]) * s_ref[:, c0:c1]
            o_ref[0, t0:t0 + rows, c0:c1] = y.astype(o_ref.dtype)


def _pool(proj, pool_w, scale):
    b, t, _ = proj.shape
    ng = len(POOL_WINDOWS)
    return pl.pallas_call(
        _pool_kernel,
        grid=(b,),
        in_specs=[pl.BlockSpec((1, t, HALF), lambda i: (i, 0, 0)),
                  pl.BlockSpec((ng, POOL_GROUP_DIM, POOL_GROUP_DIM), lambda i: (0, 0, 0)),
                  pl.BlockSpec((1, HALF), lambda i: (0, 0))],
        out_specs=pl.BlockSpec((1, t, HALF), lambda i: (i, 0, 0)),
        out_shape=jax.ShapeDtypeStruct((b, t, HALF), BF16),
        scratch_shapes=[pltpu.VMEM((t + 2 * POOL_PAD, HALF), F32)],
        compiler_params=_params(("parallel",)),
        name="pool",
    )(proj, pool_w, scale.reshape(1, HALF))


def _rope_tables(t):
    pos = jnp.arange(t)
    row = (pos // GRID_W).astype(F32)
    col = (pos % GRID_W).astype(F32)
    n_freq = HEAD_DIM // 4
    inv = ROPE_THETA ** (-jnp.arange(n_freq, dtype=F32) / n_freq)
    ang = jnp.concatenate([row[:, None] * inv, col[:, None] * inv], axis=-1)
    cos = jnp.repeat(jnp.cos(ang), 2, axis=-1)
    sin = jnp.repeat(jnp.sin(ang), 2, axis=-1)
    sign = jnp.where(jnp.arange(HEAD_DIM) % 2 == 0, -1.0, 1.0).astype(F32)
    return cos, sin * sign


def _norm_rope(x, g, cos, sin_signed, scale):
    ms = jnp.mean(x * x, axis=-1, keepdims=True)
    xn = x * lax.rsqrt(ms + RMS_EPS) * g
    lane = lax.broadcasted_iota(jnp.int32, x.shape, 1)
    partner = jnp.where((lane & 1) == 0, pltpu.roll(xn, HEAD_DIM - 1, 1), pltpu.roll(xn, 1, 1))
    return (xn * cos + partner * sin_signed) * scale


def _qkv_prep_kernel(q_ref, k_ref, v_ref, qn_ref, kn_ref, cos_ref, sin_ref, qo_ref, kt_ref, vo_ref):
    cos = cos_ref[...]
    sin = sin_ref[...]
    for h in range(GQA_Q_HEADS):
        sl = slice(h * HEAD_DIM, (h + 1) * HEAD_DIM)
        qo_ref[:, sl] = _norm_rope(q_ref[:, sl], qn_ref[...], cos, sin,
                                   ATTN_SCALE_LOG2).astype(qo_ref.dtype)
    for h in range(GQA_KV_HEADS):
        sl = slice(h * HEAD_DIM, (h + 1) * HEAD_DIM)
        kt_ref[0, sl, :] = _norm_rope(k_ref[:, sl], kn_ref[...], cos, sin, 1.0).T.astype(kt_ref.dtype)
    ones = jnp.ones((v_ref.shape[0], HEAD_DIM), vo_ref.dtype)
    for h in range(GQA_KV_HEADS):
        vo_ref[:, 2 * h * HEAD_DIM:(2 * h + 1) * HEAD_DIM] = (
            v_ref[:, h * HEAD_DIM:(h + 1) * HEAD_DIM].astype(vo_ref.dtype))
        vo_ref[:, (2 * h + 1) * HEAD_DIM:(2 * h + 2) * HEAD_DIM] = ones


def _qkv_prep(proj2d, q_norm, k_norm, cos, sin, b, t, tm):
    m = proj2d.shape[0]
    tb = t // tm
    kv_blk = (2 * HALF) // KV_W
    return pl.pallas_call(
        _qkv_prep_kernel,
        grid=(m // tm,),
        in_specs=[pl.BlockSpec((tm, HALF), lambda i: (i, 1)),
                  pl.BlockSpec((tm, KV_W), lambda i: (i, kv_blk)),
                  pl.BlockSpec((tm, KV_W), lambda i: (i, kv_blk + 1)),
                  pl.BlockSpec((1, HEAD_DIM), lambda i: (0, 0)),
                  pl.BlockSpec((1, HEAD_DIM), lambda i: (0, 0)),
                  pl.BlockSpec((tm, HEAD_DIM), lambda i: (i % tb, 0)),
                  pl.BlockSpec((tm, HEAD_DIM), lambda i: (i % tb, 0))],
        out_specs=[pl.BlockSpec((tm, HALF), lambda i: (i, 0)),
                   pl.BlockSpec((1, KV_W, tm), lambda i: (i // tb, 0, i % tb)),
                   pl.BlockSpec((tm, 2 * KV_W), lambda i: (i, 0))],
        out_shape=[jax.ShapeDtypeStruct((m, HALF), BF16),
                   jax.ShapeDtypeStruct((b, KV_W, t), BF16),
                   jax.ShapeDtypeStruct((m, 2 * KV_W), BF16)],
        compiler_params=_params(("parallel",)),
        name="qkv_prep",
    )(proj2d, proj2d, proj2d, q_norm.reshape(1, HEAD_DIM), k_norm.reshape(1, HEAD_DIM), cos, sin)


GQA_GROUP = GQA_Q_HEADS // GQA_KV_HEADS


def _gqa_kernel(q_ref, kt_ref, v_ref, o_ref):
    def scores(h):
        return _dot(q_ref[0, :, h * HEAD_DIM:(h + 1) * HEAD_DIM], kt_ref[0])

    s_next = scores(0)
    for h in range(GQA_GROUP):
        s = s_next
        if h + 1 < GQA_GROUP:
            s_next = scores(h + 1)
        m = jnp.max(s, axis=-1, keepdims=True)
        p = jnp.exp2(s - m).astype(BF16)
        ol = _dot(p, v_ref[0])
        o = ol[:, :HEAD_DIM] / ol[:, HEAD_DIM:]
        o_ref[0, :, h * HEAD_DIM:(h + 1) * HEAD_DIM] = o.astype(o_ref.dtype)


def _gqa(q, kt, v, tq):
    b, t, _ = q.shape
    gw = GQA_GROUP * HEAD_DIM
    return pl.pallas_call(
        _gqa_kernel,
        grid=(b, GQA_KV_HEADS, t // tq),
        in_specs=[pl.BlockSpec((1, tq, gw), lambda i, h, j: (i, j, h)),
                  pl.BlockSpec((1, HEAD_DIM, t), lambda i, h, j: (i, h, 0)),
                  pl.BlockSpec((1, t, 2 * HEAD_DIM), lambda i, h, j: (i, 0, h))],
        out_specs=pl.BlockSpec((1, tq, gw), lambda i, h, j: (i, j, h)),
        out_shape=jax.ShapeDtypeStruct((b, t, HALF), BF16),
        compiler_params=_params(("parallel", "parallel", "parallel")),
        name="gqa",
    )(q, kt, v)


def kernel(x, ab_w_in, ab_w_out, na_rpb, hg_lb_logits, hg_norm_w, cd_w_in, cd_w_out, pool_w,
           pool_scale, d_q_norm, d_k_norm, ln_mix_g, ln_mix_b, ffn_w_gate, ffn_w_up, ffn_w_down,
           ln_ffn_g, ln_ffn_b):
    b, t, d = x.shape
    m = b * t
    cos, sin = _rope_tables(t)
    xf = x.reshape(m, d)
    xb = xf.astype(BF16)
    ab_w_out_b = ab_w_out.astype(BF16)
    cd_w_out_b = cd_w_out.astype(BF16)
    pool_w_b = pool_w.astype(BF16)
    ffn_w_down_b = ffn_w_down.astype(BF16)
    for layer in range(DEPTH):
        j = layer // 2
        if layer % 2 == 0:
            proj = _matmul(xb, ab_w_in, j, F32, 1024, 1024, "ab_in_proj").reshape(b, t, -1)
            y1 = _na_attention(proj, na_rpb[j])
            y2 = _hgrn(proj, hg_lb_logits, hg_norm_w[j], j)
            w_out = ab_w_out_b
        else:
            proj = _matmul(xb, cd_w_in, j, F32, 1024, 1280, "cd_in_proj")
            y1 = _pool(proj.reshape(b, t, -1), pool_w_b[j], pool_scale[j])
            qn, kt, vn = _qkv_prep(proj, d_q_norm[j], d_k_norm[j], cos, sin, b, t, 512)
            y2 = _gqa(qn.reshape(b, t, HALF), kt, vn.reshape(b, t, 2 * KV_W), 512)
            w_out = cd_w_out_b
        xf, xb = _mix_out(y1.reshape(m, HALF), y2.reshape(m, HALF), w_out, j, xf,
                          ln_mix_g[layer], ln_mix_b[layer], 512)
        h = _ffn_up(xb, ffn_w_gate, ffn_w_up, layer, 1024, 512)
        xf, xb = _ffn_down(h, ffn_w_down_b, layer, xf, ln_ffn_g[layer], ln_ffn_b[layer], 256)
    return xf.reshape(b, t, d)
```

```python
import functools

import jax
import jax.numpy as jnp
from jax import lax
from jax.experimental import pallas as pl
from jax.experimental.pallas import tpu as pltpu

D_MODEL = 2048
DEPTH = 4
GRID_W = 64
HALF = 1024
HEAD_DIM = 128
NA_HEADS = 8
NA_WIN_ROWS = 8
NA_WIN_COLS = 16
HG_HEADS = 8
HG_CHUNK = 128
POOL_WINDOWS = (2, 4, 8, 16)
POOL_GROUP_DIM = 256
GQA_Q_HEADS = 8
GQA_KV_HEADS = 2
KV_W = GQA_KV_HEADS * HEAD_DIM
ROPE_THETA = 10000.0
FFN_HIDDEN = 5632
DN_ALPHA = (2 * DEPTH) ** 0.25
LN_EPS = 1e-5
RMS_EPS = 1e-6
NEG_BIG = -1e30
LOG2E = 1.4426950408889634
ATTN_SCALE_LOG2 = HEAD_DIM ** -0.5 * LOG2E

V7X_VMEM_LIMIT_BYTES = 56 * 1024 * 1024

F32 = jnp.float32
BF16 = jnp.bfloat16


def _params(semantics):
    return pltpu.CompilerParams(dimension_semantics=semantics,
                                vmem_limit_bytes=V7X_VMEM_LIMIT_BYTES)


def _sigmoid(x):
    return 1.0 / (1.0 + jnp.exp(-x))


def _silu(x):
    return x * _sigmoid(x)


def _dot(a, b):
    return jnp.dot(a, b, preferred_element_type=F32)


def _dot_nt(a, b):
    return lax.dot_general(a, b, (((1,), (1,)), ((), ())), preferred_element_type=F32)


def _dot_tn(a, b):
    return lax.dot_general(a, b, (((0,), (0,)), ((), ())), preferred_element_type=F32)


def _matmul_kernel(a_ref, w_ref, o_ref, wb_ref):
    @pl.when(pl.program_id(1) == 0)
    def _():
        wb_ref[...] = w_ref[...].astype(BF16)

    o_ref[...] = _dot(a_ref[...], wb_ref[...]).astype(o_ref.dtype)


def _matmul(a, w, layer, out_dtype, tm, tn, name):
    m, k = a.shape
    n = w.shape[2]
    return pl.pallas_call(
        _matmul_kernel,
        grid=(n // tn, m // tm),
        in_specs=[pl.BlockSpec((tm, k), lambda j, i: (i, 0)),
                  pl.BlockSpec((None, k, tn), lambda j, i: (layer, 0, j))],
        out_specs=pl.BlockSpec((tm, tn), lambda j, i: (i, j)),
        out_shape=jax.ShapeDtypeStruct((m, n), out_dtype),
        scratch_shapes=[pltpu.VMEM((k, tn), BF16)],
        compiler_params=_params(("arbitrary", "arbitrary")),
        name=name,
    )(a, w)


def _res_ln(acc, x, g, b):
    z = DN_ALPHA * x + acc
    mu = jnp.mean(z, axis=-1, keepdims=True)
    zc = z - mu
    var = jnp.mean(zc * zc, axis=-1, keepdims=True)
    return zc * lax.rsqrt(var + LN_EPS) * g + b


def _row_parts(tm, parts):
    size = tm // parts
    return [slice(p * size, (p + 1) * size) for p in range(parts)]


def _mix_out_kernel(y1_ref, y2_ref, w_ref, x_ref, g_ref, b_ref, of_ref, ob_ref, *, parts):
    half = y1_ref.shape[1]
    rows = _row_parts(y1_ref.shape[0], parts)
    accs = [_dot(y1_ref[r, :], w_ref[:half, :]) + _dot(y2_ref[r, :], w_ref[half:, :]) for r in rows]
    for r, acc in zip(rows, accs):
        out = _res_ln(acc, x_ref[r, :], g_ref[...], b_ref[...])
        of_ref[r, :] = out
        ob_ref[r, :] = out.astype(BF16)


def _mix_out(y1, y2, w, layer, x, g, b, tm):
    m, half = y1.shape
    d = w.shape[2]
    row = lambda i: (i, 0)
    fixed = lambda i: (0, 0)
    return pl.pallas_call(
        functools.partial(_mix_out_kernel, parts=4),
        grid=(m // tm,),
        in_specs=[pl.BlockSpec((tm, half), row), pl.BlockSpec((tm, half), row),
                  pl.BlockSpec((None, 2 * half, d), lambda i: (layer, 0, 0),
                               pipeline_mode=pl.Buffered(1)),
                  pl.BlockSpec((tm, d), row),
                  pl.BlockSpec((1, d), fixed), pl.BlockSpec((1, d), fixed)],
        out_specs=[pl.BlockSpec((tm, d), row), pl.BlockSpec((tm, d), row)],
        out_shape=[jax.ShapeDtypeStruct((m, d), F32), jax.ShapeDtypeStruct((m, d), BF16)],
        compiler_params=_params(("parallel",)),
        name="mix_out_ln",
    )(y1, y2, w, x, g.reshape(1, d), b.reshape(1, d))


def _ffn_up_kernel(x_ref, wg_ref, wu_ref, h_ref, wgb_ref, wub_ref):
    @pl.when(pl.program_id(1) == 0)
    def _():
        wgb_ref[...] = wg_ref[...].astype(BF16)
        wub_ref[...] = wu_ref[...].astype(BF16)

    x = x_ref[...]
    gate = _dot(x, wgb_ref[...])
    up = _dot(x, wub_ref[...])
    h_ref[...] = (_silu(gate) * up).astype(h_ref.dtype)


def _ffn_up(xb, wg, wu, layer, tm, tn):
    m, k = xb.shape
    n = wg.shape[2]
    return pl.pallas_call(
        _ffn_up_kernel,
        grid=(n // tn, m // tm),
        in_specs=[pl.BlockSpec((tm, k), lambda j, i: (i, 0)),
                  pl.BlockSpec((None, k, tn), lambda j, i: (layer, 0, j)),
                  pl.BlockSpec((None, k, tn), lambda j, i: (layer, 0, j))],
        out_specs=pl.BlockSpec((tm, tn), lambda j, i: (i, j)),
        out_shape=jax.ShapeDtypeStruct((m, n), BF16),
        scratch_shapes=[pltpu.VMEM((k, tn), BF16), pltpu.VMEM((k, tn), BF16)],
        compiler_params=_params(("arbitrary", "arbitrary")),
        name="ffn_up",
    )(xb, wg, wu)


def _ffn_down_kernel(h_ref, w_ref, x_ref, g_ref, b_ref, of_ref, ob_ref, *, parts):
    rows = _row_parts(h_ref.shape[0], parts)
    accs = [_dot(h_ref[r, :], w_ref[...]) for r in rows]
    for r, acc in zip(rows, accs):
        out = _res_ln(acc, x_ref[r, :], g_ref[...], b_ref[...])
        of_ref[r, :] = out
        ob_ref[r, :] = out.astype(BF16)


def _ffn_down(h, w, layer, x, g, b, tm):
    m, kk = h.shape
    d = w.shape[2]
    row = lambda i: (i, 0)
    fixed = lambda i: (0, 0)
    return pl.pallas_call(
        functools.partial(_ffn_down_kernel, parts=1),
        grid=(m // tm,),
        in_specs=[pl.BlockSpec((tm, kk), row),
                  pl.BlockSpec((None, kk, d), lambda i: (layer, 0, 0), pipeline_mode=pl.Buffered(1)),
                  pl.BlockSpec((tm, d), row),
                  pl.BlockSpec((1, d), fixed), pl.BlockSpec((1, d), fixed)],
        out_specs=[pl.BlockSpec((tm, d), row), pl.BlockSpec((tm, d), row)],
        out_shape=[jax.ShapeDtypeStruct((m, d), F32), jax.ShapeDtypeStruct((m, d), BF16)],
        compiler_params=_params(("parallel",)),
        name="ffn_down_ln",
    )(h, w, x, g.reshape(1, d), b.reshape(1, d))


NA_GROUP = 4
NA_WINDOW = NA_GROUP + NA_WIN_ROWS
RPB_ROWS = 2 * NA_WIN_ROWS - 1
RPB_COLS = 2 * NA_WIN_COLS - 1


def _na_window_start(g, rows):
    return min(max(g * NA_GROUP - NA_WIN_ROWS // 2, 0), rows - NA_WINDOW)


def _na_patterns(rows):
    pats, ids = [], []
    for r in range(rows):
        w0 = _na_window_start(r // NA_GROUP, rows)
        r0 = min(max(r - NA_WIN_ROWS // 2, 0), rows - NA_WIN_ROWS)
        key = (r0 - w0, w0 - r + NA_WIN_ROWS - 1)
        if key not in pats:
            pats.append(key)
        ids.append(pats.index(key))
    return tuple(pats), tuple(ids)


def _na_band_kernel(rpb_ref, o_ref, *, patterns):
    h = pl.program_id(0)
    base = h * (RPB_ROWS * RPB_COLS)
    shape = (GRID_W, 2 * GRID_W)
    q = lax.broadcasted_iota(jnp.int32, shape, 0)
    lane = lax.broadcasted_iota(jnp.int32, shape, 1)
    kc = lane & (GRID_W - 1)
    second = lane >> 6
    code = kc - q + (NA_WIN_COLS - 1) + RPB_COLS * second
    start = jnp.clip(q - NA_WIN_COLS // 2, 0, GRID_W - NA_WIN_COLS)
    valid = (kc >= start) & (kc < start + NA_WIN_COLS)
    neg = jnp.full(shape, NEG_BIG, F32)
    pairs = {}

    def pair(ri):
        if ri not in pairs:
            acc = neg
            for i in range(2 * RPB_COLS):
                if 0 <= ri + i // RPB_COLS < RPB_ROWS:
                    acc = jnp.where(code == i, rpb_ref[base + ri * RPB_COLS + i], acc)
            pairs[ri] = jnp.where(valid, acc * LOG2E, NEG_BIG)
        return pairs[ri]

    for p, (lo, c) in enumerate(patterns):
        for u in range(NA_WINDOW // 2):
            in0 = lo <= 2 * u < lo + NA_WIN_ROWS
            in1 = lo <= 2 * u + 1 < lo + NA_WIN_ROWS
            if in0 and in1:
                tile = pair(2 * u + c)
            elif in0:
                tile = jnp.where(second == 0, pair(2 * u + c), NEG_BIG)
            elif in1:
                tile = jnp.where(second == 1, pair(2 * u + c), NEG_BIG)
            else:
                tile = neg
            o_ref[0, p, :, u * 2 * GRID_W:(u + 1) * 2 * GRID_W] = tile


def _na_band(rpb, patterns):
    shape = (NA_HEADS, len(patterns), GRID_W, NA_WINDOW * GRID_W)
    return pl.pallas_call(
        functools.partial(_na_band_kernel, patterns=patterns),
        grid=(NA_HEADS,),
        in_specs=[pl.BlockSpec(memory_space=pltpu.SMEM)],
        out_specs=pl.BlockSpec((1,) + shape[1:], lambda h: (h, 0, 0, 0)),
        out_shape=jax.ShapeDtypeStruct(shape, F32),
        compiler_params=_params(("arbitrary",)),
        name="na_band",
    )(rpb.reshape(-1))


def _na_kernel(q_ref, k_ref, v_ref, band_ref, o_ref, kt_ref, vb_ref, *, pattern_ids):
    t = q_ref.shape[1]
    rows = t // GRID_W
    gq = NA_GROUP * GRID_W
    wlen = NA_WINDOW * GRID_W
    n_groups = rows // NA_GROUP
    for c in range(0, t, gq):
        kt_ref[:, c:c + gq] = k_ref[0, c:c + gq, :].T.astype(BF16)
    vb_ref[:, :HEAD_DIM] = v_ref[0].astype(BF16)
    vb_ref[:, HEAD_DIM:] = jnp.ones((t, HEAD_DIM), BF16)
    scale = ATTN_SCALE_LOG2

    def scores(g):
        w0 = _na_window_start(g, rows) * GRID_W
        q = (q_ref[0, g * gq:(g + 1) * gq, :] * scale).astype(BF16)
        bias = jnp.concatenate(
            [band_ref[0, pattern_ids[g * NA_GROUP + i]] for i in range(NA_GROUP)], axis=0)
        return _dot(q, kt_ref[:, w0:w0 + wlen]) + bias

    s_next = scores(0)
    for g in range(n_groups):
        s = s_next
        if g + 1 < n_groups:
            s_next = scores(g + 1)
        w0 = _na_window_start(g, rows) * GRID_W
        m = jnp.max(s, axis=-1, keepdims=True)
        p = jnp.exp2(s - m).astype(BF16)
        ol = _dot(p, vb_ref[w0:w0 + wlen, :])
        o = ol[:, :HEAD_DIM] / ol[:, HEAD_DIM:]
        o_ref[0, g * gq:(g + 1) * gq, :] = o.astype(o_ref.dtype)


def _na_attention(proj, rpb):
    b, t, _ = proj.shape
    nh = NA_HEADS
    patterns, pattern_ids = _na_patterns(t // GRID_W)
    band = _na_band(rpb, patterns)
    blk = (1, t, HEAD_DIM)
    return pl.pallas_call(
        functools.partial(_na_kernel, pattern_ids=pattern_ids),
        grid=(nh, b),
        in_specs=[pl.BlockSpec(blk, lambda h, i: (i, 0, h)),
                  pl.BlockSpec(blk, lambda h, i: (i, 0, nh + h)),
                  pl.BlockSpec(blk, lambda h, i: (i, 0, 2 * nh + h)),
                  pl.BlockSpec((1,) + band.shape[1:], lambda h, i: (h, 0, 0, 0))],
        out_specs=pl.BlockSpec(blk, lambda h, i: (i, 0, h)),
        out_shape=jax.ShapeDtypeStruct((b, t, HALF), BF16),
        scratch_shapes=[pltpu.VMEM((HEAD_DIM, t), BF16), pltpu.VMEM((t, 2 * HEAD_DIM), BF16)],
        compiler_params=_params(("parallel", "parallel")),
        name="na_attention",
    )(proj, proj, proj, band)


HG_LEVELS = tuple(HG_CHUNK >> (i + 1) for i in range(HG_CHUNK.bit_length() - 1))
HG_DIAG = len(HG_LEVELS)


def _split2(x):
    hi = x.astype(BF16)
    mid = (x - hi.astype(F32)).astype(BF16)
    return hi, mid


def _level_codes(c, rev):
    t = lax.broadcasted_iota(jnp.int32, (c, c), 0)
    s = lax.broadcasted_iota(jnp.int32, (c, c), 1)
    x = t ^ s
    code = jnp.full((c, c), HG_DIAG, jnp.int32)
    for li in reversed(range(len(HG_LEVELS))):
        code = jnp.where(x >= HG_LEVELS[li], li, code)
    return jnp.where((s >= t) if rev else (s <= t), code, -1)


def _pivot_rows(gc, m, rev):
    c = gc.shape[0]
    blk = 2 * m
    if blk >= 8:
        piv = m if rev else m - 1
        parts = []
        for s in range(0, c, blk):
            parts.append(jnp.broadcast_to(gc[s + piv:s + piv + 1, :], (blk, gc.shape[1])))
        return parts[0] if len(parts) == 1 else jnp.concatenate(parts, axis=0)
    pos = lax.broadcasted_iota(jnp.int32, gc.shape, 0) & (blk - 1)
    piv = m if rev else m - 1
    out = gc
    for p in range(blk):
        if p == piv:
            continue
        shift = (p - piv) % c
        out = jnp.where(pos == p, pltpu.roll(gc, shift, 0), out)
    return out


def _neg_abs(x):
    bits = lax.bitcast_convert_type(x, jnp.uint32) | jnp.uint32(0x80000000)
    return lax.bitcast_convert_type(bits, F32)


def _hgrn_prepare(zs, qs, lb, tri_ref, codes_ref):
    c = qs[0].shape[0]
    sub = 8
    dirs = (False, True)
    f = [lb + (1.0 - lb) * _sigmoid(z) for z in zs]
    k = [1.0 - x for x in f]
    parts = [_split2(jnp.log2(x)) for x in f]
    gc = [_dot(tri_ref[d], parts[d][0]) + _dot(tri_ref[d], parts[d][1]) for d in range(2)]
    pos = lax.broadcasted_iota(jnp.int32, qs[0].shape, 0)

    def tiles(x):
        return [x[i:i + sub] for i in range(0, c, sub)]

    def code_tiles(d):
        return [codes_ref[d, i:i + sub, :] for i in range(0, c, sub)]

    a = []
    for d in range(2):
        diag = jnp.sum(qs[d] * k[d], axis=-1, keepdims=True)
        a.append([jnp.where(ct == HG_DIAG, dt, 0.0) for ct, dt in zip(code_tiles(d), tiles(diag))])

    def merge(d, li, prod):
        a[d] = [jnp.where(ct == li, pt, at) for ct, pt, at in zip(code_tiles(d), tiles(prod), a[d])]

    def small_level(m, decay):
        li = HG_LEVELS.index(m)
        for d, rev in enumerate(dirs):
            is_q = ((pos & m) == 0) if rev else ((pos & m) != 0)
            xl = (jnp.where(is_q, qs[d], k[d]) * decay[d]).astype(BF16)
            merge(d, li, _dot_nt(xl, xl))

    if 1 in HG_LEVELS:
        ones = jnp.ones_like(qs[0])
        dec1 = []
        for d, rev in enumerate(dirs):
            is_q = ((pos & 1) == 0) if rev else ((pos & 1) != 0)
            dec1.append(jnp.where(is_q, f[d], ones))
        small_level(1, dec1)
    if 2 in HG_LEVELS:
        p4 = pos & 3
        dec2 = []
        for d, rev in enumerate(dirs):
            nxt = pltpu.roll(f[d], c - 1, 0)
            prv = pltpu.roll(f[d], 1, 0)
            if rev:
                e = jnp.where(p4 == 0, f[d] * nxt, jnp.where(p4 == 1, f[d], jnp.where(p4 == 2, 1.0, prv)))
            else:
                e = jnp.where(p4 == 0, nxt, jnp.where(p4 == 1, 1.0, jnp.where(p4 == 2, f[d], f[d] * prv)))
            dec2.append(e)
        small_level(2, dec2)

    for li, m in enumerate(HG_LEVELS):
        if m <= 2:
            continue
        e = [jnp.exp2(_neg_abs(gc[d] - _pivot_rows(gc[d], m, rev))) for d, rev in enumerate(dirs)]
        if m < sub:
            small_level(m, e)
            continue
        for d, rev in enumerate(dirs):
            q_rows, rhs = [], []
            for s in range(0, c, 2 * m):
                lo, hi = slice(s, s + m), slice(s + m, s + 2 * m)
                q_sl, k_sl = (lo, hi) if rev else (hi, lo)
                q_rows.append(q_sl)
                kx = k[d][k_sl] * e[d][k_sl]
                rhs += [k[d][lo], kx] if rev else [kx, k[d][hi]]
            lhs = jnp.concatenate([qs[d][r] * e[d][r] for r in q_rows], axis=0).astype(BF16)
            prod = tiles(_dot_nt(lhs, jnp.concatenate(rhs, axis=0).astype(BF16)))
            ct = code_tiles(d)
            j = 0
            for r in q_rows:
                for i in range(r.start // sub, r.stop // sub):
                    a[d][i] = jnp.where(ct[i] == li, prod[j], a[d][i])
                    j += 1

    out = []
    for d, rev in enumerate(dirs):
        g_tot = gc[d][0:1, :] if rev else gc[d][c - 1:c, :]
        qg = (qs[d] * jnp.exp2(gc[d])).astype(BF16)
        kd = (k[d] * jnp.exp2(g_tot - gc[d])).astype(BF16)
        out.append((jnp.concatenate(a[d], axis=0).astype(BF16), qg, kd, jnp.exp2(g_tot)))
    return out


HG_DEC_ROWS = 8


def _row_block(idx, size):
    start = idx * size
    return pl.ds(start if isinstance(start, int) else pl.multiple_of(start, size), size)


def _hgrn_kernel(q_ref, ff_ref, fb_ref, i_ref, g_ref, lbl_ref, nw_ref, o_ref,
                 tri_ref, codes_ref, a_ref, qg_ref, kd_ref, dec_ref, qa_ref, vb_ref, of_ref, ob_ref,
                 *, layer):
    t = q_ref.shape[1]
    c = HG_CHUNK
    n = t // c
    dk = q_ref.shape[2]
    dv = i_ref.shape[2]

    lg = lbl_ref[...]
    e = jnp.exp(lg - jnp.max(lg, axis=0, keepdims=True))
    p = e / jnp.sum(e, axis=0, keepdims=True)
    lb = jnp.sum(p[:layer + 1, :], axis=0, keepdims=True) - p[0:1, :]

    row = lax.broadcasted_iota(jnp.int32, (c, c), 0)
    col = lax.broadcasted_iota(jnp.int32, (c, c), 1)
    tri_ref[0] = jnp.where(col <= row, 1.0, 0.0).astype(BF16)
    tri_ref[1] = jnp.where(col >= row, 1.0, 0.0).astype(BF16)
    codes_ref[0] = _level_codes(c, False)
    codes_ref[1] = _level_codes(c, True)

    act_rows = 256

    def activate(i, carry):
        rows = _row_block(i, act_rows)
        qa_ref[rows, :] = _silu(q_ref[0, rows, :])
        vb_ref[rows, :] = i_ref[0, rows, :].astype(BF16)
        return carry

    lax.fori_loop(0, t // act_rows, activate, 0)

    def chunk_rows(i):
        return [_row_block(i, c), _row_block(n - 1 - i, c)]

    def dec_rows(i, size):
        return [pl.ds(r.start, size) for r in (_row_block(i, HG_DEC_ROWS), _row_block(n - 1 - i, HG_DEC_ROWS))]

    def prepare(i):
        rows = chunk_rows(i)
        both = _hgrn_prepare([ff_ref[0, rows[0], :], fb_ref[0, rows[1], :]],
                             [qa_ref[rows[0], :], qa_ref[rows[1], :]], lb, tri_ref, codes_ref)
        for d, (a, qg, kd, dec) in enumerate(both):
            a_ref[d, rows[d], :] = a
            qg_ref[d, rows[d], :] = qg
            kd_ref[d, rows[d], :] = kd
            dec_ref[d, dec_rows(i, HG_DEC_ROWS)[d], :] = jnp.broadcast_to(dec, (HG_DEC_ROWS, dk))

    def recur(i, states):
        rows = chunk_rows(i)
        new = []
        for d, out_ref in enumerate((of_ref, ob_ref)):
            vb = vb_ref[rows[d], :]
            out_ref[rows[d], :] = (_dot(a_ref[d, rows[d], :], vb)
                                   + _dot_nt(qg_ref[d, rows[d], :], states[d].astype(BF16)))
            dec = dec_ref[d, dec_rows(i, 1)[d], :]
            new.append(states[d] * dec + _dot_tn(vb, kd_ref[d, rows[d], :]))
        return tuple(new)

    def body(i, states):
        states = recur(i, states)
        prepare(i + 1)
        return states

    zero = jnp.zeros((dv, dk), F32)
    prepare(0)
    recur(n - 1, lax.fori_loop(0, n - 1, body, (zero, zero), unroll=3))

    nw = nw_ref[...]
    rows = 256

    def finish(i, carry):
        off = pl.multiple_of(i * rows, rows)
        o = of_ref[pl.ds(off, rows), :] + ob_ref[pl.ds(off, rows), :]
        ms = jnp.mean(o * o, axis=-1, keepdims=True)
        y = o * lax.rsqrt(ms + RMS_EPS) * nw
        o_ref[0, pl.ds(off, rows), :] = (y * _silu(g_ref[0, pl.ds(off, rows), :])).astype(o_ref.dtype)
        return carry

    lax.fori_loop(0, t // rows, finish, 0, unroll=4)


def _hgrn(proj, lb_logits, norm_w, layer):
    b, t, _ = proj.shape
    nh = HG_HEADS
    n_layers = lb_logits.shape[0]
    blk = (1, t, HEAD_DIM)
    base = 3 * HALF // HEAD_DIM

    def col(k):
        return lambda i, h: (i, 0, base + k * nh + h)

    return pl.pallas_call(
        functools.partial(_hgrn_kernel, layer=layer),
        grid=(b, nh),
        in_specs=[pl.BlockSpec(blk, col(0)), pl.BlockSpec(blk, col(1)), pl.BlockSpec(blk, col(2)),
                  pl.BlockSpec(blk, col(3)), pl.BlockSpec(blk, col(4)),
                  pl.BlockSpec((n_layers, HEAD_DIM), lambda i, h: (0, h)),
                  pl.BlockSpec((1, HEAD_DIM), lambda i, h: (0, h))],
        out_specs=pl.BlockSpec(blk, lambda i, h: (i, 0, h)),
        out_shape=jax.ShapeDtypeStruct((b, t, HALF), BF16),
        scratch_shapes=[pltpu.VMEM((2, HG_CHUNK, HG_CHUNK), BF16),
                        pltpu.VMEM((2, HG_CHUNK, HG_CHUNK), jnp.int32),
                        pltpu.VMEM((2, t, HG_CHUNK), BF16),
                        pltpu.VMEM((2, t, HEAD_DIM), BF16),
                        pltpu.VMEM((2, t, HEAD_DIM), BF16),
                        pltpu.VMEM((2, t // HG_CHUNK * HG_DEC_ROWS, HEAD_DIM), F32),
                        pltpu.VMEM((t, HEAD_DIM), F32),
                        pltpu.VMEM((t, HEAD_DIM), BF16),
                        pltpu.VMEM((t, HEAD_DIM), F32), pltpu.VMEM((t, HEAD_DIM), F32)],
        compiler_params=_params(("parallel", "parallel")),
        name="hgrn2",
    )(proj, proj, proj, proj, proj, lb_logits, norm_w.reshape(1, HALF))


POOL_PAD = 8


def _pool_kernel(x_ref, w_ref, s_ref, o_ref, xp_ref):
    t = x_ref.shape[1]
    width = x_ref.shape[2]
    rows = 256
    xp_ref[0:POOL_PAD, :] = jnp.zeros((POOL_PAD, width), F32)
    xp_ref[POOL_PAD + t:, :] = jnp.zeros((POOL_PAD, width), F32)
    xp_ref[POOL_PAD:POOL_PAD + t, :] = x_ref[0]
    for ci in range(t // rows):
        t0 = ci * rows
        pos = t0 + lax.broadcasted_iota(jnp.int32, (rows, POOL_GROUP_DIM), 0)
        for gi, win in enumerate(POOL_WINDOWS):
            hw = win // 2
            c0 = gi * POOL_GROUP_DIM
            c1 = c0 + POOL_GROUP_DIM
            tot = None
            for o in range(-hw, hw):
                piece = xp_ref[POOL_PAD + t0 + o:POOL_PAD + t0 + o + rows, c0:c1]
                tot = piece if tot is None else tot + piece
            cnt = (jnp.minimum(pos + hw, t) - jnp.maximum(pos - hw, 0)).astype(F32)
            pooled = tot / cnt - x_ref[0, t0:t0 + rows, c0:c1]
            y = _dot(pooled.astype(BF16), w_ref[gi]) * s_ref[:, c0:c1]
            o_ref[0, t0:t0 + rows, c0:c1] = y.astype(o_ref.dtype)


def _pool(proj, pool_w, scale):
    b, t, _ = proj.shape
    ng = len(POOL_WINDOWS)
    return pl.pallas_call(
        _pool_kernel,
        grid=(b,),
        in_specs=[pl.BlockSpec((1, t, HALF), lambda i: (i, 0, 0)),
                  pl.BlockSpec((ng, POOL_GROUP_DIM, POOL_GROUP_DIM), lambda i: (0, 0, 0)),
                  pl.BlockSpec((1, HALF), lambda i: (0, 0))],
        out_specs=pl.BlockSpec((1, t, HALF), lambda i: (i, 0, 0)),
        out_shape=jax.ShapeDtypeStruct((b, t, HALF), BF16),
        scratch_shapes=[pltpu.VMEM((t + 2 * POOL_PAD, HALF), F32)],
        compiler_params=_params(("parallel",)),
        name="pool",
    )(proj, pool_w, scale.reshape(1, HALF))


def _rope_tables(t):
    pos = jnp.arange(t)
    row = (pos // GRID_W).astype(F32)
    col = (pos % GRID_W).astype(F32)
    n_freq = HEAD_DIM // 4
    inv = ROPE_THETA ** (-jnp.arange(n_freq, dtype=F32) / n_freq)
    ang = jnp.concatenate([row[:, None] * inv, col[:, None] * inv], axis=-1)
    cos = jnp.repeat(jnp.cos(ang), 2, axis=-1)
    sin = jnp.repeat(jnp.sin(ang), 2, axis=-1)
    sign = jnp.where(jnp.arange(HEAD_DIM) % 2 == 0, -1.0, 1.0).astype(F32)
    return cos, sin * sign


def _norm_rope(x, g, cos, sin_signed, scale):
    ms = jnp.mean(x * x, axis=-1, keepdims=True)
    xn = x * lax.rsqrt(ms + RMS_EPS) * g
    lane = lax.broadcasted_iota(jnp.int32, x.shape, 1)
    partner = jnp.where((lane & 1) == 0, pltpu.roll(xn, HEAD_DIM - 1, 1), pltpu.roll(xn, 1, 1))
    return (xn * cos + partner * sin_signed) * scale


def _qkv_prep_kernel(q_ref, k_ref, v_ref, qn_ref, kn_ref, cos_ref, sin_ref, qo_ref, kt_ref, vo_ref):
    cos = cos_ref[...]
    sin = sin_ref[...]
    for h in range(GQA_Q_HEADS):
        sl = slice(h * HEAD_DIM, (h + 1) * HEAD_DIM)
        qo_ref[:, sl] = _norm_rope(q_ref[:, sl], qn_ref[...], cos, sin,
                                   ATTN_SCALE_LOG2).astype(qo_ref.dtype)
    for h in range(GQA_KV_HEADS):
        sl = slice(h * HEAD_DIM, (h + 1) * HEAD_DIM)
        kt_ref[0, sl, :] = _norm_rope(k_ref[:, sl], kn_ref[...], cos, sin, 1.0).T.astype(kt_ref.dtype)
    ones = jnp.ones((v_ref.shape[0], HEAD_DIM), vo_ref.dtype)
    for h in range(GQA_KV_HEADS):
        vo_ref[:, 2 * h * HEAD_DIM:(2 * h + 1) * HEAD_DIM] = (
            v_ref[:, h * HEAD_DIM:(h + 1) * HEAD_DIM].astype(vo_ref.dtype))
        vo_ref[:, (2 * h + 1) * HEAD_DIM:(2 * h + 2) * HEAD_DIM] = ones


def _qkv_prep(proj2d, q_norm, k_norm, cos, sin, b, t, tm):
    m = proj2d.shape[0]
    tb = t // tm
    kv_blk = (2 * HALF) // KV_W
    return pl.pallas_call(
        _qkv_prep_kernel,
        grid=(m // tm,),
        in_specs=[pl.BlockSpec((tm, HALF), lambda i: (i, 1)),
                  pl.BlockSpec((tm, KV_W), lambda i: (i, kv_blk)),
                  pl.BlockSpec((tm, KV_W), lambda i: (i, kv_blk + 1)),
                  pl.BlockSpec((1, HEAD_DIM), lambda i: (0, 0)),
                  pl.BlockSpec((1, HEAD_DIM), lambda i: (0, 0)),
                  pl.BlockSpec((tm, HEAD_DIM), lambda i: (i % tb, 0)),
                  pl.BlockSpec((tm, HEAD_DIM), lambda i: (i % tb, 0))],
        out_specs=[pl.BlockSpec((tm, HALF), lambda i: (i, 0)),
                   pl.BlockSpec((1, KV_W, tm), lambda i: (i // tb, 0, i % tb)),
                   pl.BlockSpec((tm, 2 * KV_W), lambda i: (i, 0))],
        out_shape=[jax.ShapeDtypeStruct((m, HALF), BF16),
                   jax.ShapeDtypeStruct((b, KV_W, t), BF16),
                   jax.ShapeDtypeStruct((m, 2 * KV_W), BF16)],
        compiler_params=_params(("parallel",)),
        name="qkv_prep",
    )(proj2d, proj2d, proj2d, q_norm.reshape(1, HEAD_DIM), k_norm.reshape(1, HEAD_DIM), cos, sin)


GQA_GROUP = GQA_Q_HEADS // GQA_KV_HEADS


def _gqa_kernel(q_ref, kt_ref, v_ref, o_ref):
    def scores(h):
        return _dot(q_ref[0, :, h * HEAD_DIM:(h + 1) * HEAD_DIM], kt_ref[0])

    s_next = scores(0)
    for h in range(GQA_GROUP):
        s = s_next
        if h + 1 < GQA_GROUP:
            s_next = scores(h + 1)
        m = jnp.max(s, axis=-1, keepdims=True)
        p = jnp.exp2(s - m).astype(BF16)
        ol = _dot(p, v_ref[0])
        o = ol[:, :HEAD_DIM] / ol[:, HEAD_DIM:]
        o_ref[0, :, h * HEAD_DIM:(h + 1) * HEAD_DIM] = o.astype(o_ref.dtype)


def _gqa(q, kt, v, tq):
    b, t, _ = q.shape
    gw = GQA_GROUP * HEAD_DIM
    return pl.pallas_call(
        _gqa_kernel,
        grid=(b, GQA_KV_HEADS, t // tq),
        in_specs=[pl.BlockSpec((1, tq, gw), lambda i, h, j: (i, j, h)),
                  pl.BlockSpec((1, HEAD_DIM, t), lambda i, h, j: (i, h, 0)),
                  pl.BlockSpec((1, t, 2 * HEAD_DIM), lambda i, h, j: (i, 0, h))],
        out_specs=pl.BlockSpec((1, tq, gw), lambda i, h, j: (i, j, h)),
        out_shape=jax.ShapeDtypeStruct((b, t, HALF), BF16),
        compiler_params=_params(("parallel", "parallel", "parallel")),
        name="gqa",
    )(q, kt, v)


def kernel(x, ab_w_in, ab_w_out, na_rpb, hg_lb_logits, hg_norm_w, cd_w_in, cd_w_out, pool_w,
           pool_scale, d_q_norm, d_k_norm, ln_mix_g, ln_mix_b, ffn_w_gate, ffn_w_up, ffn_w_down,
           ln_ffn_g, ln_ffn_b):
    b, t, d = x.shape
    m = b * t
    cos, sin = _rope_tables(t)
    xf = x.reshape(m, d)
    xb = xf.astype(BF16)
    ab_w_out_b = ab_w_out.astype(BF16)
    cd_w_out_b = cd_w_out.astype(BF16)
    pool_w_b = pool_w.astype(BF16)
    ffn_w_down_b = ffn_w_down.astype(BF16)
    for layer in range(DEPTH):
        j = layer // 2
        if layer % 2 == 0:
            proj = _matmul(xb, ab_w_in, j, F32, 1024, 1024, "ab_in_proj").reshape(b, t, -1)
            y1 = _na_attention(proj, na_rpb[j])
            y2 = _hgrn(proj, hg_lb_logits, hg_norm_w[j], j)
            w_out = ab_w_out_b
        else:
            proj = _matmul(xb, cd_w_in, j, F32, 1024, 1280, "cd_in_proj")
            y1 = _pool(proj.reshape(b, t, -1), pool_w_b[j], pool_scale[j])
            qn, kt, vn = _qkv_prep(proj, d_q_norm[j], d_k_norm[j], cos, sin, b, t, 512)
            y2 = _gqa(qn.reshape(b, t, HALF), kt, vn.reshape(b, t, 2 * KV_W), 512)
            w_out = cd_w_out_b
        xf, xb = _mix_out(y1.reshape(m, HALF), y2.reshape(m, HALF), w_out, j, xf,
                          ln_mix_g[layer], ln_mix_b[layer], 512)
        h = _ffn_up(xb, ffn_w_gate, ffn_w_up, layer, 1024, 512)
        xf, xb = _ffn_down(h, ffn_w_down_b, layer, xf, ln_ffn_g[layer], ln_ffn_b[layer], 256)
    return xf.reshape(b, t, d)
```

```python
import functools

import jax
import jax.numpy as jnp
from jax import lax
from jax.experimental import pallas as pl
from jax.experimental.pallas import tpu as pltpu

D_MODEL = 2048
DEPTH = 4
GRID_W = 64
HALF = 1024
HEAD_DIM = 128
NA_HEADS = 8
NA_WIN_ROWS = 8
NA_WIN_COLS = 16
HG_HEADS = 8
HG_CHUNK = 128
POOL_WINDOWS = (2, 4, 8, 16)
POOL_GROUP_DIM = 256
GQA_Q_HEADS = 8
GQA_KV_HEADS = 2
KV_W = GQA_KV_HEADS * HEAD_DIM
ROPE_THETA = 10000.0
FFN_HIDDEN = 5632
DN_ALPHA = (2 * DEPTH) ** 0.25
LN_EPS = 1e-5
RMS_EPS = 1e-6
NEG_BIG = -1e30
LOG2E = 1.4426950408889634
ATTN_SCALE_LOG2 = HEAD_DIM ** -0.5 * LOG2E

V7X_VMEM_LIMIT_BYTES = 56 * 1024 * 1024

AB_IN_TILE = (1024, 1024)
CD_IN_TILE = (1024, 1280)
MIX_OUT_ROWS = 512
MIX_OUT_PARTS = 4
FFN_UP_TILE = (1024, 512)
FFN_DOWN_ROWS = 256
QKV_PREP_ROWS = 512
GQA_QUERY_ROWS = 512

F32 = jnp.float32
BF16 = jnp.bfloat16


def _params(semantics):
    return pltpu.CompilerParams(dimension_semantics=semantics,
                                vmem_limit_bytes=V7X_VMEM_LIMIT_BYTES)


def _sigmoid(x):
    return 1.0 / (1.0 + jnp.exp(-x))


def _silu(x):
    return x * _sigmoid(x)


def _dot(a, b):
    return jnp.dot(a, b, preferred_element_type=F32)


def _dot_nt(a, b):
    return lax.dot_general(a, b, (((1,), (1,)), ((), ())), preferred_element_type=F32)


def _dot_tn(a, b):
    return lax.dot_general(a, b, (((0,), (0,)), ((), ())), preferred_element_type=F32)


def _matmul_kernel(a_ref, w_ref, o_ref, wb_ref):
    @pl.when(pl.program_id(1) == 0)
    def _():
        wb_ref[...] = w_ref[...].astype(BF16)

    o_ref[...] = _dot(a_ref[...], wb_ref[...]).astype(o_ref.dtype)


def _matmul(a, w, layer, out_dtype, tm, tn, name):
    m, k = a.shape
    n = w.shape[2]
    return pl.pallas_call(
        _matmul_kernel,
        grid=(n // tn, m // tm),
        in_specs=[pl.BlockSpec((tm, k), lambda j, i: (i, 0)),
                  pl.BlockSpec((None, k, tn), lambda j, i: (layer, 0, j))],
        out_specs=pl.BlockSpec((tm, tn), lambda j, i: (i, j)),
        out_shape=jax.ShapeDtypeStruct((m, n), out_dtype),
        scratch_shapes=[pltpu.VMEM((k, tn), BF16)],
        compiler_params=_params(("arbitrary", "arbitrary")),
        name=name,
    )(a, w)


def _res_ln(acc, x, g, b):
    z = DN_ALPHA * x + acc
    mu = jnp.mean(z, axis=-1, keepdims=True)
    zc = z - mu
    var = jnp.mean(zc * zc, axis=-1, keepdims=True)
    return zc * lax.rsqrt(var + LN_EPS) * g + b


def _row_parts(tm, parts):
    size = tm // parts
    return [slice(p * size, (p + 1) * size) for p in range(parts)]


def _mix_out_kernel(y1_ref, y2_ref, w_ref, x_ref, g_ref, b_ref, of_ref, ob_ref, *, parts):
    half = y1_ref.shape[1]
    rows = _row_parts(y1_ref.shape[0], parts)
    accs = [_dot(y1_ref[r, :], w_ref[:half, :]) + _dot(y2_ref[r, :], w_ref[half:, :]) for r in rows]
    for r, acc in zip(rows, accs):
        out = _res_ln(acc, x_ref[r, :], g_ref[...], b_ref[...])
        of_ref[r, :] = out
        ob_ref[r, :] = out.astype(BF16)


def _mix_out(y1, y2, w, layer, x, g, b, tm):
    m, half = y1.shape
    d = w.shape[2]
    row = lambda i: (i, 0)
    fixed = lambda i: (0, 0)
    return pl.pallas_call(
        functools.partial(_mix_out_kernel, parts=MIX_OUT_PARTS),
        grid=(m // tm,),
        in_specs=[pl.BlockSpec((tm, half), row), pl.BlockSpec((tm, half), row),
                  pl.BlockSpec((None, 2 * half, d), lambda i: (layer, 0, 0),
                               pipeline_mode=pl.Buffered(1)),
                  pl.BlockSpec((tm, d), row),
                  pl.BlockSpec((1, d), fixed), pl.BlockSpec((1, d), fixed)],
        out_specs=[pl.BlockSpec((tm, d), row), pl.BlockSpec((tm, d), row)],
        out_shape=[jax.ShapeDtypeStruct((m, d), F32), jax.ShapeDtypeStruct((m, d), BF16)],
        compiler_params=_params(("parallel",)),
        name="mix_out_ln",
    )(y1, y2, w, x, g.reshape(1, d), b.reshape(1, d))


def _ffn_up_kernel(x_ref, wg_ref, wu_ref, h_ref, wgb_ref, wub_ref):
    @pl.when(pl.program_id(1) == 0)
    def _():
        wgb_ref[...] = wg_ref[...].astype(BF16)
        wub_ref[...] = wu_ref[...].astype(BF16)

    x = x_ref[...]
    gate = _dot(x, wgb_ref[...])
    up = _dot(x, wub_ref[...])
    h_ref[...] = (_silu(gate) * up).astype(h_ref.dtype)


def _ffn_up(xb, wg, wu, layer, tm, tn):
    m, k = xb.shape
    n = wg.shape[2]
    return pl.pallas_call(
        _ffn_up_kernel,
        grid=(n // tn, m // tm),
        in_specs=[pl.BlockSpec((tm, k), lambda j, i: (i, 0)),
                  pl.BlockSpec((None, k, tn), lambda j, i: (layer, 0, j)),
                  pl.BlockSpec((None, k, tn), lambda j, i: (layer, 0, j))],
        out_specs=pl.BlockSpec((tm, tn), lambda j, i: (i, j)),
        out_shape=jax.ShapeDtypeStruct((m, n), BF16),
        scratch_shapes=[pltpu.VMEM((k, tn), BF16), pltpu.VMEM((k, tn), BF16)],
        compiler_params=_params(("arbitrary", "arbitrary")),
        name="ffn_up",
    )(xb, wg, wu)


def _ffn_down_kernel(h_ref, w_ref, x_ref, g_ref, b_ref, of_ref, ob_ref):
    out = _res_ln(_dot(h_ref[...], w_ref[...]), x_ref[...], g_ref[...], b_ref[...])
    of_ref[...] = out
    ob_ref[...] = out.astype(BF16)


def _ffn_down(h, w, layer, x, g, b, tm):
    m, kk = h.shape
    d = w.shape[2]
    row = lambda i: (i, 0)
    fixed = lambda i: (0, 0)
    return pl.pallas_call(
        _ffn_down_kernel,
        grid=(m // tm,),
        in_specs=[pl.BlockSpec((tm, kk), row),
                  pl.BlockSpec((None, kk, d), lambda i: (layer, 0, 0), pipeline_mode=pl.Buffered(1)),
                  pl.BlockSpec((tm, d), row),
                  pl.BlockSpec((1, d), fixed), pl.BlockSpec((1, d), fixed)],
        out_specs=[pl.BlockSpec((tm, d), row), pl.BlockSpec((tm, d), row)],
        out_shape=[jax.ShapeDtypeStruct((m, d), F32), jax.ShapeDtypeStruct((m, d), BF16)],
        compiler_params=_params(("parallel",)),
        name="ffn_down_ln",
    )(h, w, x, g.reshape(1, d), b.reshape(1, d))


NA_GROUP = 4
NA_WINDOW = NA_GROUP + NA_WIN_ROWS
RPB_ROWS = 2 * NA_WIN_ROWS - 1
RPB_COLS = 2 * NA_WIN_COLS - 1


def _na_window_start(g, rows):
    return min(max(g * NA_GROUP - NA_WIN_ROWS // 2, 0), rows - NA_WINDOW)


def _na_patterns(rows):
    pats, ids = [], []
    for r in range(rows):
        w0 = _na_window_start(r // NA_GROUP, rows)
        r0 = min(max(r - NA_WIN_ROWS // 2, 0), rows - NA_WIN_ROWS)
        key = (r0 - w0, w0 - r + NA_WIN_ROWS - 1)
        if key not in pats:
            pats.append(key)
        ids.append(pats.index(key))
    return tuple(pats), tuple(ids)


def _na_band_kernel(rpb_ref, o_ref, *, patterns):
    h = pl.program_id(0)
    base = h * (RPB_ROWS * RPB_COLS)
    shape = (GRID_W, 2 * GRID_W)
    q = lax.broadcasted_iota(jnp.int32, shape, 0)
    lane = lax.broadcasted_iota(jnp.int32, shape, 1)
    kc = lane & (GRID_W - 1)
    second = lane >> 6
    code = kc - q + (NA_WIN_COLS - 1) + RPB_COLS * second
    start = jnp.clip(q - NA_WIN_COLS // 2, 0, GRID_W - NA_WIN_COLS)
    valid = (kc >= start) & (kc < start + NA_WIN_COLS)
    neg = jnp.full(shape, NEG_BIG, F32)
    pairs = {}

    def pair(ri):
        if ri not in pairs:
            acc = neg
            for i in range(2 * RPB_COLS):
                if 0 <= ri + i // RPB_COLS < RPB_ROWS:
                    acc = jnp.where(code == i, rpb_ref[base + ri * RPB_COLS + i], acc)
            pairs[ri] = jnp.where(valid, acc * LOG2E, NEG_BIG)
        return pairs[ri]

    for p, (lo, c) in enumerate(patterns):
        for u in range(NA_WINDOW // 2):
            in0 = lo <= 2 * u < lo + NA_WIN_ROWS
            in1 = lo <= 2 * u + 1 < lo + NA_WIN_ROWS
            if in0 and in1:
                tile = pair(2 * u + c)
            elif in0:
                tile = jnp.where(second == 0, pair(2 * u + c), NEG_BIG)
            elif in1:
                tile = jnp.where(second == 1, pair(2 * u + c), NEG_BIG)
            else:
                tile = neg
            o_ref[0, p, :, u * 2 * GRID_W:(u + 1) * 2 * GRID_W] = tile


def _na_band(rpb, patterns):
    shape = (NA_HEADS, len(patterns), GRID_W, NA_WINDOW * GRID_W)
    return pl.pallas_call(
        functools.partial(_na_band_kernel, patterns=patterns),
        grid=(NA_HEADS,),
        in_specs=[pl.BlockSpec(memory_space=pltpu.SMEM)],
        out_specs=pl.BlockSpec((1,) + shape[1:], lambda h: (h, 0, 0, 0)),
        out_shape=jax.ShapeDtypeStruct(shape, F32),
        compiler_params=_params(("arbitrary",)),
        name="na_band",
    )(rpb.reshape(-1))


def _na_kernel(q_ref, k_ref, v_ref, band_ref, o_ref, kt_ref, vb_ref, *, pattern_ids):
    t = q_ref.shape[1]
    rows = t // GRID_W
    gq = NA_GROUP * GRID_W
    wlen = NA_WINDOW * GRID_W
    n_groups = rows // NA_GROUP
    for c in range(0, t, gq):
        kt_ref[:, c:c + gq] = k_ref[0, c:c + gq, :].T.astype(BF16)
    vb_ref[:, :HEAD_DIM] = v_ref[0].astype(BF16)
    vb_ref[:, HEAD_DIM:] = jnp.ones((t, HEAD_DIM), BF16)
    scale = ATTN_SCALE_LOG2

    def scores(g):
        w0 = _na_window_start(g, rows) * GRID_W
        q = (q_ref[0, g * gq:(g + 1) * gq, :] * scale).astype(BF16)
        bias = jnp.concatenate(
            [band_ref[0, pattern_ids[g * NA_GROUP + i]] for i in range(NA_GROUP)], axis=0)
        return _dot(q, kt_ref[:, w0:w0 + wlen]) + bias

    s_next = scores(0)
    for g in range(n_groups):
        s = s_next
        if g + 1 < n_groups:
            s_next = scores(g + 1)
        w0 = _na_window_start(g, rows) * GRID_W
        m = jnp.max(s, axis=-1, keepdims=True)
        p = jnp.exp2(s - m).astype(BF16)
        ol = _dot(p, vb_ref[w0:w0 + wlen, :])
        o = ol[:, :HEAD_DIM] / ol[:, HEAD_DIM:]
        o_ref[0, g * gq:(g + 1) * gq, :] = o.astype(o_ref.dtype)


def _na_attention(proj, rpb):
    b, t, _ = proj.shape
    nh = NA_HEADS
    patterns, pattern_ids = _na_patterns(t // GRID_W)
    band = _na_band(rpb, patterns)
    blk = (1, t, HEAD_DIM)
    return pl.pallas_call(
        functools.partial(_na_kernel, pattern_ids=pattern_ids),
        grid=(nh, b),
        in_specs=[pl.BlockSpec(blk, lambda h, i: (i, 0, h)),
                  pl.BlockSpec(blk, lambda h, i: (i, 0, nh + h)),
                  pl.BlockSpec(blk, lambda h, i: (i, 0, 2 * nh + h)),
                  pl.BlockSpec((1,) + band.shape[1:], lambda h, i: (h, 0, 0, 0))],
        out_specs=pl.BlockSpec(blk, lambda h, i: (i, 0, h)),
        out_shape=jax.ShapeDtypeStruct((b, t, HALF), BF16),
        scratch_shapes=[pltpu.VMEM((HEAD_DIM, t), BF16), pltpu.VMEM((t, 2 * HEAD_DIM), BF16)],
        compiler_params=_params(("parallel", "parallel")),
        name="na_attention",
    )(proj, proj, proj, band)


HG_LEVELS = tuple(HG_CHUNK >> (i + 1) for i in range(HG_CHUNK.bit_length() - 1))
HG_DIAG = len(HG_LEVELS)


def _split2(x):
    hi = x.astype(BF16)
    mid = (x - hi.astype(F32)).astype(BF16)
    return hi, mid


def _level_codes(c, rev):
    t = lax.broadcasted_iota(jnp.int32, (c, c), 0)
    s = lax.broadcasted_iota(jnp.int32, (c, c), 1)
    x = t ^ s
    code = jnp.full((c, c), HG_DIAG, jnp.int32)
    for li in reversed(range(len(HG_LEVELS))):
        code = jnp.where(x >= HG_LEVELS[li], li, code)
    return jnp.where((s >= t) if rev else (s <= t), code, -1)


def _pivot_rows(gc, m, rev):
    c = gc.shape[0]
    blk = 2 * m
    if blk >= 8:
        piv = m if rev else m - 1
        parts = []
        for s in range(0, c, blk):
            parts.append(jnp.broadcast_to(gc[s + piv:s + piv + 1, :], (blk, gc.shape[1])))
        return parts[0] if len(parts) == 1 else jnp.concatenate(parts, axis=0)
    pos = lax.broadcasted_iota(jnp.int32, gc.shape, 0) & (blk - 1)
    piv = m if rev else m - 1
    out = gc
    for p in range(blk):
        if p == piv:
            continue
        shift = (p - piv) % c
        out = jnp.where(pos == p, pltpu.roll(gc, shift, 0), out)
    return out


def _neg_abs(x):
    bits = lax.bitcast_convert_type(x, jnp.uint32) | jnp.uint32(0x80000000)
    return lax.bitcast_convert_type(bits, F32)


def _hgrn_prepare(zs, qs, lb, tri_ref, codes_ref):
    c = qs[0].shape[0]
    sub = 8
    dirs = (False, True)
    f = [lb + (1.0 - lb) * _sigmoid(z) for z in zs]
    k = [1.0 - x for x in f]
    parts = [_split2(jnp.log2(x)) for x in f]
    gc = [_dot(tri_ref[d], parts[d][0]) + _dot(tri_ref[d], parts[d][1]) for d in range(2)]
    pos = lax.broadcasted_iota(jnp.int32, qs[0].shape, 0)

    def tiles(x):
        return [x[i:i + sub] for i in range(0, c, sub)]

    def code_tiles(d):
        return [codes_ref[d, i:i + sub, :] for i in range(0, c, sub)]

    a = []
    for d in range(2):
        diag = jnp.sum(qs[d] * k[d], axis=-1, keepdims=True)
        a.append([jnp.where(ct == HG_DIAG, dt, 0.0) for ct, dt in zip(code_tiles(d), tiles(diag))])

    def merge(d, li, prod):
        a[d] = [jnp.where(ct == li, pt, at) for ct, pt, at in zip(code_tiles(d), tiles(prod), a[d])]

    def small_level(m, decay):
        li = HG_LEVELS.index(m)
        for d, rev in enumerate(dirs):
            is_q = ((pos & m) == 0) if rev else ((pos & m) != 0)
            xl = (jnp.where(is_q, qs[d], k[d]) * decay[d]).astype(BF16)
            merge(d, li, _dot_nt(xl, xl))

    if 1 in HG_LEVELS:
        ones = jnp.ones_like(qs[0])
        dec1 = []
        for d, rev in enumerate(dirs):
            is_q = ((pos & 1) == 0) if rev else ((pos & 1) != 0)
            dec1.append(jnp.where(is_q, f[d], ones))
        small_level(1, dec1)
    if 2 in HG_LEVELS:
        p4 = pos & 3
        dec2 = []
        for d, rev in enumerate(dirs):
            nxt = pltpu.roll(f[d], c - 1, 0)
            prv = pltpu.roll(f[d], 1, 0)
            if rev:
                e = jnp.where(p4 == 0, f[d] * nxt, jnp.where(p4 == 1, f[d], jnp.where(p4 == 2, 1.0, prv)))
            else:
                e = jnp.where(p4 == 0, nxt, jnp.where(p4 == 1, 1.0, jnp.where(p4 == 2, f[d], f[d] * prv)))
            dec2.append(e)
        small_level(2, dec2)

    for li, m in enumerate(HG_LEVELS):
        if m <= 2:
            continue
        e = [jnp.exp2(_neg_abs(gc[d] - _pivot_rows(gc[d], m, rev))) for d, rev in enumerate(dirs)]
        if m < sub:
            small_level(m, e)
            continue
        for d, rev in enumerate(dirs):
            q_rows, rhs = [], []
            for s in range(0, c, 2 * m):
                lo, hi = slice(s, s + m), slice(s + m, s + 2 * m)
                q_sl, k_sl = (lo, hi) if rev else (hi, lo)
                q_rows.append(q_sl)
                kx = k[d][k_sl] * e[d][k_sl]
                rhs += [k[d][lo], kx] if rev else [kx, k[d][hi]]
            lhs = jnp.concatenate([qs[d][r] * e[d][r] for r in q_rows], axis=0).astype(BF16)
            prod = tiles(_dot_nt(lhs, jnp.concatenate(rhs, axis=0).astype(BF16)))
            ct = code_tiles(d)
            j = 0
            for r in q_rows:
                for i in range(r.start // sub, r.stop // sub):
                    a[d][i] = jnp.where(ct[i] == li, prod[j], a[d][i])
                    j += 1

    out = []
    for d, rev in enumerate(dirs):
        g_tot = gc[d][0:1, :] if rev else gc[d][c - 1:c, :]
        qg = (qs[d] * jnp.exp2(gc[d])).astype(BF16)
        kd = (k[d] * jnp.exp2(g_tot - gc[d])).astype(BF16)
        out.append((jnp.concatenate(a[d], axis=0).astype(BF16), qg, kd, jnp.exp2(g_tot)))
    return out


HG_DEC_ROWS = 8
HG_PASS_ROWS = 256


def _row_block(idx, size):
    start = idx * size
    return pl.ds(start if isinstance(start, int) else pl.multiple_of(start, size), size)


def _hgrn_kernel(q_ref, ff_ref, fb_ref, i_ref, g_ref, lbl_ref, nw_ref, o_ref,
                 tri_ref, codes_ref, a_ref, qg_ref, kd_ref, dec_ref, qa_ref, vb_ref, of_ref, ob_ref,
                 *, layer):
    t = q_ref.shape[1]
    c = HG_CHUNK
    n = t // c
    dk = q_ref.shape[2]
    dv = i_ref.shape[2]

    lg = lbl_ref[...]
    e = jnp.exp(lg - jnp.max(lg, axis=0, keepdims=True))
    p = e / jnp.sum(e, axis=0, keepdims=True)
    lb = jnp.sum(p[:layer + 1, :], axis=0, keepdims=True) - p[0:1, :]

    row = lax.broadcasted_iota(jnp.int32, (c, c), 0)
    col = lax.broadcasted_iota(jnp.int32, (c, c), 1)
    tri_ref[0] = jnp.where(col <= row, 1.0, 0.0).astype(BF16)
    tri_ref[1] = jnp.where(col >= row, 1.0, 0.0).astype(BF16)
    codes_ref[0] = _level_codes(c, False)
    codes_ref[1] = _level_codes(c, True)

    act_rows = HG_PASS_ROWS

    def activate(i, carry):
        rows = _row_block(i, act_rows)
        qa_ref[rows, :] = _silu(q_ref[0, rows, :])
        vb_ref[rows, :] = i_ref[0, rows, :].astype(BF16)
        return carry

    lax.fori_loop(0, t // act_rows, activate, 0)

    def chunk_rows(i):
        return [_row_block(i, c), _row_block(n - 1 - i, c)]

    def dec_rows(i, size):
        return [pl.ds(r.start, size) for r in (_row_block(i, HG_DEC_ROWS), _row_block(n - 1 - i, HG_DEC_ROWS))]

    def prepare(i):
        rows = chunk_rows(i)
        both = _hgrn_prepare([ff_ref[0, rows[0], :], fb_ref[0, rows[1], :]],
                             [qa_ref[rows[0], :], qa_ref[rows[1], :]], lb, tri_ref, codes_ref)
        for d, (a, qg, kd, dec) in enumerate(both):
            a_ref[d, rows[d], :] = a
            qg_ref[d, rows[d], :] = qg
            kd_ref[d, rows[d], :] = kd
            dec_ref[d, dec_rows(i, HG_DEC_ROWS)[d], :] = jnp.broadcast_to(dec, (HG_DEC_ROWS, dk))

    def recur(i, states):
        rows = chunk_rows(i)
        new = []
        for d, out_ref in enumerate((of_ref, ob_ref)):
            vb = vb_ref[rows[d], :]
            out_ref[rows[d], :] = (_dot(a_ref[d, rows[d], :], vb)
                                   + _dot_nt(qg_ref[d, rows[d], :], states[d].astype(BF16)))
            dec = dec_ref[d, dec_rows(i, 1)[d], :]
            new.append(states[d] * dec + _dot_tn(vb, kd_ref[d, rows[d], :]))
        return tuple(new)

    def body(i, states):
        states = recur(i, states)
        prepare(i + 1)
        return states

    zero = jnp.zeros((dv, dk), F32)
    prepare(0)
    recur(n - 1, lax.fori_loop(0, n - 1, body, (zero, zero), unroll=3))

    nw = nw_ref[...]
    rows = HG_PASS_ROWS

    def finish(i, carry):
        off = pl.multiple_of(i * rows, rows)
        o = of_ref[pl.ds(off, rows), :] + ob_ref[pl.ds(off, rows), :]
        ms = jnp.mean(o * o, axis=-1, keepdims=True)
        y = o * lax.rsqrt(ms + RMS_EPS) * nw
        o_ref[0, pl.ds(off, rows), :] = (y * _silu(g_ref[0, pl.ds(off, rows), :])).astype(o_ref.dtype)
        return carry

    lax.fori_loop(0, t // rows, finish, 0, unroll=4)


def _hgrn(proj, lb_logits, norm_w, layer):
    b, t, _ = proj.shape
    nh = HG_HEADS
    n_layers = lb_logits.shape[0]
    blk = (1, t, HEAD_DIM)
    base = 3 * HALF // HEAD_DIM

    def col(k):
        return lambda i, h: (i, 0, base + k * nh + h)

    return pl.pallas_call(
        functools.partial(_hgrn_kernel, layer=layer),
        grid=(b, nh),
        in_specs=[pl.BlockSpec(blk, col(0)), pl.BlockSpec(blk, col(1)), pl.BlockSpec(blk, col(2)),
                  pl.BlockSpec(blk, col(3)), pl.BlockSpec(blk, col(4)),
                  pl.BlockSpec((n_layers, HEAD_DIM), lambda i, h: (0, h)),
                  pl.BlockSpec((1, HEAD_DIM), lambda i, h: (0, h))],
        out_specs=pl.BlockSpec(blk, lambda i, h: (i, 0, h)),
        out_shape=jax.ShapeDtypeStruct((b, t, HALF), BF16),
        scratch_shapes=[pltpu.VMEM((2, HG_CHUNK, HG_CHUNK), BF16),
                        pltpu.VMEM((2, HG_CHUNK, HG_CHUNK), jnp.int32),
                        pltpu.VMEM((2, t, HG_CHUNK), BF16),
                        pltpu.VMEM((2, t, HEAD_DIM), BF16),
                        pltpu.VMEM((2, t, HEAD_DIM), BF16),
                        pltpu.VMEM((2, t // HG_CHUNK * HG_DEC_ROWS, HEAD_DIM), F32),
                        pltpu.VMEM((t, HEAD_DIM), F32),
                        pltpu.VMEM((t, HEAD_DIM), BF16),
                        pltpu.VMEM((t, HEAD_DIM), F32), pltpu.VMEM((t, HEAD_DIM), F32)],
        compiler_params=_params(("parallel", "parallel")),
        name="hgrn2",
    )(proj, proj, proj, proj, proj, lb_logits, norm_w.reshape(1, HALF))


POOL_PAD = 16
POOL_ROWS = 256


def _pool_kernel(x_ref, w_ref, s_ref, o_ref, xp_ref, sa_ref, sb_ref):
    t = x_ref.shape[1]
    width = x_ref.shape[2]
    gd = POOL_GROUP_DIM
    half = POOL_PAD // 2
    lo, hi = half, t + POOL_PAD + half
    xp_ref[0:POOL_PAD, :] = jnp.zeros((POOL_PAD, width), F32)
    xp_ref[POOL_PAD + t:, :] = jnp.zeros((POOL_PAD, width), F32)
    xp_ref[POOL_PAD:POOL_PAD + t, :] = x_ref[0]
    for ref in (sa_ref, sb_ref):
        ref[0:lo, :] = jnp.zeros((lo, gd), F32)
        ref[hi:, :] = jnp.zeros((t + 2 * POOL_PAD - hi, gd), F32)

    for gi, win in enumerate(POOL_WINDOWS):
        c0, c1 = gi * gd, (gi + 1) * gd
        sa_ref[lo:hi, :] = xp_ref[lo - 1:hi - 1, c0:c1] + xp_ref[lo:hi, c0:c1]
        src, dst, span = sa_ref, sb_ref, 2
        while span < win:
            sh = span // 2
            dst[lo:hi, :] = src[lo - sh:hi - sh, :] + src[lo + sh:hi + sh, :]
            src, dst, span = dst, src, 2 * span
        hw = win // 2
        for t0 in range(0, t, POOL_ROWS):
            pos = t0 + lax.broadcasted_iota(jnp.int32, (POOL_ROWS, gd), 0)
            cnt = (jnp.minimum(pos + hw, t) - jnp.maximum(pos - hw, 0)).astype(F32)
            tot = src[POOL_PAD + t0:POOL_PAD + t0 + POOL_ROWS, :]
            pooled = tot / cnt - x_ref[0, t0:t0 + POOL_ROWS, c0:c1]
            y = _dot(pooled.astype(BF16), w_ref[gi]) * s_ref[:, c0:c1]
            o_ref[0, t0:t0 + POOL_ROWS, c0:c1] = y.astype(o_ref.dtype)


def _pool(proj, pool_w, scale):
    b, t, _ = proj.shape
    ng = len(POOL_WINDOWS)
    return pl.pallas_call(
        _pool_kernel,
        grid=(b,),
        in_specs=[pl.BlockSpec((1, t, HALF), lambda i: (i, 0, 0)),
                  pl.BlockSpec((ng, POOL_GROUP_DIM, POOL_GROUP_DIM), lambda i: (0, 0, 0)),
                  pl.BlockSpec((1, HALF), lambda i: (0, 0))],
        out_specs=pl.BlockSpec((1, t, HALF), lambda i: (i, 0, 0)),
        out_shape=jax.ShapeDtypeStruct((b, t, HALF), BF16),
        scratch_shapes=[pltpu.VMEM((t + 2 * POOL_PAD, HALF), F32),
                        pltpu.VMEM((t + 2 * POOL_PAD, POOL_GROUP_DIM), F32),
                        pltpu.VMEM((t + 2 * POOL_PAD, POOL_GROUP_DIM), F32)],
        compiler_params=_params(("parallel",)),
        name="pool",
    )(proj, pool_w, scale.reshape(1, HALF))


def _rope_tables(t):
    pos = jnp.arange(t)
    row = (pos // GRID_W).astype(F32)
    col = (pos % GRID_W).astype(F32)
    n_freq = HEAD_DIM // 4
    inv = ROPE_THETA ** (-jnp.arange(n_freq, dtype=F32) / n_freq)
    ang = jnp.concatenate([row[:, None] * inv, col[:, None] * inv], axis=-1)
    cos = jnp.repeat(jnp.cos(ang), 2, axis=-1)
    sin = jnp.repeat(jnp.sin(ang), 2, axis=-1)
    sign = jnp.where(jnp.arange(HEAD_DIM) % 2 == 0, -1.0, 1.0).astype(F32)
    return cos, sin * sign


def _norm_rope(x, g, cos, sin_signed, scale):
    ms = jnp.mean(x * x, axis=-1, keepdims=True)
    xn = x * lax.rsqrt(ms + RMS_EPS) * g
    lane = lax.broadcasted_iota(jnp.int32, x.shape, 1)
    partner = jnp.where((lane & 1) == 0, pltpu.roll(xn, HEAD_DIM - 1, 1), pltpu.roll(xn, 1, 1))
    return (xn * cos + partner * sin_signed) * scale


def _qkv_prep_kernel(q_ref, k_ref, v_ref, qn_ref, kn_ref, cos_ref, sin_ref, qo_ref, kt_ref, vo_ref):
    cos = cos_ref[...]
    sin = sin_ref[...]
    for h in range(GQA_Q_HEADS):
        sl = slice(h * HEAD_DIM, (h + 1) * HEAD_DIM)
        qo_ref[:, sl] = _norm_rope(q_ref[:, sl], qn_ref[...], cos, sin,
                                   ATTN_SCALE_LOG2).astype(qo_ref.dtype)
    for h in range(GQA_KV_HEADS):
        sl = slice(h * HEAD_DIM, (h + 1) * HEAD_DIM)
        kt_ref[0, sl, :] = _norm_rope(k_ref[:, sl], kn_ref[...], cos, sin, 1.0).T.astype(kt_ref.dtype)
    ones = jnp.ones((v_ref.shape[0], HEAD_DIM), vo_ref.dtype)
    for h in range(GQA_KV_HEADS):
        vo_ref[:, 2 * h * HEAD_DIM:(2 * h + 1) * HEAD_DIM] = (
            v_ref[:, h * HEAD_DIM:(h + 1) * HEAD_DIM].astype(vo_ref.dtype))
        vo_ref[:, (2 * h + 1) * HEAD_DIM:(2 * h + 2) * HEAD_DIM] = ones


def _qkv_prep(proj2d, q_norm, k_norm, cos, sin, b, t, tm):
    m = proj2d.shape[0]
    tb = t // tm
    kv_blk = (2 * HALF) // KV_W
    return pl.pallas_call(
        _qkv_prep_kernel,
        grid=(m // tm,),
        in_specs=[pl.BlockSpec((tm, HALF), lambda i: (i, 1)),
                  pl.BlockSpec((tm, KV_W), lambda i: (i, kv_blk)),
                  pl.BlockSpec((tm, KV_W), lambda i: (i, kv_blk + 1)),
                  pl.BlockSpec((1, HEAD_DIM), lambda i: (0, 0)),
                  pl.BlockSpec((1, HEAD_DIM), lambda i: (0, 0)),
                  pl.BlockSpec((tm, HEAD_DIM), lambda i: (i % tb, 0)),
                  pl.BlockSpec((tm, HEAD_DIM), lambda i: (i % tb, 0))],
        out_specs=[pl.BlockSpec((tm, HALF), lambda i: (i, 0)),
                   pl.BlockSpec((1, KV_W, tm), lambda i: (i // tb, 0, i % tb)),
                   pl.BlockSpec((tm, 2 * KV_W), lambda i: (i, 0))],
        out_shape=[jax.ShapeDtypeStruct((m, HALF), BF16),
                   jax.ShapeDtypeStruct((b, KV_W, t), BF16),
                   jax.ShapeDtypeStruct((m, 2 * KV_W), BF16)],
        compiler_params=_params(("parallel",)),
        name="qkv_prep",
    )(proj2d, proj2d, proj2d, q_norm.reshape(1, HEAD_DIM), k_norm.reshape(1, HEAD_DIM), cos, sin)


GQA_GROUP = GQA_Q_HEADS // GQA_KV_HEADS


def _gqa_kernel(q_ref, kt_ref, v_ref, o_ref):
    def scores(h):
        return _dot(q_ref[0, :, h * HEAD_DIM:(h + 1) * HEAD_DIM], kt_ref[0])

    s_next = scores(0)
    for h in range(GQA_GROUP):
        s = s_next
        if h + 1 < GQA_GROUP:
            s_next = scores(h + 1)
        m = jnp.max(s, axis=-1, keepdims=True)
        p = jnp.exp2(s - m).astype(BF16)
        ol = _dot(p, v_ref[0])
        o = ol[:, :HEAD_DIM] / ol[:, HEAD_DIM:]
        o_ref[0, :, h * HEAD_DIM:(h + 1) * HEAD_DIM] = o.astype(o_ref.dtype)


def _gqa(q, kt, v, tq):
    b, t, _ = q.shape
    gw = GQA_GROUP * HEAD_DIM
    return pl.pallas_call(
        _gqa_kernel,
        grid=(b, GQA_KV_HEADS, t // tq),
        in_specs=[pl.BlockSpec((1, tq, gw), lambda i, h, j: (i, j, h)),
                  pl.BlockSpec((1, HEAD_DIM, t), lambda i, h, j: (i, h, 0)),
                  pl.BlockSpec((1, t, 2 * HEAD_DIM), lambda i, h, j: (i, 0, h))],
        out_specs=pl.BlockSpec((1, tq, gw), lambda i, h, j: (i, j, h)),
        out_shape=jax.ShapeDtypeStruct((b, t, HALF), BF16),
        compiler_params=_params(("parallel", "parallel", "parallel")),
        name="gqa",
    )(q, kt, v)


def kernel(x, ab_w_in, ab_w_out, na_rpb, hg_lb_logits, hg_norm_w, cd_w_in, cd_w_out, pool_w,
           pool_scale, d_q_norm, d_k_norm, ln_mix_g, ln_mix_b, ffn_w_gate, ffn_w_up, ffn_w_down,
           ln_ffn_g, ln_ffn_b):
    b, t, d = x.shape
    m = b * t
    cos, sin = _rope_tables(t)
    xf = x.reshape(m, d)
    xb = xf.astype(BF16)
    ab_w_out_b = ab_w_out.astype(BF16)
    cd_w_out_b = cd_w_out.astype(BF16)
    pool_w_b = pool_w.astype(BF16)
    ffn_w_down_b = ffn_w_down.astype(BF16)
    for layer in range(DEPTH):
        j = layer // 2
        if layer % 2 == 0:
            proj = _matmul(xb, ab_w_in, j, F32, *AB_IN_TILE, "ab_in_proj").reshape(b, t, -1)
            y1 = _na_attention(proj, na_rpb[j])
            y2 = _hgrn(proj, hg_lb_logits, hg_norm_w[j], j)
            w_out = ab_w_out_b
        else:
            proj = _matmul(xb, cd_w_in, j, F32, *CD_IN_TILE, "cd_in_proj")
            y1 = _pool(proj.reshape(b, t, -1), pool_w_b[j], pool_scale[j])
            qn, kt, vn = _qkv_prep(proj, d_q_norm[j], d_k_norm[j], cos, sin, b, t, QKV_PREP_ROWS)
            y2 = _gqa(qn.reshape(b, t, HALF), kt, vn.reshape(b, t, 2 * KV_W), GQA_QUERY_ROWS)
            w_out = cd_w_out_b
        xf, xb = _mix_out(y1.reshape(m, HALF), y2.reshape(m, HALF), w_out, j, xf,
                          ln_mix_g[layer], ln_mix_b[layer], MIX_OUT_ROWS)
        h = _ffn_up(xb, ffn_w_gate, ffn_w_up, layer, *FFN_UP_TILE)
        xf, xb = _ffn_down(h, ffn_w_down_b, layer, xf, ln_ffn_g[layer], ln_ffn_b[layer],
                           FFN_DOWN_ROWS)
    return xf.reshape(b, t, d)
```

```python
import functools

import jax
import jax.numpy as jnp
from jax import lax
from jax.experimental import pallas as pl
from jax.experimental.pallas import tpu as pltpu

D_MODEL = 2048
DEPTH = 4
GRID_W = 64
HALF = 1024
HEAD_DIM = 128
NA_HEADS = 8
NA_WIN_ROWS = 8
NA_WIN_COLS = 16
HG_HEADS = 8
HG_CHUNK = 128
POOL_WINDOWS = (2, 4, 8, 16)
POOL_GROUP_DIM = 256
GQA_Q_HEADS = 8
GQA_KV_HEADS = 2
KV_W = GQA_KV_HEADS * HEAD_DIM
ROPE_THETA = 10000.0
FFN_HIDDEN = 5632
DN_ALPHA = (2 * DEPTH) ** 0.25
LN_EPS = 1e-5
RMS_EPS = 1e-6
NEG_BIG = -1e30
LOG2E = 1.4426950408889634
ATTN_SCALE_LOG2 = HEAD_DIM ** -0.5 * LOG2E

V7X_VMEM_LIMIT_BYTES = 56 * 1024 * 1024

AB_IN_TILE = (1024, 1024)
CD_IN_TILE = (1024, 1280)
MIX_OUT_ROWS = 512
MIX_OUT_PARTS = 4
FFN_UP_TILE = (1024, 512)
FFN_DOWN_ROWS = 256
QKV_PREP_ROWS = 512
GQA_QUERY_ROWS = 512
NA_BATCH_PER_STEP = 1

F32 = jnp.float32
BF16 = jnp.bfloat16


def _params(semantics):
    return pltpu.CompilerParams(dimension_semantics=semantics,
                                vmem_limit_bytes=V7X_VMEM_LIMIT_BYTES)


def _sigmoid(x):
    return 1.0 / (1.0 + jnp.exp(-x))


def _silu(x):
    return x * _sigmoid(x)


def _dot(a, b):
    return jnp.dot(a, b, preferred_element_type=F32)


def _dot_nt(a, b):
    return lax.dot_general(a, b, (((1,), (1,)), ((), ())), preferred_element_type=F32)


def _dot_tn(a, b):
    return lax.dot_general(a, b, (((0,), (0,)), ((), ())), preferred_element_type=F32)


def _matmul_kernel(a_ref, w_ref, o_ref, wb_ref):
    @pl.when(pl.program_id(1) == 0)
    def _():
        wb_ref[...] = w_ref[...].astype(BF16)

    o_ref[...] = _dot(a_ref[...], wb_ref[...]).astype(o_ref.dtype)


def _matmul(a, w, layer, out_dtype, tm, tn, name):
    m, k = a.shape
    n = w.shape[2]
    return pl.pallas_call(
        _matmul_kernel,
        grid=(n // tn, m // tm),
        in_specs=[pl.BlockSpec((tm, k), lambda j, i: (i, 0)),
                  pl.BlockSpec((None, k, tn), lambda j, i: (layer, 0, j))],
        out_specs=pl.BlockSpec((tm, tn), lambda j, i: (i, j)),
        out_shape=jax.ShapeDtypeStruct((m, n), out_dtype),
        scratch_shapes=[pltpu.VMEM((k, tn), BF16)],
        compiler_params=_params(("arbitrary", "arbitrary")),
        name=name,
    )(a, w)


def _res_ln(acc, x, g, b):
    z = DN_ALPHA * x + acc
    mu = jnp.mean(z, axis=-1, keepdims=True)
    zc = z - mu
    var = jnp.mean(zc * zc, axis=-1, keepdims=True)
    return zc * lax.rsqrt(var + LN_EPS) * g + b


def _row_parts(tm, parts):
    size = tm // parts
    return [slice(p * size, (p + 1) * size) for p in range(parts)]


def _mix_out_kernel(y1_ref, y2_ref, w_ref, x_ref, g_ref, b_ref, of_ref, ob_ref, *, parts):
    half = y1_ref.shape[1]
    rows = _row_parts(y1_ref.shape[0], parts)
    accs = [_dot(y1_ref[r, :], w_ref[:half, :]) + _dot(y2_ref[r, :], w_ref[half:, :]) for r in rows]
    for r, acc in zip(rows, accs):
        out = _res_ln(acc, x_ref[r, :], g_ref[...], b_ref[...])
        of_ref[r, :] = out
        ob_ref[r, :] = out.astype(BF16)


def _mix_out(y1, y2, w, layer, x, g, b, tm):
    m, half = y1.shape
    d = w.shape[2]
    row = lambda i: (i, 0)
    fixed = lambda i: (0, 0)
    return pl.pallas_call(
        functools.partial(_mix_out_kernel, parts=MIX_OUT_PARTS),
        grid=(m // tm,),
        in_specs=[pl.BlockSpec((tm, half), row), pl.BlockSpec((tm, half), row),
                  pl.BlockSpec((None, 2 * half, d), lambda i: (layer, 0, 0),
                               pipeline_mode=pl.Buffered(1)),
                  pl.BlockSpec((tm, d), row),
                  pl.BlockSpec((1, d), fixed), pl.BlockSpec((1, d), fixed)],
        out_specs=[pl.BlockSpec((tm, d), row), pl.BlockSpec((tm, d), row)],
        out_shape=[jax.ShapeDtypeStruct((m, d), F32), jax.ShapeDtypeStruct((m, d), BF16)],
        compiler_params=_params(("parallel",)),
        name="mix_out_ln",
    )(y1, y2, w, x, g.reshape(1, d), b.reshape(1, d))


def _ffn_up_kernel(x_ref, wg_ref, wu_ref, h_ref, wgb_ref, wub_ref):
    @pl.when(pl.program_id(1) == 0)
    def _():
        wgb_ref[...] = wg_ref[...].astype(BF16)
        wub_ref[...] = wu_ref[...].astype(BF16)

    x = x_ref[...]
    gate = _dot(x, wgb_ref[...])
    up = _dot(x, wub_ref[...])
    h_ref[...] = (_silu(gate) * up).astype(h_ref.dtype)


def _ffn_up(xb, wg, wu, layer, tm, tn):
    m, k = xb.shape
    n = wg.shape[2]
    return pl.pallas_call(
        _ffn_up_kernel,
        grid=(n // tn, m // tm),
        in_specs=[pl.BlockSpec((tm, k), lambda j, i: (i, 0)),
                  pl.BlockSpec((None, k, tn), lambda j, i: (layer, 0, j)),
                  pl.BlockSpec((None, k, tn), lambda j, i: (layer, 0, j))],
        out_specs=pl.BlockSpec((tm, tn), lambda j, i: (i, j)),
        out_shape=jax.ShapeDtypeStruct((m, n), BF16),
        scratch_shapes=[pltpu.VMEM((k, tn), BF16), pltpu.VMEM((k, tn), BF16)],
        compiler_params=_params(("arbitrary", "arbitrary")),
        name="ffn_up",
    )(xb, wg, wu)


def _ffn_down_kernel(h_ref, w_ref, x_ref, g_ref, b_ref, of_ref, ob_ref):
    out = _res_ln(_dot(h_ref[...], w_ref[...]), x_ref[...], g_ref[...], b_ref[...])
    of_ref[...] = out
    ob_ref[...] = out.astype(BF16)


def _ffn_down(h, w, layer, x, g, b, tm):
    m, kk = h.shape
    d = w.shape[2]
    row = lambda i: (i, 0)
    fixed = lambda i: (0, 0)
    return pl.pallas_call(
        _ffn_down_kernel,
        grid=(m // tm,),
        in_specs=[pl.BlockSpec((tm, kk), row),
                  pl.BlockSpec((None, kk, d), lambda i: (layer, 0, 0), pipeline_mode=pl.Buffered(1)),
                  pl.BlockSpec((tm, d), row),
                  pl.BlockSpec((1, d), fixed), pl.BlockSpec((1, d), fixed)],
        out_specs=[pl.BlockSpec((tm, d), row), pl.BlockSpec((tm, d), row)],
        out_shape=[jax.ShapeDtypeStruct((m, d), F32), jax.ShapeDtypeStruct((m, d), BF16)],
        compiler_params=_params(("parallel",)),
        name="ffn_down_ln",
    )(h, w, x, g.reshape(1, d), b.reshape(1, d))


NA_GROUP = 4
NA_WINDOW = NA_GROUP + NA_WIN_ROWS
RPB_ROWS = 2 * NA_WIN_ROWS - 1
RPB_COLS = 2 * NA_WIN_COLS - 1


def _na_window_start(g, rows):
    return min(max(g * NA_GROUP - NA_WIN_ROWS // 2, 0), rows - NA_WINDOW)


def _na_patterns(rows):
    pats, ids = [], []
    for r in range(rows):
        w0 = _na_window_start(r // NA_GROUP, rows)
        r0 = min(max(r - NA_WIN_ROWS // 2, 0), rows - NA_WIN_ROWS)
        key = (r0 - w0, w0 - r + NA_WIN_ROWS - 1)
        if key not in pats:
            pats.append(key)
        ids.append(pats.index(key))
    return tuple(pats), tuple(ids)


def _na_band_kernel(rpb_ref, o_ref, *, patterns):
    h = pl.program_id(0)
    base = h * (RPB_ROWS * RPB_COLS)
    shape = (GRID_W, 2 * GRID_W)
    q = lax.broadcasted_iota(jnp.int32, shape, 0)
    lane = lax.broadcasted_iota(jnp.int32, shape, 1)
    kc = lane & (GRID_W - 1)
    second = lane >> 6
    code = kc - q + (NA_WIN_COLS - 1) + RPB_COLS * second
    start = jnp.clip(q - NA_WIN_COLS // 2, 0, GRID_W - NA_WIN_COLS)
    valid = (kc >= start) & (kc < start + NA_WIN_COLS)
    neg = jnp.full(shape, NEG_BIG, F32)
    pairs = {}

    def pair(ri):
        if ri not in pairs:
            acc = neg
            for i in range(2 * RPB_COLS):
                if 0 <= ri + i // RPB_COLS < RPB_ROWS:
                    acc = jnp.where(code == i, rpb_ref[base + ri * RPB_COLS + i], acc)
            pairs[ri] = jnp.where(valid, acc * LOG2E, NEG_BIG)
        return pairs[ri]

    for p, (lo, c) in enumerate(patterns):
        for u in range(NA_WINDOW // 2):
            in0 = lo <= 2 * u < lo + NA_WIN_ROWS
            in1 = lo <= 2 * u + 1 < lo + NA_WIN_ROWS
            if in0 and in1:
                tile = pair(2 * u + c)
            elif in0:
                tile = jnp.where(second == 0, pair(2 * u + c), NEG_BIG)
            elif in1:
                tile = jnp.where(second == 1, pair(2 * u + c), NEG_BIG)
            else:
                tile = neg
            o_ref[0, p, :, u * 2 * GRID_W:(u + 1) * 2 * GRID_W] = tile


def _na_band(rpb, patterns):
    shape = (NA_HEADS, len(patterns), GRID_W, NA_WINDOW * GRID_W)
    return pl.pallas_call(
        functools.partial(_na_band_kernel, patterns=patterns),
        grid=(NA_HEADS,),
        in_specs=[pl.BlockSpec(memory_space=pltpu.SMEM)],
        out_specs=pl.BlockSpec((1,) + shape[1:], lambda h: (h, 0, 0, 0)),
        out_shape=jax.ShapeDtypeStruct(shape, F32),
        compiler_params=_params(("arbitrary",)),
        name="na_band",
    )(rpb.reshape(-1))


def _na_kernel(q_ref, k_ref, v_ref, band_ref, o_ref, kt_ref, vb_ref, *, pattern_ids):
    t = q_ref.shape[1]
    rows = t // GRID_W
    gq = NA_GROUP * GRID_W
    wlen = NA_WINDOW * GRID_W
    n_groups = rows // NA_GROUP
    scale = ATTN_SCALE_LOG2
    for bi in range(q_ref.shape[0]):
        for c in range(0, t, gq):
            kt_ref[:, c:c + gq] = k_ref[bi, c:c + gq, :].T.astype(BF16)
        vb_ref[:, :HEAD_DIM] = v_ref[bi].astype(BF16)
        vb_ref[:, HEAD_DIM:] = jnp.ones((t, HEAD_DIM), BF16)

        def scores(g):
            w0 = _na_window_start(g, rows) * GRID_W
            q = (q_ref[bi, g * gq:(g + 1) * gq, :] * scale).astype(BF16)
            bias = jnp.concatenate(
                [band_ref[0, pattern_ids[g * NA_GROUP + i]] for i in range(NA_GROUP)], axis=0)
            return _dot(q, kt_ref[:, w0:w0 + wlen]) + bias

        s_next = scores(0)
        for g in range(n_groups):
            s = s_next
            if g + 1 < n_groups:
                s_next = scores(g + 1)
            w0 = _na_window_start(g, rows) * GRID_W
            m = jnp.max(s, axis=-1, keepdims=True)
            p = jnp.exp2(s - m).astype(BF16)
            ol = _dot(p, vb_ref[w0:w0 + wlen, :])
            o = ol[:, :HEAD_DIM] / ol[:, HEAD_DIM:]
            o_ref[bi, g * gq:(g + 1) * gq, :] = o.astype(o_ref.dtype)


def _na_attention(proj, rpb):
    b, t, _ = proj.shape
    nh = NA_HEADS
    nb = NA_BATCH_PER_STEP
    patterns, pattern_ids = _na_patterns(t // GRID_W)
    band = _na_band(rpb, patterns)
    blk = (nb, t, HEAD_DIM)
    return pl.pallas_call(
        functools.partial(_na_kernel, pattern_ids=pattern_ids),
        grid=(nh, b // nb),
        in_specs=[pl.BlockSpec(blk, lambda h, i: (i, 0, h)),
                  pl.BlockSpec(blk, lambda h, i: (i, 0, nh + h)),
                  pl.BlockSpec(blk, lambda h, i: (i, 0, 2 * nh + h)),
                  pl.BlockSpec((1,) + band.shape[1:], lambda h, i: (h, 0, 0, 0))],
        out_specs=pl.BlockSpec(blk, lambda h, i: (i, 0, h)),
        out_shape=jax.ShapeDtypeStruct((b, t, HALF), BF16),
        scratch_shapes=[pltpu.VMEM((HEAD_DIM, t), BF16), pltpu.VMEM((t, 2 * HEAD_DIM), BF16)],
        compiler_params=_params(("parallel", "parallel")),
        name="na_attention",
    )(proj, proj, proj, band)


HG_LEVELS = tuple(HG_CHUNK >> (i + 1) for i in range(HG_CHUNK.bit_length() - 1))
HG_DIAG = len(HG_LEVELS)


def _split2(x):
    hi = x.astype(BF16)
    mid = (x - hi.astype(F32)).astype(BF16)
    return hi, mid


def _level_codes(c, rev):
    t = lax.broadcasted_iota(jnp.int32, (c, c), 0)
    s = lax.broadcasted_iota(jnp.int32, (c, c), 1)
    x = t ^ s
    code = jnp.full((c, c), HG_DIAG, jnp.int32)
    for li in reversed(range(len(HG_LEVELS))):
        code = jnp.where(x >= HG_LEVELS[li], li, code)
    return jnp.where((s >= t) if rev else (s <= t), code, -1)


def _pivot_rows(gc, m, rev):
    c = gc.shape[0]
    blk = 2 * m
    if blk >= 8:
        piv = m if rev else m - 1
        parts = []
        for s in range(0, c, blk):
            parts.append(jnp.broadcast_to(gc[s + piv:s + piv + 1, :], (blk, gc.shape[1])))
        return parts[0] if len(parts) == 1 else jnp.concatenate(parts, axis=0)
    pos = lax.broadcasted_iota(jnp.int32, gc.shape, 0) & (blk - 1)
    piv = m if rev else m - 1
    out = gc
    for p in range(blk):
        if p == piv:
            continue
        shift = (p - piv) % c
        out = jnp.where(pos == p, pltpu.roll(gc, shift, 0), out)
    return out


def _neg_abs(x):
    bits = lax.bitcast_convert_type(x, jnp.uint32) | jnp.uint32(0x80000000)
    return lax.bitcast_convert_type(bits, F32)


def _hgrn_prepare(zs, qs, lbs, dirs, tri_ref, codes_ref):
    c = qs[0].shape[0]
    sub = 8
    units = range(len(zs))
    f = [lb + (1.0 - lb) * _sigmoid(z) for z, lb in zip(zs, lbs)]
    k = [1.0 - x for x in f]
    parts = [_split2(jnp.log2(x)) for x in f]
    gc = [_dot(tri_ref[int(dirs[d])], parts[d][0]) + _dot(tri_ref[int(dirs[d])], parts[d][1]) for d in units]
    pos = lax.broadcasted_iota(jnp.int32, qs[0].shape, 0)

    def tiles(x):
        return [x[i:i + sub] for i in range(0, c, sub)]

    def code_tiles(d):
        return [codes_ref[int(dirs[d]), i:i + sub, :] for i in range(0, c, sub)]

    a = []
    for d in units:
        diag = jnp.sum(qs[d] * k[d], axis=-1, keepdims=True)
        a.append([jnp.where(ct == HG_DIAG, dt, 0.0) for ct, dt in zip(code_tiles(d), tiles(diag))])

    def merge(d, li, prod):
        a[d] = [jnp.where(ct == li, pt, at) for ct, pt, at in zip(code_tiles(d), tiles(prod), a[d])]

    def small_level(m, decay):
        li = HG_LEVELS.index(m)
        for d, rev in enumerate(dirs):
            is_q = ((pos & m) == 0) if rev else ((pos & m) != 0)
            xl = (jnp.where(is_q, qs[d], k[d]) * decay[d]).astype(BF16)
            merge(d, li, _dot_nt(xl, xl))

    if 1 in HG_LEVELS:
        ones = jnp.ones_like(qs[0])
        dec1 = []
        for d, rev in enumerate(dirs):
            is_q = ((pos & 1) == 0) if rev else ((pos & 1) != 0)
            dec1.append(jnp.where(is_q, f[d], ones))
        small_level(1, dec1)
    if 2 in HG_LEVELS:
        p4 = pos & 3
        dec2 = []
        for d, rev in enumerate(dirs):
            nxt = pltpu.roll(f[d], c - 1, 0)
            prv = pltpu.roll(f[d], 1, 0)
            if rev:
                e = jnp.where(p4 == 0, f[d] * nxt, jnp.where(p4 == 1, f[d], jnp.where(p4 == 2, 1.0, prv)))
            else:
                e = jnp.where(p4 == 0, nxt, jnp.where(p4 == 1, 1.0, jnp.where(p4 == 2, f[d], f[d] * prv)))
            dec2.append(e)
        small_level(2, dec2)

    for li, m in enumerate(HG_LEVELS):
        if m <= 2:
            continue
        e = [jnp.exp2(_neg_abs(gc[d] - _pivot_rows(gc[d], m, rev))) for d, rev in enumerate(dirs)]
        if m < sub:
            small_level(m, e)
            continue
        for d, rev in enumerate(dirs):
            q_rows, rhs = [], []
            for s in range(0, c, 2 * m):
                lo, hi = slice(s, s + m), slice(s + m, s + 2 * m)
                q_sl, k_sl = (lo, hi) if rev else (hi, lo)
                q_rows.append(q_sl)
                kx = k[d][k_sl] * e[d][k_sl]
                rhs += [k[d][lo], kx] if rev else [kx, k[d][hi]]
            lhs = jnp.concatenate([qs[d][r] * e[d][r] for r in q_rows], axis=0).astype(BF16)
            prod = tiles(_dot_nt(lhs, jnp.concatenate(rhs, axis=0).astype(BF16)))
            ct = code_tiles(d)
            j = 0
            for r in q_rows:
                for i in range(r.start // sub, r.stop // sub):
                    a[d][i] = jnp.where(ct[i] == li, prod[j], a[d][i])
                    j += 1

    out = []
    for d, rev in enumerate(dirs):
        g_tot = gc[d][0:1, :] if rev else gc[d][c - 1:c, :]
        qg = (qs[d] * jnp.exp2(gc[d])).astype(BF16)
        kd = (k[d] * jnp.exp2(g_tot - gc[d])).astype(BF16)
        out.append((jnp.concatenate(a[d], axis=0).astype(BF16), qg, kd, jnp.exp2(g_tot)))
    return out


HG_DEC_ROWS = 8
HG_PASS_ROWS = 256
HG_HEADS_PER_STEP = 2
HG_LOOP_UNROLL = 3


def _row_block(idx, size):
    start = idx * size
    return pl.ds(start if isinstance(start, int) else pl.multiple_of(start, size), size)


def _hgrn_kernel(q_ref, ff_ref, fb_ref, i_ref, g_ref, lbl_ref, nw_ref, o_ref,
                 tri_ref, codes_ref, a_ref, qg_ref, kd_ref, dec_ref, qa_ref, vb_ref, acc_ref, *, layer):
    t = q_ref.shape[1]
    c = HG_CHUNK
    n = t // c
    dk = dv = HEAD_DIM
    heads = q_ref.shape[2] // HEAD_DIM
    units = [(hh, rev) for hh in range(heads) for rev in (False, True)]
    dirs = [rev for _, rev in units]

    def cols(hh):
        return slice(hh * HEAD_DIM, (hh + 1) * HEAD_DIM)

    lg = lbl_ref[...]
    e = jnp.exp(lg - jnp.max(lg, axis=0, keepdims=True))
    p = e / jnp.sum(e, axis=0, keepdims=True)
    lb = jnp.sum(p[:layer + 1, :], axis=0, keepdims=True) - p[0:1, :]
    lbs = [lb[:, cols(hh)] for hh, _ in units]

    row = lax.broadcasted_iota(jnp.int32, (c, c), 0)
    col = lax.broadcasted_iota(jnp.int32, (c, c), 1)
    tri_ref[0] = jnp.where(col <= row, 1.0, 0.0).astype(BF16)
    tri_ref[1] = jnp.where(col >= row, 1.0, 0.0).astype(BF16)
    codes_ref[0] = _level_codes(c, False)
    codes_ref[1] = _level_codes(c, True)

    act_rows = HG_PASS_ROWS

    def activate(i, carry):
        rows = _row_block(i, act_rows)
        qa_ref[rows, :] = _silu(q_ref[0, rows, :])
        vb_ref[rows, :] = i_ref[0, rows, :].astype(BF16)
        return carry

    lax.fori_loop(0, t // act_rows, activate, 0)

    def chunk_rows(i):
        return [_row_block(i, c), _row_block(n - 1 - i, c)]

    def dec_rows(i, size):
        return [pl.ds(r.start, size) for r in (_row_block(i, HG_DEC_ROWS), _row_block(n - 1 - i, HG_DEC_ROWS))]

    def prepare(i):
        rows = chunk_rows(i)
        zs = [(fb_ref if rev else ff_ref)[0, rows[rev], cols(hh)] for hh, rev in units]
        qs = [qa_ref[rows[rev], cols(hh)] for hh, rev in units]
        staged = _hgrn_prepare(zs, qs, lbs, dirs, tri_ref, codes_ref)
        for u, ((_, rev), (a, qg, kd, dec)) in enumerate(zip(units, staged)):
            a_ref[u, rows[rev], :] = a
            qg_ref[u, rows[rev], :] = qg
            kd_ref[u, rows[rev], :] = kd
            dec_ref[u, dec_rows(i, HG_DEC_ROWS)[rev], :] = jnp.broadcast_to(dec, (HG_DEC_ROWS, dk))

    def recur(i, states):
        rows = chunk_rows(i)
        new = []
        for u, (hh, rev) in enumerate(units):
            r = rows[rev]
            vb = vb_ref[r, cols(hh)]
            acc_ref[u, r, :] = _dot(a_ref[u, r, :], vb) + _dot_nt(qg_ref[u, r, :], states[u].astype(BF16))
            dec = dec_ref[u, dec_rows(i, 1)[rev], :]
            new.append(states[u] * dec + _dot_tn(vb, kd_ref[u, r, :]))
        return tuple(new)

    def body(i, states):
        states = recur(i, states)
        prepare(i + 1)
        return states

    zero = jnp.zeros((dv, dk), F32)
    prepare(0)
    recur(n - 1, lax.fori_loop(0, n - 1, body, (zero,) * len(units), unroll=HG_LOOP_UNROLL))

    rows = HG_PASS_ROWS

    def finish(i, carry):
        r = _row_block(i, rows)
        for hh in range(heads):
            o = acc_ref[2 * hh, r, :] + acc_ref[2 * hh + 1, r, :]
            ms = jnp.mean(o * o, axis=-1, keepdims=True)
            y = o * lax.rsqrt(ms + RMS_EPS) * nw_ref[:, cols(hh)]
            o_ref[0, r, cols(hh)] = (y * _silu(g_ref[0, r, cols(hh)])).astype(o_ref.dtype)
        return carry

    lax.fori_loop(0, t // rows, finish, 0, unroll=4)


def _hgrn(proj, lb_logits, norm_w, layer):
    b, t, _ = proj.shape
    nh = HG_HEADS
    g = HG_HEADS_PER_STEP
    n_layers = lb_logits.shape[0]
    width = g * HEAD_DIM
    blk = (1, t, width)
    base = 3 * HALF // width
    units = 2 * g

    def col(k):
        return lambda i, h: (i, 0, base + k * (nh // g) + h)

    return pl.pallas_call(
        functools.partial(_hgrn_kernel, layer=layer),
        grid=(b, nh // g),
        in_specs=[pl.BlockSpec(blk, col(0)), pl.BlockSpec(blk, col(1)), pl.BlockSpec(blk, col(2)),
                  pl.BlockSpec(blk, col(3)), pl.BlockSpec(blk, col(4)),
                  pl.BlockSpec((n_layers, width), lambda i, h: (0, h)),
                  pl.BlockSpec((1, width), lambda i, h: (0, h))],
        out_specs=pl.BlockSpec(blk, lambda i, h: (i, 0, h)),
        out_shape=jax.ShapeDtypeStruct((b, t, HALF), BF16),
        scratch_shapes=[pltpu.VMEM((2, HG_CHUNK, HG_CHUNK), BF16),
                        pltpu.VMEM((2, HG_CHUNK, HG_CHUNK), jnp.int32),
                        pltpu.VMEM((units, t, HG_CHUNK), BF16),
                        pltpu.VMEM((units, t, HEAD_DIM), BF16),
                        pltpu.VMEM((units, t, HEAD_DIM), BF16),
                        pltpu.VMEM((units, t // HG_CHUNK * HG_DEC_ROWS, HEAD_DIM), F32),
                        pltpu.VMEM((t, width), F32),
                        pltpu.VMEM((t, width), BF16),
                        pltpu.VMEM((units, t, HEAD_DIM), F32)],
        compiler_params=_params(("parallel", "parallel")),
        name="hgrn2",
    )(proj, proj, proj, proj, proj, lb_logits, norm_w.reshape(1, HALF))


POOL_PAD = 16
POOL_ROWS = 256


def _pool_kernel(x_ref, w_ref, s_ref, o_ref, xp_ref, sa_ref, sb_ref):
    t = x_ref.shape[1]
    width = x_ref.shape[2]
    gd = POOL_GROUP_DIM
    half = POOL_PAD // 2
    lo, hi = half, t + POOL_PAD + half
    xp_ref[0:POOL_PAD, :] = jnp.zeros((POOL_PAD, width), F32)
    xp_ref[POOL_PAD + t:, :] = jnp.zeros((POOL_PAD, width), F32)
    xp_ref[POOL_PAD:POOL_PAD + t, :] = x_ref[0]
    for ref in (sa_ref, sb_ref):
        ref[0:lo, :] = jnp.zeros((lo, gd), F32)
        ref[hi:, :] = jnp.zeros((t + 2 * POOL_PAD - hi, gd), F32)

    for gi, win in enumerate(POOL_WINDOWS):
        c0, c1 = gi * gd, (gi + 1) * gd
        sa_ref[lo:hi, :] = xp_ref[lo - 1:hi - 1, c0:c1] + xp_ref[lo:hi, c0:c1]
        src, dst, span = sa_ref, sb_ref, 2
        while span < win:
            sh = span // 2
            dst[lo:hi, :] = src[lo - sh:hi - sh, :] + src[lo + sh:hi + sh, :]
            src, dst, span = dst, src, 2 * span
        hw = win // 2
        for t0 in range(0, t, POOL_ROWS):
            pos = t0 + lax.broadcasted_iota(jnp.int32, (POOL_ROWS, gd), 0)
            cnt = (jnp.minimum(pos + hw, t) - jnp.maximum(pos - hw, 0)).astype(F32)
            tot = src[POOL_PAD + t0:POOL_PAD + t0 + POOL_ROWS, :]
            pooled = tot / cnt - x_ref[0, t0:t0 + POOL_ROWS, c0:c1]
            y = _dot(pooled.astype(BF16), w_ref[gi]) * s_ref[:, c0:c1]
            o_ref[0, t0:t0 + POOL_ROWS, c0:c1] = y.astype(o_ref.dtype)


def _pool(proj, pool_w, scale):
    b, t, _ = proj.shape
    ng = len(POOL_WINDOWS)
    return pl.pallas_call(
        _pool_kernel,
        grid=(b,),
        in_specs=[pl.BlockSpec((1, t, HALF), lambda i: (i, 0, 0)),
                  pl.BlockSpec((ng, POOL_GROUP_DIM, POOL_GROUP_DIM), lambda i: (0, 0, 0)),
                  pl.BlockSpec((1, HALF), lambda i: (0, 0))],
        out_specs=pl.BlockSpec((1, t, HALF), lambda i: (i, 0, 0)),
        out_shape=jax.ShapeDtypeStruct((b, t, HALF), BF16),
        scratch_shapes=[pltpu.VMEM((t + 2 * POOL_PAD, HALF), F32),
                        pltpu.VMEM((t + 2 * POOL_PAD, POOL_GROUP_DIM), F32),
                        pltpu.VMEM((t + 2 * POOL_PAD, POOL_GROUP_DIM), F32)],
        compiler_params=_params(("parallel",)),
        name="pool",
    )(proj, pool_w, scale.reshape(1, HALF))


def _rope_tables(t):
    pos = jnp.arange(t)
    row = (pos // GRID_W).astype(F32)
    col = (pos % GRID_W).astype(F32)
    n_freq = HEAD_DIM // 4
    inv = ROPE_THETA ** (-jnp.arange(n_freq, dtype=F32) / n_freq)
    ang = jnp.concatenate([row[:, None] * inv, col[:, None] * inv], axis=-1)
    cos = jnp.repeat(jnp.cos(ang), 2, axis=-1)
    sin = jnp.repeat(jnp.sin(ang), 2, axis=-1)
    sign = jnp.where(jnp.arange(HEAD_DIM) % 2 == 0, -1.0, 1.0).astype(F32)
    return cos, sin * sign


def _norm_rope(x, g, cos, sin_signed, scale):
    ms = jnp.mean(x * x, axis=-1, keepdims=True)
    xn = x * lax.rsqrt(ms + RMS_EPS) * g
    lane = lax.broadcasted_iota(jnp.int32, x.shape, 1)
    partner = jnp.where((lane & 1) == 0, pltpu.roll(xn, HEAD_DIM - 1, 1), pltpu.roll(xn, 1, 1))
    return (xn * cos + partner * sin_signed) * scale


def _qkv_prep_kernel(q_ref, k_ref, v_ref, qn_ref, kn_ref, cos_ref, sin_ref, qo_ref, kt_ref, vo_ref):
    cos = cos_ref[...]
    sin = sin_ref[...]
    for h in range(GQA_Q_HEADS):
        sl = slice(h * HEAD_DIM, (h + 1) * HEAD_DIM)
        qo_ref[:, sl] = _norm_rope(q_ref[:, sl], qn_ref[...], cos, sin,
                                   ATTN_SCALE_LOG2).astype(qo_ref.dtype)
    for h in range(GQA_KV_HEADS):
        sl = slice(h * HEAD_DIM, (h + 1) * HEAD_DIM)
        kt_ref[0, sl, :] = _norm_rope(k_ref[:, sl], kn_ref[...], cos, sin, 1.0).T.astype(kt_ref.dtype)
    ones = jnp.ones((v_ref.shape[0], HEAD_DIM), vo_ref.dtype)
    for h in range(GQA_KV_HEADS):
        vo_ref[:, 2 * h * HEAD_DIM:(2 * h + 1) * HEAD_DIM] = (
            v_ref[:, h * HEAD_DIM:(h + 1) * HEAD_DIM].astype(vo_ref.dtype))
        vo_ref[:, (2 * h + 1) * HEAD_DIM:(2 * h + 2) * HEAD_DIM] = ones


def _qkv_prep(proj2d, q_norm, k_norm, cos, sin, b, t, tm):
    m = proj2d.shape[0]
    tb = t // tm
    kv_blk = (2 * HALF) // KV_W
    return pl.pallas_call(
        _qkv_prep_kernel,
        grid=(m // tm,),
        in_specs=[pl.BlockSpec((tm, HALF), lambda i: (i, 1)),
                  pl.BlockSpec((tm, KV_W), lambda i: (i, kv_blk)),
                  pl.BlockSpec((tm, KV_W), lambda i: (i, kv_blk + 1)),
                  pl.BlockSpec((1, HEAD_DIM), lambda i: (0, 0)),
                  pl.BlockSpec((1, HEAD_DIM), lambda i: (0, 0)),
                  pl.BlockSpec((tm, HEAD_DIM), lambda i: (i % tb, 0)),
                  pl.BlockSpec((tm, HEAD_DIM), lambda i: (i % tb, 0))],
        out_specs=[pl.BlockSpec((tm, HALF), lambda i: (i, 0)),
                   pl.BlockSpec((1, KV_W, tm), lambda i: (i // tb, 0, i % tb)),
                   pl.BlockSpec((tm, 2 * KV_W), lambda i: (i, 0))],
        out_shape=[jax.ShapeDtypeStruct((m, HALF), BF16),
                   jax.ShapeDtypeStruct((b, KV_W, t), BF16),
                   jax.ShapeDtypeStruct((m, 2 * KV_W), BF16)],
        compiler_params=_params(("parallel",)),
        name="qkv_prep",
    )(proj2d, proj2d, proj2d, q_norm.reshape(1, HEAD_DIM), k_norm.reshape(1, HEAD_DIM), cos, sin)


GQA_GROUP = GQA_Q_HEADS // GQA_KV_HEADS


def _gqa_kernel(q_ref, kt_ref, v_ref, o_ref):
    def scores(h):
        return _dot(q_ref[0, :, h * HEAD_DIM:(h + 1) * HEAD_DIM], kt_ref[0])

    s_next = scores(0)
    for h in range(GQA_GROUP):
        s = s_next
        if h + 1 < GQA_GROUP:
            s_next = scores(h + 1)
        m = jnp.max(s, axis=-1, keepdims=True)
        p = jnp.exp2(s - m).astype(BF16)
        ol = _dot(p, v_ref[0])
        o = ol[:, :HEAD_DIM] / ol[:, HEAD_DIM:]
        o_ref[0, :, h * HEAD_DIM:(h + 1) * HEAD_DIM] = o.astype(o_ref.dtype)


def _gqa(q, kt, v, tq):
    b, t, _ = q.shape
    gw = GQA_GROUP * HEAD_DIM
    return pl.pallas_call(
        _gqa_kernel,
        grid=(b, GQA_KV_HEADS, t // tq),
        in_specs=[pl.BlockSpec((1, tq, gw), lambda i, h, j: (i, j, h)),
                  pl.BlockSpec((1, HEAD_DIM, t), lambda i, h, j: (i, h, 0)),
                  pl.BlockSpec((1, t, 2 * HEAD_DIM), lambda i, h, j: (i, 0, h))],
        out_specs=pl.BlockSpec((1, tq, gw), lambda i, h, j: (i, j, h)),
        out_shape=jax.ShapeDtypeStruct((b, t, HALF), BF16),
        compiler_params=_params(("parallel", "parallel", "parallel")),
        name="gqa",
    )(q, kt, v)


def kernel(x, ab_w_in, ab_w_out, na_rpb, hg_lb_logits, hg_norm_w, cd_w_in, cd_w_out, pool_w,
           pool_scale, d_q_norm, d_k_norm, ln_mix_g, ln_mix_b, ffn_w_gate, ffn_w_up, ffn_w_down,
           ln_ffn_g, ln_ffn_b):
    b, t, d = x.shape
    m = b * t
    cos, sin = _rope_tables(t)
    xf = x.reshape(m, d)
    xb = xf.astype(BF16)
    ab_w_out_b = ab_w_out.astype(BF16)
    cd_w_out_b = cd_w_out.astype(BF16)
    pool_w_b = pool_w.astype(BF16)
    ffn_w_down_b = ffn_w_down.astype(BF16)
    for layer in range(DEPTH):
        j = layer // 2
        if layer % 2 == 0:
            proj = _matmul(xb, ab_w_in, j, F32, *AB_IN_TILE, "ab_in_proj").reshape(b, t, -1)
            y1 = _na_attention(proj, na_rpb[j])
            y2 = _hgrn(proj, hg_lb_logits, hg_norm_w[j], j)
            w_out = ab_w_out_b
        else:
            proj = _matmul(xb, cd_w_in, j, F32, *CD_IN_TILE, "cd_in_proj")
            y1 = _pool(proj.reshape(b, t, -1), pool_w_b[j], pool_scale[j])
            qn, kt, vn = _qkv_prep(proj, d_q_norm[j], d_k_norm[j], cos, sin, b, t, QKV_PREP_ROWS)
            y2 = _gqa(qn.reshape(b, t, HALF), kt, vn.reshape(b, t, 2 * KV_W), GQA_QUERY_ROWS)
            w_out = cd_w_out_b
        xf, xb = _mix_out(y1.reshape(m, HALF), y2.reshape(m, HALF), w_out, j, xf,
                          ln_mix_g[layer], ln_mix_b[layer], MIX_OUT_ROWS)
        h = _ffn_up(xb, ffn_w_gate, ffn_w_up, layer, *FFN_UP_TILE)
        xf, xb = _ffn_down(h, ffn_w_down_b, layer, xf, ln_ffn_g[layer], ln_ffn_b[layer],
                           FFN_DOWN_ROWS)
    return xf.reshape(b, t, d)
```

```python
import functools

import jax
import jax.numpy as jnp
from jax import lax
from jax.experimental import pallas as pl
from jax.experimental.pallas import tpu as pltpu

D_MODEL = 2048
DEPTH = 4
GRID_W = 64
HALF = 1024
HEAD_DIM = 128
NA_HEADS = 8
NA_WIN_ROWS = 8
NA_WIN_COLS = 16
HG_HEADS = 8
HG_CHUNK = 128
POOL_WINDOWS = (2, 4, 8, 16)
POOL_GROUP_DIM = 256
GQA_Q_HEADS = 8
GQA_KV_HEADS = 2
KV_W = GQA_KV_HEADS * HEAD_DIM
ROPE_THETA = 10000.0
FFN_HIDDEN = 5632
DN_ALPHA = (2 * DEPTH) ** 0.25
LN_EPS = 1e-5
RMS_EPS = 1e-6
NEG_BIG = -1e30
LOG2E = 1.4426950408889634
ATTN_SCALE_LOG2 = HEAD_DIM ** -0.5 * LOG2E

V7X_VMEM_LIMIT_BYTES = 56 * 1024 * 1024

AB_IN_TILE = (1024, 1024)
CD_IN_TILE = (1024, 1280)
MIX_OUT_ROWS = 512
MIX_OUT_PARTS = 4
FFN_UP_TILE = (1024, 512)
FFN_DOWN_ROWS = 256
QKV_PREP_ROWS = 512
GQA_QUERY_ROWS = 1024
NA_BATCH_PER_STEP = 1

F32 = jnp.float32
BF16 = jnp.bfloat16


def _params(semantics):
    return pltpu.CompilerParams(dimension_semantics=semantics,
                                vmem_limit_bytes=V7X_VMEM_LIMIT_BYTES)


def _sigmoid(x):
    return 1.0 / (1.0 + jnp.exp(-x))


def _silu(x):
    return x * _sigmoid(x)


def _dot(a, b):
    return jnp.dot(a, b, preferred_element_type=F32)


def _dot_nt(a, b):
    return lax.dot_general(a, b, (((1,), (1,)), ((), ())), preferred_element_type=F32)


def _dot_tn(a, b):
    return lax.dot_general(a, b, (((0,), (0,)), ((), ())), preferred_element_type=F32)


def _matmul_kernel(a_ref, w_ref, o_ref, wb_ref):
    @pl.when(pl.program_id(1) == 0)
    def _():
        wb_ref[...] = w_ref[...].astype(BF16)

    o_ref[...] = _dot(a_ref[...], wb_ref[...]).astype(o_ref.dtype)


def _matmul(a, w, layer, out_dtype, tm, tn, name):
    m, k = a.shape
    n = w.shape[2]
    return pl.pallas_call(
        _matmul_kernel,
        grid=(n // tn, m // tm),
        in_specs=[pl.BlockSpec((tm, k), lambda j, i: (i, 0)),
                  pl.BlockSpec((None, k, tn), lambda j, i: (layer, 0, j))],
        out_specs=pl.BlockSpec((tm, tn), lambda j, i: (i, j)),
        out_shape=jax.ShapeDtypeStruct((m, n), out_dtype),
        scratch_shapes=[pltpu.VMEM((k, tn), BF16)],
        compiler_params=_params(("arbitrary", "arbitrary")),
        name=name,
    )(a, w)


def _res_ln(acc, x, g, b):
    z = DN_ALPHA * x + acc
    mu = jnp.mean(z, axis=-1, keepdims=True)
    zc = z - mu
    var = jnp.mean(zc * zc, axis=-1, keepdims=True)
    return zc * lax.rsqrt(var + LN_EPS) * g + b


def _row_parts(tm, parts):
    size = tm // parts
    return [slice(p * size, (p + 1) * size) for p in range(parts)]


def _mix_out_kernel(y1_ref, y2_ref, w_ref, x_ref, g_ref, b_ref, of_ref, ob_ref, *, parts):
    half = y1_ref.shape[1]
    rows = _row_parts(y1_ref.shape[0], parts)
    accs = [_dot(y1_ref[r, :], w_ref[:half, :]) + _dot(y2_ref[r, :], w_ref[half:, :]) for r in rows]
    for r, acc in zip(rows, accs):
        out = _res_ln(acc, x_ref[r, :], g_ref[...], b_ref[...])
        of_ref[r, :] = out
        ob_ref[r, :] = out.astype(BF16)


def _mix_out(y1, y2, w, layer, x, g, b, tm):
    m, half = y1.shape
    d = w.shape[2]
    row = lambda i: (i, 0)
    fixed = lambda i: (0, 0)
    return pl.pallas_call(
        functools.partial(_mix_out_kernel, parts=MIX_OUT_PARTS),
        grid=(m // tm,),
        in_specs=[pl.BlockSpec((tm, half), row), pl.BlockSpec((tm, half), row),
                  pl.BlockSpec((None, 2 * half, d), lambda i: (layer, 0, 0),
                               pipeline_mode=pl.Buffered(1)),
                  pl.BlockSpec((tm, d), row),
                  pl.BlockSpec((1, d), fixed), pl.BlockSpec((1, d), fixed)],
        out_specs=[pl.BlockSpec((tm, d), row), pl.BlockSpec((tm, d), row)],
        out_shape=[jax.ShapeDtypeStruct((m, d), F32), jax.ShapeDtypeStruct((m, d), BF16)],
        compiler_params=_params(("parallel",)),
        name="mix_out_ln",
    )(y1, y2, w, x, g.reshape(1, d), b.reshape(1, d))


def _ffn_up_kernel(x_ref, wg_ref, wu_ref, h_ref, wgb_ref, wub_ref):
    @pl.when(pl.program_id(1) == 0)
    def _():
        wgb_ref[...] = wg_ref[...].astype(BF16)
        wub_ref[...] = wu_ref[...].astype(BF16)

    x = x_ref[...]
    gate = _dot(x, wgb_ref[...])
    up = _dot(x, wub_ref[...])
    h_ref[...] = (_silu(gate) * up).astype(h_ref.dtype)


def _ffn_up(xb, wg, wu, layer, tm, tn):
    m, k = xb.shape
    n = wg.shape[2]
    return pl.pallas_call(
        _ffn_up_kernel,
        grid=(n // tn, m // tm),
        in_specs=[pl.BlockSpec((tm, k), lambda j, i: (i, 0)),
                  pl.BlockSpec((None, k, tn), lambda j, i: (layer, 0, j)),
                  pl.BlockSpec((None, k, tn), lambda j, i: (layer, 0, j))],
        out_specs=pl.BlockSpec((tm, tn), lambda j, i: (i, j)),
        out_shape=jax.ShapeDtypeStruct((m, n), BF16),
        scratch_shapes=[pltpu.VMEM((k, tn), BF16), pltpu.VMEM((k, tn), BF16)],
        compiler_params=_params(("arbitrary", "arbitrary")),
        name="ffn_up",
    )(xb, wg, wu)


def _ffn_down_kernel(h_ref, w_ref, x_ref, g_ref, b_ref, of_ref, ob_ref):
    out = _res_ln(_dot(h_ref[...], w_ref[...]), x_ref[...], g_ref[...], b_ref[...])
    of_ref[...] = out
    ob_ref[...] = out.astype(BF16)


def _ffn_down(h, w, layer, x, g, b, tm):
    m, kk = h.shape
    d = w.shape[2]
    row = lambda i: (i, 0)
    fixed = lambda i: (0, 0)
    return pl.pallas_call(
        _ffn_down_kernel,
        grid=(m // tm,),
        in_specs=[pl.BlockSpec((tm, kk), row),
                  pl.BlockSpec((None, kk, d), lambda i: (layer, 0, 0), pipeline_mode=pl.Buffered(1)),
                  pl.BlockSpec((tm, d), row),
                  pl.BlockSpec((1, d), fixed), pl.BlockSpec((1, d), fixed)],
        out_specs=[pl.BlockSpec((tm, d), row), pl.BlockSpec((tm, d), row)],
        out_shape=[jax.ShapeDtypeStruct((m, d), F32), jax.ShapeDtypeStruct((m, d), BF16)],
        compiler_params=_params(("parallel",)),
        name="ffn_down_ln",
    )(h, w, x, g.reshape(1, d), b.reshape(1, d))


NA_GROUP = 4
NA_WINDOW = NA_GROUP + NA_WIN_ROWS
RPB_ROWS = 2 * NA_WIN_ROWS - 1
RPB_COLS = 2 * NA_WIN_COLS - 1


def _na_window_start(g, rows):
    return min(max(g * NA_GROUP - NA_WIN_ROWS // 2, 0), rows - NA_WINDOW)


def _na_patterns(rows):
    pats, ids = [], []
    for r in range(rows):
        w0 = _na_window_start(r // NA_GROUP, rows)
        r0 = min(max(r - NA_WIN_ROWS // 2, 0), rows - NA_WIN_ROWS)
        key = (r0 - w0, w0 - r + NA_WIN_ROWS - 1)
        if key not in pats:
            pats.append(key)
        ids.append(pats.index(key))
    return tuple(pats), tuple(ids)


def _na_band_kernel(rpb_ref, o_ref, *, patterns):
    h = pl.program_id(0)
    base = h * (RPB_ROWS * RPB_COLS)
    shape = (GRID_W, 2 * GRID_W)
    q = lax.broadcasted_iota(jnp.int32, shape, 0)
    lane = lax.broadcasted_iota(jnp.int32, shape, 1)
    kc = lane & (GRID_W - 1)
    second = lane >> 6
    code = kc - q + (NA_WIN_COLS - 1) + RPB_COLS * second
    start = jnp.clip(q - NA_WIN_COLS // 2, 0, GRID_W - NA_WIN_COLS)
    valid = (kc >= start) & (kc < start + NA_WIN_COLS)
    neg = jnp.full(shape, NEG_BIG, F32)
    pairs = {}

    def pair(ri):
        if ri not in pairs:
            acc = neg
            for i in range(2 * RPB_COLS):
                if 0 <= ri + i // RPB_COLS < RPB_ROWS:
                    acc = jnp.where(code == i, rpb_ref[base + ri * RPB_COLS + i], acc)
            pairs[ri] = jnp.where(valid, acc * LOG2E, NEG_BIG)
        return pairs[ri]

    for p, (lo, c) in enumerate(patterns):
        for u in range(NA_WINDOW // 2):
            in0 = lo <= 2 * u < lo + NA_WIN_ROWS
            in1 = lo <= 2 * u + 1 < lo + NA_WIN_ROWS
            if in0 and in1:
                tile = pair(2 * u + c)
            elif in0:
                tile = jnp.where(second == 0, pair(2 * u + c), NEG_BIG)
            elif in1:
                tile = jnp.where(second == 1, pair(2 * u + c), NEG_BIG)
            else:
                tile = neg
            o_ref[0, p, :, u * 2 * GRID_W:(u + 1) * 2 * GRID_W] = tile


def _na_band(rpb, patterns):
    shape = (NA_HEADS, len(patterns), GRID_W, NA_WINDOW * GRID_W)
    return pl.pallas_call(
        functools.partial(_na_band_kernel, patterns=patterns),
        grid=(NA_HEADS,),
        in_specs=[pl.BlockSpec(memory_space=pltpu.SMEM)],
        out_specs=pl.BlockSpec((1,) + shape[1:], lambda h: (h, 0, 0, 0)),
        out_shape=jax.ShapeDtypeStruct(shape, F32),
        compiler_params=_params(("arbitrary",)),
        name="na_band",
    )(rpb.reshape(-1))


def _na_kernel(q_ref, k_ref, v_ref, band_ref, o_ref, kt_ref, vb_ref, *, pattern_ids):
    t = q_ref.shape[1]
    rows = t // GRID_W
    gq = NA_GROUP * GRID_W
    wlen = NA_WINDOW * GRID_W
    n_groups = rows // NA_GROUP
    scale = ATTN_SCALE_LOG2
    for bi in range(q_ref.shape[0]):
        for c in range(0, t, gq):
            kt_ref[:, c:c + gq] = k_ref[bi, c:c + gq, :].T.astype(BF16)
        vb_ref[:, :HEAD_DIM] = v_ref[bi].astype(BF16)
        vb_ref[:, HEAD_DIM:] = jnp.ones((t, HEAD_DIM), BF16)

        def scores(g):
            w0 = _na_window_start(g, rows) * GRID_W
            q = (q_ref[bi, g * gq:(g + 1) * gq, :] * scale).astype(BF16)
            bias = jnp.concatenate(
                [band_ref[0, pattern_ids[g * NA_GROUP + i]] for i in range(NA_GROUP)], axis=0)
            return _dot(q, kt_ref[:, w0:w0 + wlen]) + bias

        s_next = scores(0)
        for g in range(n_groups):
            s = s_next
            if g + 1 < n_groups:
                s_next = scores(g + 1)
            w0 = _na_window_start(g, rows) * GRID_W
            m = jnp.max(s, axis=-1, keepdims=True)
            p = jnp.exp2(s - m).astype(BF16)
            ol = _dot(p, vb_ref[w0:w0 + wlen, :])
            o = ol[:, :HEAD_DIM] / ol[:, HEAD_DIM:]
            o_ref[bi, g * gq:(g + 1) * gq, :] = o.astype(o_ref.dtype)


def _na_attention(proj, rpb):
    b, t, _ = proj.shape
    nh = NA_HEADS
    nb = NA_BATCH_PER_STEP
    patterns, pattern_ids = _na_patterns(t // GRID_W)
    band = _na_band(rpb, patterns)
    blk = (nb, t, HEAD_DIM)
    return pl.pallas_call(
        functools.partial(_na_kernel, pattern_ids=pattern_ids),
        grid=(nh, b // nb),
        in_specs=[pl.BlockSpec(blk, lambda h, i: (i, 0, h)),
                  pl.BlockSpec(blk, lambda h, i: (i, 0, nh + h)),
                  pl.BlockSpec(blk, lambda h, i: (i, 0, 2 * nh + h)),
                  pl.BlockSpec((1,) + band.shape[1:], lambda h, i: (h, 0, 0, 0))],
        out_specs=pl.BlockSpec(blk, lambda h, i: (i, 0, h)),
        out_shape=jax.ShapeDtypeStruct((b, t, HALF), BF16),
        scratch_shapes=[pltpu.VMEM((HEAD_DIM, t), BF16), pltpu.VMEM((t, 2 * HEAD_DIM), BF16)],
        compiler_params=_params(("parallel", "parallel")),
        name="na_attention",
    )(proj, proj, proj, band)


HG_LEVELS = tuple(HG_CHUNK >> (i + 1) for i in range(HG_CHUNK.bit_length() - 1))
HG_DIAG = len(HG_LEVELS)


def _split2(x):
    hi = x.astype(BF16)
    mid = (x - hi.astype(F32)).astype(BF16)
    return hi, mid


def _level_codes(c, rev):
    t = lax.broadcasted_iota(jnp.int32, (c, c), 0)
    s = lax.broadcasted_iota(jnp.int32, (c, c), 1)
    x = t ^ s
    code = jnp.full((c, c), HG_DIAG, jnp.int32)
    for li in reversed(range(len(HG_LEVELS))):
        code = jnp.where(x >= HG_LEVELS[li], li, code)
    return jnp.where((s >= t) if rev else (s <= t), code, -1)


def _pivot_rows(gc, m, rev):
    c = gc.shape[0]
    blk = 2 * m
    if blk >= 8:
        piv = m if rev else m - 1
        parts = []
        for s in range(0, c, blk):
            parts.append(jnp.broadcast_to(gc[s + piv:s + piv + 1, :], (blk, gc.shape[1])))
        return parts[0] if len(parts) == 1 else jnp.concatenate(parts, axis=0)
    pos = lax.broadcasted_iota(jnp.int32, gc.shape, 0) & (blk - 1)
    piv = m if rev else m - 1
    out = gc
    for p in range(blk):
        if p == piv:
            continue
        shift = (p - piv) % c
        out = jnp.where(pos == p, pltpu.roll(gc, shift, 0), out)
    return out


def _neg_abs(x):
    bits = lax.bitcast_convert_type(x, jnp.uint32) | jnp.uint32(0x80000000)
    return lax.bitcast_convert_type(bits, F32)


def _hgrn_prepare(zs, qs, lbs, dirs, tri_ref, codes_ref):
    c = qs[0].shape[0]
    sub = 8
    units = range(len(zs))
    f = [lb + (1.0 - lb) * _sigmoid(z) for z, lb in zip(zs, lbs)]
    k = [1.0 - x for x in f]
    parts = [_split2(jnp.log2(x)) for x in f]
    gc = [_dot(tri_ref[int(dirs[d])], parts[d][0]) + _dot(tri_ref[int(dirs[d])], parts[d][1]) for d in units]
    pos = lax.broadcasted_iota(jnp.int32, qs[0].shape, 0)

    def tiles(x):
        return [x[i:i + sub] for i in range(0, c, sub)]

    def code_tiles(d):
        return [codes_ref[int(dirs[d]), i:i + sub, :] for i in range(0, c, sub)]

    a = []
    for d in units:
        diag = jnp.sum(qs[d] * k[d], axis=-1, keepdims=True)
        a.append([jnp.where(ct == HG_DIAG, dt, 0.0) for ct, dt in zip(code_tiles(d), tiles(diag))])

    def merge(d, li, prod):
        a[d] = [jnp.where(ct == li, pt, at) for ct, pt, at in zip(code_tiles(d), tiles(prod), a[d])]

    def small_level(m, decay):
        li = HG_LEVELS.index(m)
        for d, rev in enumerate(dirs):
            is_q = ((pos & m) == 0) if rev else ((pos & m) != 0)
            xl = (jnp.where(is_q, qs[d], k[d]) * decay[d]).astype(BF16)
            merge(d, li, _dot_nt(xl, xl))

    if 1 in HG_LEVELS:
        ones = jnp.ones_like(qs[0])
        dec1 = []
        for d, rev in enumerate(dirs):
            is_q = ((pos & 1) == 0) if rev else ((pos & 1) != 0)
            dec1.append(jnp.where(is_q, f[d], ones))
        small_level(1, dec1)
    if 2 in HG_LEVELS:
        p4 = pos & 3
        dec2 = []
        for d, rev in enumerate(dirs):
            nxt = pltpu.roll(f[d], c - 1, 0)
            prv = pltpu.roll(f[d], 1, 0)
            if rev:
                e = jnp.where(p4 == 0, f[d] * nxt, jnp.where(p4 == 1, f[d], jnp.where(p4 == 2, 1.0, prv)))
            else:
                e = jnp.where(p4 == 0, nxt, jnp.where(p4 == 1, 1.0, jnp.where(p4 == 2, f[d], f[d] * prv)))
            dec2.append(e)
        small_level(2, dec2)

    for li, m in enumerate(HG_LEVELS):
        if m <= 2:
            continue
        e = [jnp.exp2(_neg_abs(gc[d] - _pivot_rows(gc[d], m, rev))) for d, rev in enumerate(dirs)]
        if m < sub:
            small_level(m, e)
            continue
        for d, rev in enumerate(dirs):
            q_rows, rhs = [], []
            for s in range(0, c, 2 * m):
                lo, hi = slice(s, s + m), slice(s + m, s + 2 * m)
                q_sl, k_sl = (lo, hi) if rev else (hi, lo)
                q_rows.append(q_sl)
                kx = k[d][k_sl] * e[d][k_sl]
                rhs += [k[d][lo], kx] if rev else [kx, k[d][hi]]
            lhs = jnp.concatenate([qs[d][r] * e[d][r] for r in q_rows], axis=0).astype(BF16)
            prod = tiles(_dot_nt(lhs, jnp.concatenate(rhs, axis=0).astype(BF16)))
            ct = code_tiles(d)
            j = 0
            for r in q_rows:
                for i in range(r.start // sub, r.stop // sub):
                    a[d][i] = jnp.where(ct[i] == li, prod[j], a[d][i])
                    j += 1

    out = []
    for d, rev in enumerate(dirs):
        g_tot = gc[d][0:1, :] if rev else gc[d][c - 1:c, :]
        qg = (qs[d] * jnp.exp2(gc[d])).astype(BF16)
        kd = (k[d] * jnp.exp2(g_tot - gc[d])).astype(BF16)
        out.append((jnp.concatenate(a[d], axis=0).astype(BF16), qg, kd, jnp.exp2(g_tot)))
    return out


HG_DEC_ROWS = 8
HG_PASS_ROWS = 256
HG_HEADS_PER_STEP = 2
HG_LOOP_UNROLL = 3


def _row_block(idx, size):
    start = idx * size
    return pl.ds(start if isinstance(start, int) else pl.multiple_of(start, size), size)


def _hgrn_kernel(q_ref, ff_ref, fb_ref, i_ref, g_ref, lbl_ref, nw_ref, o_ref,
                 tri_ref, codes_ref, a_ref, qg_ref, kd_ref, dec_ref, qa_ref, vb_ref, acc_ref, *, layer):
    t = q_ref.shape[1]
    c = HG_CHUNK
    n = t // c
    dk = dv = HEAD_DIM
    heads = q_ref.shape[2] // HEAD_DIM
    units = [(hh, rev) for hh in range(heads) for rev in (False, True)]
    dirs = [rev for _, rev in units]

    def cols(hh):
        return slice(hh * HEAD_DIM, (hh + 1) * HEAD_DIM)

    lg = lbl_ref[...]
    e = jnp.exp(lg - jnp.max(lg, axis=0, keepdims=True))
    p = e / jnp.sum(e, axis=0, keepdims=True)
    lb = jnp.sum(p[:layer + 1, :], axis=0, keepdims=True) - p[0:1, :]
    lbs = [lb[:, cols(hh)] for hh, _ in units]

    row = lax.broadcasted_iota(jnp.int32, (c, c), 0)
    col = lax.broadcasted_iota(jnp.int32, (c, c), 1)
    tri_ref[0] = jnp.where(col <= row, 1.0, 0.0).astype(BF16)
    tri_ref[1] = jnp.where(col >= row, 1.0, 0.0).astype(BF16)
    codes_ref[0] = _level_codes(c, False)
    codes_ref[1] = _level_codes(c, True)

    act_rows = HG_PASS_ROWS

    def activate(i, carry):
        rows = _row_block(i, act_rows)
        qa_ref[rows, :] = _silu(q_ref[0, rows, :])
        vb_ref[rows, :] = i_ref[0, rows, :].astype(BF16)
        return carry

    lax.fori_loop(0, t // act_rows, activate, 0)

    def chunk_rows(i):
        return [_row_block(i, c), _row_block(n - 1 - i, c)]

    def dec_rows(i, size):
        return [pl.ds(r.start, size) for r in (_row_block(i, HG_DEC_ROWS), _row_block(n - 1 - i, HG_DEC_ROWS))]

    def prepare(i):
        rows = chunk_rows(i)
        zs = [(fb_ref if rev else ff_ref)[0, rows[rev], cols(hh)] for hh, rev in units]
        qs = [qa_ref[rows[rev], cols(hh)] for hh, rev in units]
        staged = _hgrn_prepare(zs, qs, lbs, dirs, tri_ref, codes_ref)
        for u, ((_, rev), (a, qg, kd, dec)) in enumerate(zip(units, staged)):
            a_ref[u, rows[rev], :] = a
            qg_ref[u, rows[rev], :] = qg
            kd_ref[u, rows[rev], :] = kd
            dec_ref[u, dec_rows(i, HG_DEC_ROWS)[rev], :] = jnp.broadcast_to(dec, (HG_DEC_ROWS, dk))

    def recur(i, states):
        rows = chunk_rows(i)
        new = []
        for u, (hh, rev) in enumerate(units):
            r = rows[rev]
            vb = vb_ref[r, cols(hh)]
            acc_ref[u, r, :] = _dot(a_ref[u, r, :], vb) + _dot_nt(qg_ref[u, r, :], states[u].astype(BF16))
            dec = dec_ref[u, dec_rows(i, 1)[rev], :]
            new.append(states[u] * dec + _dot_tn(vb, kd_ref[u, r, :]))
        return tuple(new)

    def body(i, states):
        states = recur(i, states)
        prepare(i + 1)
        return states

    zero = jnp.zeros((dv, dk), F32)
    prepare(0)
    recur(n - 1, lax.fori_loop(0, n - 1, body, (zero,) * len(units), unroll=HG_LOOP_UNROLL))

    rows = HG_PASS_ROWS

    def finish(i, carry):
        r = _row_block(i, rows)
        for hh in range(heads):
            o = acc_ref[2 * hh, r, :] + acc_ref[2 * hh + 1, r, :]
            ms = jnp.mean(o * o, axis=-1, keepdims=True)
            y = o * lax.rsqrt(ms + RMS_EPS) * nw_ref[:, cols(hh)]
            o_ref[0, r, cols(hh)] = (y * _silu(g_ref[0, r, cols(hh)])).astype(o_ref.dtype)
        return carry

    lax.fori_loop(0, t // rows, finish, 0, unroll=4)


def _hgrn(proj, lb_logits, norm_w, layer):
    b, t, _ = proj.shape
    nh = HG_HEADS
    g = HG_HEADS_PER_STEP
    n_layers = lb_logits.shape[0]
    width = g * HEAD_DIM
    blk = (1, t, width)
    base = 3 * HALF // width
    units = 2 * g

    def col(k):
        return lambda i, h: (i, 0, base + k * (nh // g) + h)

    return pl.pallas_call(
        functools.partial(_hgrn_kernel, layer=layer),
        grid=(b, nh // g),
        in_specs=[pl.BlockSpec(blk, col(0)), pl.BlockSpec(blk, col(1)), pl.BlockSpec(blk, col(2)),
                  pl.BlockSpec(blk, col(3)), pl.BlockSpec(blk, col(4)),
                  pl.BlockSpec((n_layers, width), lambda i, h: (0, h)),
                  pl.BlockSpec((1, width), lambda i, h: (0, h))],
        out_specs=pl.BlockSpec(blk, lambda i, h: (i, 0, h)),
        out_shape=jax.ShapeDtypeStruct((b, t, HALF), BF16),
        scratch_shapes=[pltpu.VMEM((2, HG_CHUNK, HG_CHUNK), BF16),
                        pltpu.VMEM((2, HG_CHUNK, HG_CHUNK), jnp.int32),
                        pltpu.VMEM((units, t, HG_CHUNK), BF16),
                        pltpu.VMEM((units, t, HEAD_DIM), BF16),
                        pltpu.VMEM((units, t, HEAD_DIM), BF16),
                        pltpu.VMEM((units, t // HG_CHUNK * HG_DEC_ROWS, HEAD_DIM), F32),
                        pltpu.VMEM((t, width), F32),
                        pltpu.VMEM((t, width), BF16),
                        pltpu.VMEM((units, t, HEAD_DIM), F32)],
        compiler_params=_params(("parallel", "parallel")),
        name="hgrn2",
    )(proj, proj, proj, proj, proj, lb_logits, norm_w.reshape(1, HALF))


POOL_PAD = 16
POOL_ROWS = 256


def _pool_kernel(x_ref, w_ref, s_ref, o_ref, xp_ref, sa_ref, sb_ref):
    t = x_ref.shape[1]
    width = x_ref.shape[2]
    gd = POOL_GROUP_DIM
    half = POOL_PAD // 2
    lo, hi = half, t + POOL_PAD + half
    xp_ref[0:POOL_PAD, :] = jnp.zeros((POOL_PAD, width), F32)
    xp_ref[POOL_PAD + t:, :] = jnp.zeros((POOL_PAD, width), F32)
    xp_ref[POOL_PAD:POOL_PAD + t, :] = x_ref[0]
    for ref in (sa_ref, sb_ref):
        ref[0:lo, :] = jnp.zeros((lo, gd), F32)
        ref[hi:, :] = jnp.zeros((t + 2 * POOL_PAD - hi, gd), F32)

    for gi, win in enumerate(POOL_WINDOWS):
        c0, c1 = gi * gd, (gi + 1) * gd
        sa_ref[lo:hi, :] = xp_ref[lo - 1:hi - 1, c0:c1] + xp_ref[lo:hi, c0:c1]
        src, dst, span = sa_ref, sb_ref, 2
        while span < win:
            sh = span // 2
            dst[lo:hi, :] = src[lo - sh:hi - sh, :] + src[lo + sh:hi + sh, :]
            src, dst, span = dst, src, 2 * span
        hw = win // 2
        for t0 in range(0, t, POOL_ROWS):
            pos = t0 + lax.broadcasted_iota(jnp.int32, (POOL_ROWS, gd), 0)
            cnt = (jnp.minimum(pos + hw, t) - jnp.maximum(pos - hw, 0)).astype(F32)
            tot = src[POOL_PAD + t0:POOL_PAD + t0 + POOL_ROWS, :]
            pooled = tot / cnt - x_ref[0, t0:t0 + POOL_ROWS, c0:c1]
            y = _dot(pooled.astype(BF16), w_ref[gi]) * s_ref[:, c0:c1]
            o_ref[0, t0:t0 + POOL_ROWS, c0:c1] = y.astype(o_ref.dtype)


def _pool(proj, pool_w, scale):
    b, t, _ = proj.shape
    ng = len(POOL_WINDOWS)
    return pl.pallas_call(
        _pool_kernel,
        grid=(b,),
        in_specs=[pl.BlockSpec((1, t, HALF), lambda i: (i, 0, 0)),
                  pl.BlockSpec((ng, POOL_GROUP_DIM, POOL_GROUP_DIM), lambda i: (0, 0, 0)),
                  pl.BlockSpec((1, HALF), lambda i: (0, 0))],
        out_specs=pl.BlockSpec((1, t, HALF), lambda i: (i, 0, 0)),
        out_shape=jax.ShapeDtypeStruct((b, t, HALF), BF16),
        scratch_shapes=[pltpu.VMEM((t + 2 * POOL_PAD, HALF), F32),
                        pltpu.VMEM((t + 2 * POOL_PAD, POOL_GROUP_DIM), F32),
                        pltpu.VMEM((t + 2 * POOL_PAD, POOL_GROUP_DIM), F32)],
        compiler_params=_params(("parallel",)),
        name="pool",
    )(proj, pool_w, scale.reshape(1, HALF))


def _rope_tables(t):
    pos = jnp.arange(t)
    row = (pos // GRID_W).astype(F32)
    col = (pos % GRID_W).astype(F32)
    n_freq = HEAD_DIM // 4
    inv = ROPE_THETA ** (-jnp.arange(n_freq, dtype=F32) / n_freq)
    ang = jnp.concatenate([row[:, None] * inv, col[:, None] * inv], axis=-1)
    cos = jnp.repeat(jnp.cos(ang), 2, axis=-1)
    sin = jnp.repeat(jnp.sin(ang), 2, axis=-1)
    sign = jnp.where(jnp.arange(HEAD_DIM) % 2 == 0, -1.0, 1.0).astype(F32)
    return cos, sin * sign


def _norm_rope(x, g, cos, sin_signed, scale):
    ms = jnp.mean(x * x, axis=-1, keepdims=True)
    xn = x * lax.rsqrt(ms + RMS_EPS) * g
    lane = lax.broadcasted_iota(jnp.int32, x.shape, 1)
    partner = jnp.where((lane & 1) == 0, pltpu.roll(xn, HEAD_DIM - 1, 1), pltpu.roll(xn, 1, 1))
    return (xn * cos + partner * sin_signed) * scale


def _qkv_prep_kernel(q_ref, k_ref, v_ref, qn_ref, kn_ref, cos_ref, sin_ref, qo_ref, kt_ref, vo_ref):
    cos = cos_ref[...]
    sin = sin_ref[...]
    for h in range(GQA_Q_HEADS):
        sl = slice(h * HEAD_DIM, (h + 1) * HEAD_DIM)
        qo_ref[:, sl] = _norm_rope(q_ref[:, sl], qn_ref[...], cos, sin,
                                   ATTN_SCALE_LOG2).astype(qo_ref.dtype)
    for h in range(GQA_KV_HEADS):
        sl = slice(h * HEAD_DIM, (h + 1) * HEAD_DIM)
        kt_ref[0, sl, :] = _norm_rope(k_ref[:, sl], kn_ref[...], cos, sin, 1.0).T.astype(kt_ref.dtype)
    ones = jnp.ones((v_ref.shape[0], HEAD_DIM), vo_ref.dtype)
    for h in range(GQA_KV_HEADS):
        vo_ref[:, 2 * h * HEAD_DIM:(2 * h + 1) * HEAD_DIM] = (
            v_ref[:, h * HEAD_DIM:(h + 1) * HEAD_DIM].astype(vo_ref.dtype))
        vo_ref[:, (2 * h + 1) * HEAD_DIM:(2 * h + 2) * HEAD_DIM] = ones


def _qkv_prep(proj2d, q_norm, k_norm, cos, sin, b, t, tm):
    m = proj2d.shape[0]
    tb = t // tm
    kv_blk = (2 * HALF) // KV_W
    return pl.pallas_call(
        _qkv_prep_kernel,
        grid=(m // tm,),
        in_specs=[pl.BlockSpec((tm, HALF), lambda i: (i, 1)),
                  pl.BlockSpec((tm, KV_W), lambda i: (i, kv_blk)),
                  pl.BlockSpec((tm, KV_W), lambda i: (i, kv_blk + 1)),
                  pl.BlockSpec((1, HEAD_DIM), lambda i: (0, 0)),
                  pl.BlockSpec((1, HEAD_DIM), lambda i: (0, 0)),
                  pl.BlockSpec((tm, HEAD_DIM), lambda i: (i % tb, 0)),
                  pl.BlockSpec((tm, HEAD_DIM), lambda i: (i % tb, 0))],
        out_specs=[pl.BlockSpec((tm, HALF), lambda i: (i, 0)),
                   pl.BlockSpec((1, KV_W, tm), lambda i: (i // tb, 0, i % tb)),
                   pl.BlockSpec((tm, 2 * KV_W), lambda i: (i, 0))],
        out_shape=[jax.ShapeDtypeStruct((m, HALF), BF16),
                   jax.ShapeDtypeStruct((b, KV_W, t), BF16),
                   jax.ShapeDtypeStruct((m, 2 * KV_W), BF16)],
        compiler_params=_params(("parallel",)),
        name="qkv_prep",
    )(proj2d, proj2d, proj2d, q_norm.reshape(1, HEAD_DIM), k_norm.reshape(1, HEAD_DIM), cos, sin)


GQA_GROUP = GQA_Q_HEADS // GQA_KV_HEADS


def _gqa_kernel(q_ref, kt_ref, v_ref, o_ref):
    def scores(h):
        return _dot(q_ref[0, :, h * HEAD_DIM:(h + 1) * HEAD_DIM], kt_ref[0])

    s_next = scores(0)
    for h in range(GQA_GROUP):
        s = s_next
        if h + 1 < GQA_GROUP:
            s_next = scores(h + 1)
        m = jnp.max(s, axis=-1, keepdims=True)
        p = jnp.exp2(s - m).astype(BF16)
        ol = _dot(p, v_ref[0])
        o = ol[:, :HEAD_DIM] / ol[:, HEAD_DIM:]
        o_ref[0, :, h * HEAD_DIM:(h + 1) * HEAD_DIM] = o.astype(o_ref.dtype)


def _gqa(q, kt, v, tq):
    b, t, _ = q.shape
    gw = GQA_GROUP * HEAD_DIM
    return pl.pallas_call(
        _gqa_kernel,
        grid=(b, GQA_KV_HEADS, t // tq),
        in_specs=[pl.BlockSpec((1, tq, gw), lambda i, h, j: (i, j, h)),
                  pl.BlockSpec((1, HEAD_DIM, t), lambda i, h, j: (i, h, 0)),
                  pl.BlockSpec((1, t, 2 * HEAD_DIM), lambda i, h, j: (i, 0, h))],
        out_specs=pl.BlockSpec((1, tq, gw), lambda i, h, j: (i, j, h)),
        out_shape=jax.ShapeDtypeStruct((b, t, HALF), BF16),
        compiler_params=_params(("parallel", "parallel", "parallel")),
        name="gqa",
    )(q, kt, v)


def kernel(x, ab_w_in, ab_w_out, na_rpb, hg_lb_logits, hg_norm_w, cd_w_in, cd_w_out, pool_w,
           pool_scale, d_q_norm, d_k_norm, ln_mix_g, ln_mix_b, ffn_w_gate, ffn_w_up, ffn_w_down,
           ln_ffn_g, ln_ffn_b):
    b, t, d = x.shape
    m = b * t
    cos, sin = _rope_tables(t)
    xf = x.reshape(m, d)
    xb = xf.astype(BF16)
    ab_w_out_b = ab_w_out.astype(BF16)
    cd_w_out_b = cd_w_out.astype(BF16)
    pool_w_b = pool_w.astype(BF16)
    ffn_w_down_b = ffn_w_down.astype(BF16)
    for layer in range(DEPTH):
        j = layer // 2
        if layer % 2 == 0:
            proj = _matmul(xb, ab_w_in, j, F32, *AB_IN_TILE, "ab_in_proj").reshape(b, t, -1)
            y1 = _na_attention(proj, na_rpb[j])
            y2 = _hgrn(proj, hg_lb_logits, hg_norm_w[j], j)
            w_out = ab_w_out_b
        else:
            proj = _matmul(xb, cd_w_in, j, F32, *CD_IN_TILE, "cd_in_proj")
            y1 = _pool(proj.reshape(b, t, -1), pool_w_b[j], pool_scale[j])
            qn, kt, vn = _qkv_prep(proj, d_q_norm[j], d_k_norm[j], cos, sin, b, t, QKV_PREP_ROWS)
            y2 = _gqa(qn.reshape(b, t, HALF), kt, vn.reshape(b, t, 2 * KV_W), GQA_QUERY_ROWS)
            w_out = cd_w_out_b
        xf, xb = _mix_out(y1.reshape(m, HALF), y2.reshape(m, HALF), w_out, j, xf,
                          ln_mix_g[layer], ln_mix_b[layer], MIX_OUT_ROWS)
        h = _ffn_up(xb, ffn_w_gate, ffn_w_up, layer, *FFN_UP_TILE)
        xf, xb = _ffn_down(h, ffn_w_down_b, layer, xf, ln_ffn_g[layer], ln_ffn_b[layer],
                           FFN_DOWN_ROWS)
    return xf.reshape(b, t, d)
```

```python
import functools

import jax
import jax.numpy as jnp
from jax import lax
from jax.experimental import pallas as pl
from jax.experimental.pallas import tpu as pltpu

D_MODEL = 2048
DEPTH = 4
GRID_W = 64
HALF = 1024
HEAD_DIM = 128
NA_HEADS = 8
NA_WIN_ROWS = 8
NA_WIN_COLS = 16
HG_HEADS = 8
HG_CHUNK = 128
POOL_WINDOWS = (2, 4, 8, 16)
POOL_GROUP_DIM = 256
GQA_Q_HEADS = 8
GQA_KV_HEADS = 2
KV_W = GQA_KV_HEADS * HEAD_DIM
ROPE_THETA = 10000.0
FFN_HIDDEN = 5632
DN_ALPHA = (2 * DEPTH) ** 0.25
LN_EPS = 1e-5
RMS_EPS = 1e-6
NEG_BIG = -1e30
LOG2E = 1.4426950408889634
ATTN_SCALE_LOG2 = HEAD_DIM ** -0.5 * LOG2E

V7X_VMEM_LIMIT_BYTES = 56 * 1024 * 1024

AB_IN_TILE = (1024, 1024)
CD_IN_TILE = (1024, 1280)
MIX_OUT_ROWS = 512
MIX_OUT_PARTS = 4
FFN_UP_TILE = (1024, 512)
FFN_DOWN_ROWS = 256
QKV_PREP_ROWS = 512
GQA_QUERY_ROWS = 1024
NA_BATCH_PER_STEP = 1

F32 = jnp.float32
BF16 = jnp.bfloat16


def _params(semantics):
    return pltpu.CompilerParams(dimension_semantics=semantics,
                                vmem_limit_bytes=V7X_VMEM_LIMIT_BYTES)


def _sigmoid(x):
    return 1.0 / (1.0 + jnp.exp(-x))


def _silu(x):
    return x * _sigmoid(x)


def _dot(a, b):
    return jnp.dot(a, b, preferred_element_type=F32)


def _dot_nt(a, b):
    return lax.dot_general(a, b, (((1,), (1,)), ((), ())), preferred_element_type=F32)


def _dot_tn(a, b):
    return lax.dot_general(a, b, (((0,), (0,)), ((), ())), preferred_element_type=F32)


def _matmul_kernel(a_ref, w_ref, o_ref, wb_ref):
    @pl.when(pl.program_id(1) == 0)
    def _():
        wb_ref[...] = w_ref[...].astype(BF16)

    a = a_ref[...]
    if a.dtype != BF16:
        a = a.astype(BF16)
    o_ref[...] = _dot(a, wb_ref[...]).astype(o_ref.dtype)


def _matmul(a, w, layer, out_dtype, tm, tn, name):
    m, k = a.shape
    n = w.shape[2]
    return pl.pallas_call(
        _matmul_kernel,
        grid=(n // tn, m // tm),
        in_specs=[pl.BlockSpec((tm, k), lambda j, i: (i, 0)),
                  pl.BlockSpec((None, k, tn), lambda j, i: (layer, 0, j))],
        out_specs=pl.BlockSpec((tm, tn), lambda j, i: (i, j)),
        out_shape=jax.ShapeDtypeStruct((m, n), out_dtype),
        scratch_shapes=[pltpu.VMEM((k, tn), BF16)],
        compiler_params=_params(("arbitrary", "arbitrary")),
        name=name,
    )(a, w)


def _res_ln(acc, x, g, b):
    z = DN_ALPHA * x + acc
    mu = jnp.mean(z, axis=-1, keepdims=True)
    zc = z - mu
    var = jnp.mean(zc * zc, axis=-1, keepdims=True)
    return zc * lax.rsqrt(var + LN_EPS) * g + b


def _row_parts(tm, parts):
    size = tm // parts
    return [slice(p * size, (p + 1) * size) for p in range(parts)]


def _mix_out_kernel(y1_ref, y2_ref, w_ref, x_ref, g_ref, b_ref, of_ref, ob_ref, *, parts):
    half = y1_ref.shape[1]
    rows = _row_parts(y1_ref.shape[0], parts)
    accs = [_dot(y1_ref[r, :], w_ref[:half, :]) + _dot(y2_ref[r, :], w_ref[half:, :]) for r in rows]
    for r, acc in zip(rows, accs):
        out = _res_ln(acc, x_ref[r, :], g_ref[...], b_ref[...])
        of_ref[r, :] = out
        ob_ref[r, :] = out.astype(BF16)


def _mix_out(y1, y2, w, layer, x, g, b, tm):
    m, half = y1.shape
    d = w.shape[2]
    row = lambda i: (i, 0)
    fixed = lambda i: (0, 0)
    return pl.pallas_call(
        functools.partial(_mix_out_kernel, parts=MIX_OUT_PARTS),
        grid=(m // tm,),
        in_specs=[pl.BlockSpec((tm, half), row), pl.BlockSpec((tm, half), row),
                  pl.BlockSpec((None, 2 * half, d), lambda i: (layer, 0, 0),
                               pipeline_mode=pl.Buffered(1)),
                  pl.BlockSpec((tm, d), row),
                  pl.BlockSpec((1, d), fixed), pl.BlockSpec((1, d), fixed)],
        out_specs=[pl.BlockSpec((tm, d), row), pl.BlockSpec((tm, d), row)],
        out_shape=[jax.ShapeDtypeStruct((m, d), F32), jax.ShapeDtypeStruct((m, d), BF16)],
        compiler_params=_params(("parallel",)),
        name="mix_out_ln",
    )(y1, y2, w, x, g.reshape(1, d), b.reshape(1, d))


def _ffn_up_kernel(x_ref, wg_ref, wu_ref, h_ref, wgb_ref, wub_ref):
    @pl.when(pl.program_id(1) == 0)
    def _():
        wgb_ref[...] = wg_ref[...].astype(BF16)
        wub_ref[...] = wu_ref[...].astype(BF16)

    x = x_ref[...]
    gate = _dot(x, wgb_ref[...])
    up = _dot(x, wub_ref[...])
    h_ref[...] = (_silu(gate) * up).astype(h_ref.dtype)


def _ffn_up(xb, wg, wu, layer, tm, tn):
    m, k = xb.shape
    n = wg.shape[2]
    return pl.pallas_call(
        _ffn_up_kernel,
        grid=(n // tn, m // tm),
        in_specs=[pl.BlockSpec((tm, k), lambda j, i: (i, 0)),
                  pl.BlockSpec((None, k, tn), lambda j, i: (layer, 0, j)),
                  pl.BlockSpec((None, k, tn), lambda j, i: (layer, 0, j))],
        out_specs=pl.BlockSpec((tm, tn), lambda j, i: (i, j)),
        out_shape=jax.ShapeDtypeStruct((m, n), BF16),
        scratch_shapes=[pltpu.VMEM((k, tn), BF16), pltpu.VMEM((k, tn), BF16)],
        compiler_params=_params(("arbitrary", "arbitrary")),
        name="ffn_up",
    )(xb, wg, wu)


def _ffn_down_kernel(h_ref, w_ref, x_ref, g_ref, b_ref, of_ref, ob_ref):
    out = _res_ln(_dot(h_ref[...], w_ref[...]), x_ref[...], g_ref[...], b_ref[...])
    of_ref[...] = out
    ob_ref[...] = out.astype(BF16)


def _ffn_down(h, w, layer, x, g, b, tm):
    m, kk = h.shape
    d = w.shape[2]
    row = lambda i: (i, 0)
    fixed = lambda i: (0, 0)
    return pl.pallas_call(
        _ffn_down_kernel,
        grid=(m // tm,),
        in_specs=[pl.BlockSpec((tm, kk), row),
                  pl.BlockSpec((None, kk, d), lambda i: (layer, 0, 0), pipeline_mode=pl.Buffered(1)),
                  pl.BlockSpec((tm, d), row),
                  pl.BlockSpec((1, d), fixed), pl.BlockSpec((1, d), fixed)],
        out_specs=[pl.BlockSpec((tm, d), row), pl.BlockSpec((tm, d), row)],
        out_shape=[jax.ShapeDtypeStruct((m, d), F32), jax.ShapeDtypeStruct((m, d), BF16)],
        compiler_params=_params(("parallel",)),
        name="ffn_down_ln",
    )(h, w, x, g.reshape(1, d), b.reshape(1, d))


NA_GROUP = 4
NA_WINDOW = NA_GROUP + NA_WIN_ROWS
RPB_ROWS = 2 * NA_WIN_ROWS - 1
RPB_COLS = 2 * NA_WIN_COLS - 1


def _na_window_start(g, rows):
    return min(max(g * NA_GROUP - NA_WIN_ROWS // 2, 0), rows - NA_WINDOW)


def _na_patterns(rows):
    pats, ids = [], []
    for r in range(rows):
        w0 = _na_window_start(r // NA_GROUP, rows)
        r0 = min(max(r - NA_WIN_ROWS // 2, 0), rows - NA_WIN_ROWS)
        key = (r0 - w0, w0 - r + NA_WIN_ROWS - 1)
        if key not in pats:
            pats.append(key)
        ids.append(pats.index(key))
    return tuple(pats), tuple(ids)


def _na_band_kernel(rpb_ref, o_ref, *, patterns):
    h = pl.program_id(0)
    base = h * (RPB_ROWS * RPB_COLS)
    shape = (GRID_W, 2 * GRID_W)
    q = lax.broadcasted_iota(jnp.int32, shape, 0)
    lane = lax.broadcasted_iota(jnp.int32, shape, 1)
    kc = lane & (GRID_W - 1)
    second = lane >> 6
    code = kc - q + (NA_WIN_COLS - 1) + RPB_COLS * second
    start = jnp.clip(q - NA_WIN_COLS // 2, 0, GRID_W - NA_WIN_COLS)
    valid = (kc >= start) & (kc < start + NA_WIN_COLS)
    neg = jnp.full(shape, NEG_BIG, F32)
    pairs = {}

    def pair(ri):
        if ri not in pairs:
            acc = neg
            for i in range(2 * RPB_COLS):
                if 0 <= ri + i // RPB_COLS < RPB_ROWS:
                    acc = jnp.where(code == i, rpb_ref[base + ri * RPB_COLS + i], acc)
            pairs[ri] = jnp.where(valid, acc * LOG2E, NEG_BIG)
        return pairs[ri]

    for p, (lo, c) in enumerate(patterns):
        for u in range(NA_WINDOW // 2):
            in0 = lo <= 2 * u < lo + NA_WIN_ROWS
            in1 = lo <= 2 * u + 1 < lo + NA_WIN_ROWS
            if in0 and in1:
                tile = pair(2 * u + c)
            elif in0:
                tile = jnp.where(second == 0, pair(2 * u + c), NEG_BIG)
            elif in1:
                tile = jnp.where(second == 1, pair(2 * u + c), NEG_BIG)
            else:
                tile = neg
            o_ref[0, p, :, u * 2 * GRID_W:(u + 1) * 2 * GRID_W] = tile


def _na_band(rpb, patterns):
    shape = (NA_HEADS, len(patterns), GRID_W, NA_WINDOW * GRID_W)
    return pl.pallas_call(
        functools.partial(_na_band_kernel, patterns=patterns),
        grid=(NA_HEADS,),
        in_specs=[pl.BlockSpec(memory_space=pltpu.SMEM)],
        out_specs=pl.BlockSpec((1,) + shape[1:], lambda h: (h, 0, 0, 0)),
        out_shape=jax.ShapeDtypeStruct(shape, F32),
        compiler_params=_params(("arbitrary",)),
        name="na_band",
    )(rpb.reshape(-1))


def _na_kernel(q_ref, k_ref, v_ref, band_ref, o_ref, kt_ref, vb_ref, *, pattern_ids):
    t = q_ref.shape[1]
    rows = t // GRID_W
    gq = NA_GROUP * GRID_W
    wlen = NA_WINDOW * GRID_W
    n_groups = rows // NA_GROUP
    scale = ATTN_SCALE_LOG2
    for bi in range(q_ref.shape[0]):
        for c in range(0, t, gq):
            kt_ref[:, c:c + gq] = k_ref[bi, c:c + gq, :].T.astype(BF16)
        vb_ref[:, :HEAD_DIM] = v_ref[bi].astype(BF16)
        vb_ref[:, HEAD_DIM:] = jnp.ones((t, HEAD_DIM), BF16)

        def scores(g):
            w0 = _na_window_start(g, rows) * GRID_W
            q = (q_ref[bi, g * gq:(g + 1) * gq, :] * scale).astype(BF16)
            bias = jnp.concatenate(
                [band_ref[0, pattern_ids[g * NA_GROUP + i]] for i in range(NA_GROUP)], axis=0)
            return _dot(q, kt_ref[:, w0:w0 + wlen]) + bias

        s_next = scores(0)
        for g in range(n_groups):
            s = s_next
            if g + 1 < n_groups:
                s_next = scores(g + 1)
            w0 = _na_window_start(g, rows) * GRID_W
            m = jnp.max(s, axis=-1, keepdims=True)
            p = jnp.exp2(s - m).astype(BF16)
            ol = _dot(p, vb_ref[w0:w0 + wlen, :])
            o = ol[:, :HEAD_DIM] / ol[:, HEAD_DIM:]
            o_ref[bi, g * gq:(g + 1) * gq, :] = o.astype(o_ref.dtype)


def _na_attention(proj, rpb):
    b, t, _ = proj.shape
    nh = NA_HEADS
    nb = NA_BATCH_PER_STEP
    patterns, pattern_ids = _na_patterns(t // GRID_W)
    band = _na_band(rpb, patterns)
    blk = (nb, t, HEAD_DIM)
    return pl.pallas_call(
        functools.partial(_na_kernel, pattern_ids=pattern_ids),
        grid=(nh, b // nb),
        in_specs=[pl.BlockSpec(blk, lambda h, i: (i, 0, h)),
                  pl.BlockSpec(blk, lambda h, i: (i, 0, nh + h)),
                  pl.BlockSpec(blk, lambda h, i: (i, 0, 2 * nh + h)),
                  pl.BlockSpec((1,) + band.shape[1:], lambda h, i: (h, 0, 0, 0))],
        out_specs=pl.BlockSpec(blk, lambda h, i: (i, 0, h)),
        out_shape=jax.ShapeDtypeStruct((b, t, HALF), BF16),
        scratch_shapes=[pltpu.VMEM((HEAD_DIM, t), BF16), pltpu.VMEM((t, 2 * HEAD_DIM), BF16)],
        compiler_params=_params(("parallel", "parallel")),
        name="na_attention",
    )(proj, proj, proj, band)


HG_LEVELS = tuple(HG_CHUNK >> (i + 1) for i in range(HG_CHUNK.bit_length() - 1))
HG_DIAG = len(HG_LEVELS)


def _split2(x):
    hi = x.astype(BF16)
    mid = (x - hi.astype(F32)).astype(BF16)
    return hi, mid


def _level_codes(c, rev):
    t = lax.broadcasted_iota(jnp.int32, (c, c), 0)
    s = lax.broadcasted_iota(jnp.int32, (c, c), 1)
    x = t ^ s
    code = jnp.full((c, c), HG_DIAG, jnp.int32)
    for li in reversed(range(len(HG_LEVELS))):
        code = jnp.where(x >= HG_LEVELS[li], li, code)
    return jnp.where((s >= t) if rev else (s <= t), code, -1)


def _pivot_rows(gc, m, rev):
    c = gc.shape[0]
    blk = 2 * m
    if blk >= 8:
        piv = m if rev else m - 1
        parts = []
        for s in range(0, c, blk):
            parts.append(jnp.broadcast_to(gc[s + piv:s + piv + 1, :], (blk, gc.shape[1])))
        return parts[0] if len(parts) == 1 else jnp.concatenate(parts, axis=0)
    pos = lax.broadcasted_iota(jnp.int32, gc.shape, 0) & (blk - 1)
    piv = m if rev else m - 1
    out = gc
    for p in range(blk):
        if p == piv:
            continue
        shift = (p - piv) % c
        out = jnp.where(pos == p, pltpu.roll(gc, shift, 0), out)
    return out


def _neg_abs(x):
    bits = lax.bitcast_convert_type(x, jnp.uint32) | jnp.uint32(0x80000000)
    return lax.bitcast_convert_type(bits, F32)


def _hgrn_prepare(zs, qs, lbs, dirs, tri_ref, codes_ref):
    c = qs[0].shape[0]
    sub = 8
    units = range(len(zs))
    f = [lb + (1.0 - lb) * _sigmoid(z) for z, lb in zip(zs, lbs)]
    k = [1.0 - x for x in f]
    parts = [_split2(jnp.log2(x)) for x in f]
    gc = [_dot(tri_ref[int(dirs[d])], parts[d][0]) + _dot(tri_ref[int(dirs[d])], parts[d][1]) for d in units]
    pos = lax.broadcasted_iota(jnp.int32, qs[0].shape, 0)

    def tiles(x):
        return [x[i:i + sub] for i in range(0, c, sub)]

    def code_tiles(d):
        return [codes_ref[int(dirs[d]), i:i + sub, :] for i in range(0, c, sub)]

    a = []
    for d in units:
        diag = jnp.sum(qs[d] * k[d], axis=-1, keepdims=True)
        a.append([jnp.where(ct == HG_DIAG, dt, 0.0) for ct, dt in zip(code_tiles(d), tiles(diag))])

    def merge(d, li, prod):
        a[d] = [jnp.where(ct == li, pt, at) for ct, pt, at in zip(code_tiles(d), tiles(prod), a[d])]

    def small_level(m, decay):
        li = HG_LEVELS.index(m)
        for d, rev in enumerate(dirs):
            is_q = ((pos & m) == 0) if rev else ((pos & m) != 0)
            xl = (jnp.where(is_q, qs[d], k[d]) * decay[d]).astype(BF16)
            merge(d, li, _dot_nt(xl, xl))

    if 1 in HG_LEVELS:
        ones = jnp.ones_like(qs[0])
        dec1 = []
        for d, rev in enumerate(dirs):
            is_q = ((pos & 1) == 0) if rev else ((pos & 1) != 0)
            dec1.append(jnp.where(is_q, f[d], ones))
        small_level(1, dec1)
    if 2 in HG_LEVELS:
        p4 = pos & 3
        dec2 = []
        for d, rev in enumerate(dirs):
            nxt = pltpu.roll(f[d], c - 1, 0)
            prv = pltpu.roll(f[d], 1, 0)
            if rev:
                e = jnp.where(p4 == 0, f[d] * nxt, jnp.where(p4 == 1, f[d], jnp.where(p4 == 2, 1.0, prv)))
            else:
                e = jnp.where(p4 == 0, nxt, jnp.where(p4 == 1, 1.0, jnp.where(p4 == 2, f[d], f[d] * prv)))
            dec2.append(e)
        small_level(2, dec2)

    for li, m in enumerate(HG_LEVELS):
        if m <= 2:
            continue
        e = [jnp.exp2(_neg_abs(gc[d] - _pivot_rows(gc[d], m, rev))) for d, rev in enumerate(dirs)]
        if m < sub:
            small_level(m, e)
            continue
        for d, rev in enumerate(dirs):
            q_rows, rhs = [], []
            for s in range(0, c, 2 * m):
                lo, hi = slice(s, s + m), slice(s + m, s + 2 * m)
                q_sl, k_sl = (lo, hi) if rev else (hi, lo)
                q_rows.append(q_sl)
                kx = k[d][k_sl] * e[d][k_sl]
                rhs += [k[d][lo], kx] if rev else [kx, k[d][hi]]
            lhs = jnp.concatenate([qs[d][r] * e[d][r] for r in q_rows], axis=0).astype(BF16)
            prod = tiles(_dot_nt(lhs, jnp.concatenate(rhs, axis=0).astype(BF16)))
            ct = code_tiles(d)
            j = 0
            for r in q_rows:
                for i in range(r.start // sub, r.stop // sub):
                    a[d][i] = jnp.where(ct[i] == li, prod[j], a[d][i])
                    j += 1

    out = []
    for d, rev in enumerate(dirs):
        g_tot = gc[d][0:1, :] if rev else gc[d][c - 1:c, :]
        qg = (qs[d] * jnp.exp2(gc[d])).astype(BF16)
        kd = (k[d] * jnp.exp2(g_tot - gc[d])).astype(BF16)
        out.append((jnp.concatenate(a[d], axis=0).astype(BF16), qg, kd, jnp.exp2(g_tot)))
    return out


HG_DEC_ROWS = 8
HG_PASS_ROWS = 256
HG_HEADS_PER_STEP = 2
HG_LOOP_UNROLL = 3


def _row_block(idx, size):
    start = idx * size
    return pl.ds(start if isinstance(start, int) else pl.multiple_of(start, size), size)


def _hgrn_kernel(q_ref, ff_ref, fb_ref, i_ref, g_ref, lbl_ref, nw_ref, o_ref,
                 tri_ref, codes_ref, a_ref, qg_ref, kd_ref, dec_ref, qa_ref, vb_ref, acc_ref, *, layer):
    t = q_ref.shape[1]
    c = HG_CHUNK
    n = t // c
    dk = dv = HEAD_DIM
    heads = q_ref.shape[2] // HEAD_DIM
    units = [(hh, rev) for hh in range(heads) for rev in (False, True)]
    dirs = [rev for _, rev in units]

    def cols(hh):
        return slice(hh * HEAD_DIM, (hh + 1) * HEAD_DIM)

    lg = lbl_ref[...]
    e = jnp.exp(lg - jnp.max(lg, axis=0, keepdims=True))
    p = e / jnp.sum(e, axis=0, keepdims=True)
    lb = jnp.sum(p[:layer + 1, :], axis=0, keepdims=True) - p[0:1, :]
    lbs = [lb[:, cols(hh)] for hh, _ in units]

    row = lax.broadcasted_iota(jnp.int32, (c, c), 0)
    col = lax.broadcasted_iota(jnp.int32, (c, c), 1)
    tri_ref[0] = jnp.where(col <= row, 1.0, 0.0).astype(BF16)
    tri_ref[1] = jnp.where(col >= row, 1.0, 0.0).astype(BF16)
    codes_ref[0] = _level_codes(c, False)
    codes_ref[1] = _level_codes(c, True)

    act_rows = HG_PASS_ROWS

    def activate(i, carry):
        rows = _row_block(i, act_rows)
        qa_ref[rows, :] = _silu(q_ref[0, rows, :])
        vb_ref[rows, :] = i_ref[0, rows, :].astype(BF16)
        return carry

    lax.fori_loop(0, t // act_rows, activate, 0)

    def chunk_rows(i):
        return [_row_block(i, c), _row_block(n - 1 - i, c)]

    def dec_rows(i, size):
        return [pl.ds(r.start, size) for r in (_row_block(i, HG_DEC_ROWS), _row_block(n - 1 - i, HG_DEC_ROWS))]

    def prepare(i):
        rows = chunk_rows(i)
        zs = [(fb_ref if rev else ff_ref)[0, rows[rev], cols(hh)] for hh, rev in units]
        qs = [qa_ref[rows[rev], cols(hh)] for hh, rev in units]
        staged = _hgrn_prepare(zs, qs, lbs, dirs, tri_ref, codes_ref)
        for u, ((_, rev), (a, qg, kd, dec)) in enumerate(zip(units, staged)):
            a_ref[u, rows[rev], :] = a
            qg_ref[u, rows[rev], :] = qg
            kd_ref[u, rows[rev], :] = kd
            dec_ref[u, dec_rows(i, HG_DEC_ROWS)[rev], :] = jnp.broadcast_to(dec, (HG_DEC_ROWS, dk))

    def recur(i, states):
        rows = chunk_rows(i)
        new = []
        for u, (hh, rev) in enumerate(units):
            r = rows[rev]
            vb = vb_ref[r, cols(hh)]
            acc_ref[u, r, :] = _dot(a_ref[u, r, :], vb) + _dot_nt(qg_ref[u, r, :], states[u].astype(BF16))
            dec = dec_ref[u, dec_rows(i, 1)[rev], :]
            new.append(states[u] * dec + _dot_tn(vb, kd_ref[u, r, :]))
        return tuple(new)

    def body(i, states):
        states = recur(i, states)
        prepare(i + 1)
        return states

    zero = jnp.zeros((dv, dk), F32)
    prepare(0)
    recur(n - 1, lax.fori_loop(0, n - 1, body, (zero,) * len(units), unroll=HG_LOOP_UNROLL))

    rows = HG_PASS_ROWS

    def finish(i, carry):
        r = _row_block(i, rows)
        for hh in range(heads):
            o = acc_ref[2 * hh, r, :] + acc_ref[2 * hh + 1, r, :]
            ms = jnp.mean(o * o, axis=-1, keepdims=True)
            y = o * lax.rsqrt(ms + RMS_EPS) * nw_ref[:, cols(hh)]
            o_ref[0, r, cols(hh)] = (y * _silu(g_ref[0, r, cols(hh)])).astype(o_ref.dtype)
        return carry

    lax.fori_loop(0, t // rows, finish, 0, unroll=4)


def _hgrn(proj, lb_logits, norm_w, layer):
    b, t, _ = proj.shape
    nh = HG_HEADS
    g = HG_HEADS_PER_STEP
    n_layers = lb_logits.shape[0]
    width = g * HEAD_DIM
    blk = (1, t, width)
    base = 3 * HALF // width
    units = 2 * g

    def col(k):
        return lambda i, h: (i, 0, base + k * (nh // g) + h)

    return pl.pallas_call(
        functools.partial(_hgrn_kernel, layer=layer),
        grid=(b, nh // g),
        in_specs=[pl.BlockSpec(blk, col(0)), pl.BlockSpec(blk, col(1)), pl.BlockSpec(blk, col(2)),
                  pl.BlockSpec(blk, col(3)), pl.BlockSpec(blk, col(4)),
                  pl.BlockSpec((n_layers, width), lambda i, h: (0, h)),
                  pl.BlockSpec((1, width), lambda i, h: (0, h))],
        out_specs=pl.BlockSpec(blk, lambda i, h: (i, 0, h)),
        out_shape=jax.ShapeDtypeStruct((b, t, HALF), BF16),
        scratch_shapes=[pltpu.VMEM((2, HG_CHUNK, HG_CHUNK), BF16),
                        pltpu.VMEM((2, HG_CHUNK, HG_CHUNK), jnp.int32),
                        pltpu.VMEM((units, t, HG_CHUNK), BF16),
                        pltpu.VMEM((units, t, HEAD_DIM), BF16),
                        pltpu.VMEM((units, t, HEAD_DIM), BF16),
                        pltpu.VMEM((units, t // HG_CHUNK * HG_DEC_ROWS, HEAD_DIM), F32),
                        pltpu.VMEM((t, width), F32),
                        pltpu.VMEM((t, width), BF16),
                        pltpu.VMEM((units, t, HEAD_DIM), F32)],
        compiler_params=_params(("parallel", "parallel")),
        name="hgrn2",
    )(proj, proj, proj, proj, proj, lb_logits, norm_w.reshape(1, HALF))


POOL_PAD = 16
POOL_ROWS = 256


def _pool_kernel(x_ref, w_ref, s_ref, o_ref, xp_ref, sa_ref, sb_ref):
    t = x_ref.shape[1]
    width = x_ref.shape[2]
    gd = POOL_GROUP_DIM
    half = POOL_PAD // 2
    lo, hi = half, t + POOL_PAD + half
    xp_ref[0:POOL_PAD, :] = jnp.zeros((POOL_PAD, width), F32)
    xp_ref[POOL_PAD + t:, :] = jnp.zeros((POOL_PAD, width), F32)
    xp_ref[POOL_PAD:POOL_PAD + t, :] = x_ref[0]
    for ref in (sa_ref, sb_ref):
        ref[0:lo, :] = jnp.zeros((lo, gd), F32)
        ref[hi:, :] = jnp.zeros((t + 2 * POOL_PAD - hi, gd), F32)

    for gi, win in enumerate(POOL_WINDOWS):
        c0, c1 = gi * gd, (gi + 1) * gd
        sa_ref[lo:hi, :] = xp_ref[lo - 1:hi - 1, c0:c1] + xp_ref[lo:hi, c0:c1]
        src, dst, span = sa_ref, sb_ref, 2
        while span < win:
            sh = span // 2
            dst[lo:hi, :] = src[lo - sh:hi - sh, :] + src[lo + sh:hi + sh, :]
            src, dst, span = dst, src, 2 * span
        hw = win // 2
        for t0 in range(0, t, POOL_ROWS):
            pos = t0 + lax.broadcasted_iota(jnp.int32, (POOL_ROWS, gd), 0)
            cnt = (jnp.minimum(pos + hw, t) - jnp.maximum(pos - hw, 0)).astype(F32)
            tot = src[POOL_PAD + t0:POOL_PAD + t0 + POOL_ROWS, :]
            pooled = tot / cnt - x_ref[0, t0:t0 + POOL_ROWS, c0:c1]
            y = _dot(pooled.astype(BF16), w_ref[gi]) * s_ref[:, c0:c1]
            o_ref[0, t0:t0 + POOL_ROWS, c0:c1] = y.astype(o_ref.dtype)


def _pool(proj, pool_w, scale):
    b, t, _ = proj.shape
    ng = len(POOL_WINDOWS)
    return pl.pallas_call(
        _pool_kernel,
        grid=(b,),
        in_specs=[pl.BlockSpec((1, t, HALF), lambda i: (i, 0, 0)),
                  pl.BlockSpec((ng, POOL_GROUP_DIM, POOL_GROUP_DIM), lambda i: (0, 0, 0)),
                  pl.BlockSpec((1, HALF), lambda i: (0, 0))],
        out_specs=pl.BlockSpec((1, t, HALF), lambda i: (i, 0, 0)),
        out_shape=jax.ShapeDtypeStruct((b, t, HALF), BF16),
        scratch_shapes=[pltpu.VMEM((t + 2 * POOL_PAD, HALF), F32),
                        pltpu.VMEM((t + 2 * POOL_PAD, POOL_GROUP_DIM), F32),
                        pltpu.VMEM((t + 2 * POOL_PAD, POOL_GROUP_DIM), F32)],
        compiler_params=_params(("parallel",)),
        name="pool",
    )(proj, pool_w, scale.reshape(1, HALF))


def _rope_tables(t):
    pos = jnp.arange(t)
    row = (pos // GRID_W).astype(F32)
    col = (pos % GRID_W).astype(F32)
    n_freq = HEAD_DIM // 4
    inv = ROPE_THETA ** (-jnp.arange(n_freq, dtype=F32) / n_freq)
    ang = jnp.concatenate([row[:, None] * inv, col[:, None] * inv], axis=-1)
    cos = jnp.repeat(jnp.cos(ang), 2, axis=-1)
    sin = jnp.repeat(jnp.sin(ang), 2, axis=-1)
    sign = jnp.where(jnp.arange(HEAD_DIM) % 2 == 0, -1.0, 1.0).astype(F32)
    return cos, sin * sign


def _norm_rope(x, g, cos, sin_signed, scale):
    ms = jnp.mean(x * x, axis=-1, keepdims=True)
    xn = x * lax.rsqrt(ms + RMS_EPS) * g
    lane = lax.broadcasted_iota(jnp.int32, x.shape, 1)
    partner = jnp.where((lane & 1) == 0, pltpu.roll(xn, HEAD_DIM - 1, 1), pltpu.roll(xn, 1, 1))
    return (xn * cos + partner * sin_signed) * scale


def _qkv_prep_kernel(q_ref, k_ref, v_ref, qn_ref, kn_ref, cos_ref, sin_ref, qo_ref, kt_ref, vo_ref):
    cos = cos_ref[...]
    sin = sin_ref[...]
    for h in range(GQA_Q_HEADS):
        sl = slice(h * HEAD_DIM, (h + 1) * HEAD_DIM)
        qo_ref[:, sl] = _norm_rope(q_ref[:, sl], qn_ref[...], cos, sin,
                                   ATTN_SCALE_LOG2).astype(qo_ref.dtype)
    for h in range(GQA_KV_HEADS):
        sl = slice(h * HEAD_DIM, (h + 1) * HEAD_DIM)
        kt_ref[0, sl, :] = _norm_rope(k_ref[:, sl], kn_ref[...], cos, sin, 1.0).T.astype(kt_ref.dtype)
    ones = jnp.ones((v_ref.shape[0], HEAD_DIM), vo_ref.dtype)
    for h in range(GQA_KV_HEADS):
        vo_ref[:, 2 * h * HEAD_DIM:(2 * h + 1) * HEAD_DIM] = (
            v_ref[:, h * HEAD_DIM:(h + 1) * HEAD_DIM].astype(vo_ref.dtype))
        vo_ref[:, (2 * h + 1) * HEAD_DIM:(2 * h + 2) * HEAD_DIM] = ones


def _qkv_prep(proj2d, q_norm, k_norm, cos, sin, b, t, tm):
    m = proj2d.shape[0]
    tb = t // tm
    kv_blk = (2 * HALF) // KV_W
    return pl.pallas_call(
        _qkv_prep_kernel,
        grid=(m // tm,),
        in_specs=[pl.BlockSpec((tm, HALF), lambda i: (i, 1)),
                  pl.BlockSpec((tm, KV_W), lambda i: (i, kv_blk)),
                  pl.BlockSpec((tm, KV_W), lambda i: (i, kv_blk + 1)),
                  pl.BlockSpec((1, HEAD_DIM), lambda i: (0, 0)),
                  pl.BlockSpec((1, HEAD_DIM), lambda i: (0, 0)),
                  pl.BlockSpec((tm, HEAD_DIM), lambda i: (i % tb, 0)),
                  pl.BlockSpec((tm, HEAD_DIM), lambda i: (i % tb, 0))],
        out_specs=[pl.BlockSpec((tm, HALF), lambda i: (i, 0)),
                   pl.BlockSpec((1, KV_W, tm), lambda i: (i // tb, 0, i % tb)),
                   pl.BlockSpec((tm, 2 * KV_W), lambda i: (i, 0))],
        out_shape=[jax.ShapeDtypeStruct((m, HALF), BF16),
                   jax.ShapeDtypeStruct((b, KV_W, t), BF16),
                   jax.ShapeDtypeStruct((m, 2 * KV_W), BF16)],
        compiler_params=_params(("parallel",)),
        name="qkv_prep",
    )(proj2d, proj2d, proj2d, q_norm.reshape(1, HEAD_DIM), k_norm.reshape(1, HEAD_DIM), cos, sin)


GQA_GROUP = GQA_Q_HEADS // GQA_KV_HEADS


def _gqa_kernel(q_ref, kt_ref, v_ref, o_ref):
    def scores(h):
        return _dot(q_ref[0, :, h * HEAD_DIM:(h + 1) * HEAD_DIM], kt_ref[0])

    s_next = scores(0)
    for h in range(GQA_GROUP):
        s = s_next
        if h + 1 < GQA_GROUP:
            s_next = scores(h + 1)
        m = jnp.max(s, axis=-1, keepdims=True)
        p = jnp.exp2(s - m).astype(BF16)
        ol = _dot(p, v_ref[0])
        o = ol[:, :HEAD_DIM] / ol[:, HEAD_DIM:]
        o_ref[0, :, h * HEAD_DIM:(h + 1) * HEAD_DIM] = o.astype(o_ref.dtype)


def _gqa(q, kt, v, tq):
    b, t, _ = q.shape
    gw = GQA_GROUP * HEAD_DIM
    return pl.pallas_call(
        _gqa_kernel,
        grid=(b, GQA_KV_HEADS, t // tq),
        in_specs=[pl.BlockSpec((1, tq, gw), lambda i, h, j: (i, j, h)),
                  pl.BlockSpec((1, HEAD_DIM, t), lambda i, h, j: (i, h, 0)),
                  pl.BlockSpec((1, t, 2 * HEAD_DIM), lambda i, h, j: (i, 0, h))],
        out_specs=pl.BlockSpec((1, tq, gw), lambda i, h, j: (i, j, h)),
        out_shape=jax.ShapeDtypeStruct((b, t, HALF), BF16),
        compiler_params=_params(("parallel", "parallel", "parallel")),
        name="gqa",
    )(q, kt, v)


def kernel(x, ab_w_in, ab_w_out, na_rpb, hg_lb_logits, hg_norm_w, cd_w_in, cd_w_out, pool_w,
           pool_scale, d_q_norm, d_k_norm, ln_mix_g, ln_mix_b, ffn_w_gate, ffn_w_up, ffn_w_down,
           ln_ffn_g, ln_ffn_b):
    b, t, d = x.shape
    m = b * t
    cos, sin = _rope_tables(t)
    xf = x.reshape(m, d)
    xb = xf
    ab_w_out_b = ab_w_out.astype(BF16)
    cd_w_out_b = cd_w_out.astype(BF16)
    pool_w_b = pool_w.astype(BF16)
    ffn_w_down_b = ffn_w_down.astype(BF16)
    for layer in range(DEPTH):
        j = layer // 2
        if layer % 2 == 0:
            proj = _matmul(xb, ab_w_in, j, F32, *AB_IN_TILE, "ab_in_proj").reshape(b, t, -1)
            y1 = _na_attention(proj, na_rpb[j])
            y2 = _hgrn(proj, hg_lb_logits, hg_norm_w[j], j)
            w_out = ab_w_out_b
        else:
            proj = _matmul(xb, cd_w_in, j, F32, *CD_IN_TILE, "cd_in_proj")
            y1 = _pool(proj.reshape(b, t, -1), pool_w_b[j], pool_scale[j])
            qn, kt, vn = _qkv_prep(proj, d_q_norm[j], d_k_norm[j], cos, sin, b, t, QKV_PREP_ROWS)
            y2 = _gqa(qn.reshape(b, t, HALF), kt, vn.reshape(b, t, 2 * KV_W), GQA_QUERY_ROWS)
            w_out = cd_w_out_b
        xf, xb = _mix_out(y1.reshape(m, HALF), y2.reshape(m, HALF), w_out, j, xf,
                          ln_mix_g[layer], ln_mix_b[layer], MIX_OUT_ROWS)
        h = _ffn_up(xb, ffn_w_gate, ffn_w_up, layer, *FFN_UP_TILE)
        xf, xb = _ffn_down(h, ffn_w_down_b, layer, xf, ln_ffn_g[layer], ln_ffn_b[layer],
                           FFN_DOWN_ROWS)
    return xf.reshape(b, t, d)
```

```python
import functools

import jax
import jax.numpy as jnp
from jax import lax
from jax.experimental import pallas as pl
from jax.experimental.pallas import tpu as pltpu

D_MODEL = 2048
DEPTH = 4
GRID_W = 64
HALF = 1024
HEAD_DIM = 128
NA_HEADS = 8
NA_WIN_ROWS = 8
NA_WIN_COLS = 16
HG_HEADS = 8
HG_CHUNK = 128
POOL_WINDOWS = (2, 4, 8, 16)
POOL_GROUP_DIM = 256
GQA_Q_HEADS = 8
GQA_KV_HEADS = 2
KV_W = GQA_KV_HEADS * HEAD_DIM
ROPE_THETA = 10000.0
FFN_HIDDEN = 5632
DN_ALPHA = (2 * DEPTH) ** 0.25
LN_EPS = 1e-5
RMS_EPS = 1e-6
NEG_BIG = -1e30
LOG2E = 1.4426950408889634
ATTN_SCALE_LOG2 = HEAD_DIM ** -0.5 * LOG2E

V7X_VMEM_LIMIT_BYTES = 56 * 1024 * 1024

AB_IN_TILE = (1024, 1024)
CD_IN_TILE = (1024, 1280)
MIX_OUT_ROWS = 512
MIX_OUT_PARTS = 4
FFN_UP_TILE = (1024, 512)
FFN_DOWN_ROWS = 256
QKV_PREP_ROWS = 512
GQA_QUERY_ROWS = 1024
NA_BATCH_PER_STEP = 1

F32 = jnp.float32
BF16 = jnp.bfloat16


def _params(semantics):
    return pltpu.CompilerParams(dimension_semantics=semantics,
                                vmem_limit_bytes=V7X_VMEM_LIMIT_BYTES)


def _sigmoid(x):
    return 1.0 / (1.0 + jnp.exp(-x))


def _silu(x):
    return x * _sigmoid(x)


def _dot(a, b):
    return jnp.dot(a, b, preferred_element_type=F32)


def _dot_nt(a, b):
    return lax.dot_general(a, b, (((1,), (1,)), ((), ())), preferred_element_type=F32)


def _dot_tn(a, b):
    return lax.dot_general(a, b, (((0,), (0,)), ((), ())), preferred_element_type=F32)


def _matmul_kernel(a_ref, w_ref, o_ref, wb_ref):
    @pl.when(pl.program_id(1) == 0)
    def _():
        wb_ref[...] = w_ref[...].astype(BF16)

    a = a_ref[...]
    if a.dtype != BF16:
        a = a.astype(BF16)
    o_ref[...] = _dot(a, wb_ref[...]).astype(o_ref.dtype)


def _matmul(a, w, layer, out_dtype, tm, tn, name):
    m, k = a.shape
    n = w.shape[2]
    return pl.pallas_call(
        _matmul_kernel,
        grid=(n // tn, m // tm),
        in_specs=[pl.BlockSpec((tm, k), lambda j, i: (i, 0)),
                  pl.BlockSpec((None, k, tn), lambda j, i: (layer, 0, j))],
        out_specs=pl.BlockSpec((tm, tn), lambda j, i: (i, j)),
        out_shape=jax.ShapeDtypeStruct((m, n), out_dtype),
        scratch_shapes=[pltpu.VMEM((k, tn), BF16)],
        compiler_params=_params(("arbitrary", "arbitrary")),
        name=name,
    )(a, w)


def _res_ln(acc, x, g, b):
    z = DN_ALPHA * x + acc
    mu = jnp.mean(z, axis=-1, keepdims=True)
    zc = z - mu
    var = jnp.mean(zc * zc, axis=-1, keepdims=True)
    return zc * lax.rsqrt(var + LN_EPS) * g + b


def _row_parts(tm, parts):
    size = tm // parts
    return [slice(p * size, (p + 1) * size) for p in range(parts)]


def _mix_out_kernel(y1_ref, y2_ref, w_ref, x_ref, g_ref, b_ref, of_ref, ob_ref, *, parts):
    half = y1_ref.shape[1]
    rows = _row_parts(y1_ref.shape[0], parts)
    accs = [_dot(y1_ref[r, :], w_ref[:half, :]) + _dot(y2_ref[r, :], w_ref[half:, :]) for r in rows]
    for r, acc in zip(rows, accs):
        out = _res_ln(acc, x_ref[r, :], g_ref[...], b_ref[...])
        of_ref[r, :] = out
        ob_ref[r, :] = out.astype(BF16)


def _mix_out(y1, y2, w, layer, x, g, b, tm):
    m, half = y1.shape
    d = w.shape[2]
    row = lambda i: (i, 0)
    fixed = lambda i: (0, 0)
    return pl.pallas_call(
        functools.partial(_mix_out_kernel, parts=MIX_OUT_PARTS),
        grid=(m // tm,),
        in_specs=[pl.BlockSpec((tm, half), row), pl.BlockSpec((tm, half), row),
                  pl.BlockSpec((None, 2 * half, d), lambda i: (layer, 0, 0),
                               pipeline_mode=pl.Buffered(1)),
                  pl.BlockSpec((tm, d), row),
                  pl.BlockSpec((1, d), fixed), pl.BlockSpec((1, d), fixed)],
        out_specs=[pl.BlockSpec((tm, d), row), pl.BlockSpec((tm, d), row)],
        out_shape=[jax.ShapeDtypeStruct((m, d), F32), jax.ShapeDtypeStruct((m, d), BF16)],
        compiler_params=_params(("parallel",)),
        name="mix_out_ln",
    )(y1, y2, w, x, g.reshape(1, d), b.reshape(1, d))


def _ffn_up_kernel(x_ref, wg_ref, wu_ref, h_ref, wgb_ref, wub_ref):
    @pl.when(pl.program_id(1) == 0)
    def _():
        wgb_ref[...] = wg_ref[...].astype(BF16)
        wub_ref[...] = wu_ref[...].astype(BF16)

    x = x_ref[...]
    gate = _dot(x, wgb_ref[...])
    up = _dot(x, wub_ref[...])
    half_gate = 0.5 * gate
    h_ref[...] = ((half_gate + half_gate * jnp.tanh(half_gate)) * up).astype(h_ref.dtype)


def _ffn_up(xb, wg, wu, layer, tm, tn):
    m, k = xb.shape
    n = wg.shape[2]
    return pl.pallas_call(
        _ffn_up_kernel,
        grid=(n // tn, m // tm),
        in_specs=[pl.BlockSpec((tm, k), lambda j, i: (i, 0)),
                  pl.BlockSpec((None, k, tn), lambda j, i: (layer, 0, j)),
                  pl.BlockSpec((None, k, tn), lambda j, i: (layer, 0, j))],
        out_specs=pl.BlockSpec((tm, tn), lambda j, i: (i, j)),
        out_shape=jax.ShapeDtypeStruct((m, n), BF16),
        scratch_shapes=[pltpu.VMEM((k, tn), BF16), pltpu.VMEM((k, tn), BF16)],
        compiler_params=_params(("arbitrary", "arbitrary")),
        name="ffn_up",
    )(xb, wg, wu)


def _ffn_down_kernel(h_ref, w_ref, x_ref, g_ref, b_ref, of_ref, ob_ref):
    out = _res_ln(_dot(h_ref[...], w_ref[...]), x_ref[...], g_ref[...], b_ref[...])
    of_ref[...] = out
    ob_ref[...] = out.astype(BF16)


def _ffn_down(h, w, layer, x, g, b, tm):
    m, kk = h.shape
    d = w.shape[2]
    row = lambda i: (i, 0)
    fixed = lambda i: (0, 0)
    return pl.pallas_call(
        _ffn_down_kernel,
        grid=(m // tm,),
        in_specs=[pl.BlockSpec((tm, kk), row),
                  pl.BlockSpec((None, kk, d), lambda i: (layer, 0, 0), pipeline_mode=pl.Buffered(1)),
                  pl.BlockSpec((tm, d), row),
                  pl.BlockSpec((1, d), fixed), pl.BlockSpec((1, d), fixed)],
        out_specs=[pl.BlockSpec((tm, d), row), pl.BlockSpec((tm, d), row)],
        out_shape=[jax.ShapeDtypeStruct((m, d), F32), jax.ShapeDtypeStruct((m, d), BF16)],
        compiler_params=_params(("parallel",)),
        name="ffn_down_ln",
    )(h, w, x, g.reshape(1, d), b.reshape(1, d))


NA_GROUP = 4
NA_WINDOW = NA_GROUP + NA_WIN_ROWS
RPB_ROWS = 2 * NA_WIN_ROWS - 1
RPB_COLS = 2 * NA_WIN_COLS - 1


def _na_window_start(g, rows):
    return min(max(g * NA_GROUP - NA_WIN_ROWS // 2, 0), rows - NA_WINDOW)


def _na_patterns(rows):
    pats, ids = [], []
    for r in range(rows):
        w0 = _na_window_start(r // NA_GROUP, rows)
        r0 = min(max(r - NA_WIN_ROWS // 2, 0), rows - NA_WIN_ROWS)
        key = (r0 - w0, w0 - r + NA_WIN_ROWS - 1)
        if key not in pats:
            pats.append(key)
        ids.append(pats.index(key))
    return tuple(pats), tuple(ids)


def _na_band_kernel(rpb_ref, o_ref, *, patterns):
    h = pl.program_id(0)
    base = h * (RPB_ROWS * RPB_COLS)
    shape = (GRID_W, 2 * GRID_W)
    q = lax.broadcasted_iota(jnp.int32, shape, 0)
    lane = lax.broadcasted_iota(jnp.int32, shape, 1)
    kc = lane & (GRID_W - 1)
    second = lane >> 6
    code = kc - q + (NA_WIN_COLS - 1) + RPB_COLS * second
    start = jnp.clip(q - NA_WIN_COLS // 2, 0, GRID_W - NA_WIN_COLS)
    valid = (kc >= start) & (kc < start + NA_WIN_COLS)
    neg = jnp.full(shape, NEG_BIG, F32)
    pairs = {}

    def pair(ri):
        if ri not in pairs:
            acc = neg
            for i in range(2 * RPB_COLS):
                if 0 <= ri + i // RPB_COLS < RPB_ROWS:
                    acc = jnp.where(code == i, rpb_ref[base + ri * RPB_COLS + i], acc)
            pairs[ri] = jnp.where(valid, acc * LOG2E, NEG_BIG)
        return pairs[ri]

    for p, (lo, c) in enumerate(patterns):
        for u in range(NA_WINDOW // 2):
            in0 = lo <= 2 * u < lo + NA_WIN_ROWS
            in1 = lo <= 2 * u + 1 < lo + NA_WIN_ROWS
            if in0 and in1:
                tile = pair(2 * u + c)
            elif in0:
                tile = jnp.where(second == 0, pair(2 * u + c), NEG_BIG)
            elif in1:
                tile = jnp.where(second == 1, pair(2 * u + c), NEG_BIG)
            else:
                tile = neg
            o_ref[0, p, :, u * 2 * GRID_W:(u + 1) * 2 * GRID_W] = tile


def _na_band(rpb, patterns):
    shape = (NA_HEADS, len(patterns), GRID_W, NA_WINDOW * GRID_W)
    return pl.pallas_call(
        functools.partial(_na_band_kernel, patterns=patterns),
        grid=(NA_HEADS,),
        in_specs=[pl.BlockSpec(memory_space=pltpu.SMEM)],
        out_specs=pl.BlockSpec((1,) + shape[1:], lambda h: (h, 0, 0, 0)),
        out_shape=jax.ShapeDtypeStruct(shape, F32),
        compiler_params=_params(("arbitrary",)),
        name="na_band",
    )(rpb.reshape(-1))


def _na_kernel(q_ref, k_ref, v_ref, band_ref, o_ref, kt_ref, vb_ref, *, pattern_ids):
    t = q_ref.shape[1]
    rows = t // GRID_W
    gq = NA_GROUP * GRID_W
    wlen = NA_WINDOW * GRID_W
    n_groups = rows // NA_GROUP
    scale = ATTN_SCALE_LOG2
    for bi in range(q_ref.shape[0]):
        for c in range(0, t, gq):
            kt_ref[:, c:c + gq] = k_ref[bi, c:c + gq, :].T.astype(BF16)
        vb_ref[:, :HEAD_DIM] = v_ref[bi].astype(BF16)
        vb_ref[:, HEAD_DIM:] = jnp.ones((t, HEAD_DIM), BF16)

        def scores(g):
            w0 = _na_window_start(g, rows) * GRID_W
            q = (q_ref[bi, g * gq:(g + 1) * gq, :] * scale).astype(BF16)
            bias = jnp.concatenate(
                [band_ref[0, pattern_ids[g * NA_GROUP + i]] for i in range(NA_GROUP)], axis=0)
            return _dot(q, kt_ref[:, w0:w0 + wlen]) + bias

        s_next = scores(0)
        for g in range(n_groups):
            s = s_next
            if g + 1 < n_groups:
                s_next = scores(g + 1)
            w0 = _na_window_start(g, rows) * GRID_W
            m = jnp.max(s, axis=-1, keepdims=True)
            p = jnp.exp2(s - m).astype(BF16)
            ol = _dot(p, vb_ref[w0:w0 + wlen, :])
            o = ol[:, :HEAD_DIM] / ol[:, HEAD_DIM:]
            o_ref[bi, g * gq:(g + 1) * gq, :] = o.astype(o_ref.dtype)


def _na_attention(proj, rpb):
    b, t, _ = proj.shape
    nh = NA_HEADS
    nb = NA_BATCH_PER_STEP
    patterns, pattern_ids = _na_patterns(t // GRID_W)
    band = _na_band(rpb, patterns)
    blk = (nb, t, HEAD_DIM)
    return pl.pallas_call(
        functools.partial(_na_kernel, pattern_ids=pattern_ids),
        grid=(nh, b // nb),
        in_specs=[pl.BlockSpec(blk, lambda h, i: (i, 0, h)),
                  pl.BlockSpec(blk, lambda h, i: (i, 0, nh + h)),
                  pl.BlockSpec(blk, lambda h, i: (i, 0, 2 * nh + h)),
                  pl.BlockSpec((1,) + band.shape[1:], lambda h, i: (h, 0, 0, 0))],
        out_specs=pl.BlockSpec(blk, lambda h, i: (i, 0, h)),
        out_shape=jax.ShapeDtypeStruct((b, t, HALF), BF16),
        scratch_shapes=[pltpu.VMEM((HEAD_DIM, t), BF16), pltpu.VMEM((t, 2 * HEAD_DIM), BF16)],
        compiler_params=_params(("parallel", "parallel")),
        name="na_attention",
    )(proj, proj, proj, band)


HG_LEVELS = tuple(HG_CHUNK >> (i + 1) for i in range(HG_CHUNK.bit_length() - 1))
HG_DIAG = len(HG_LEVELS)


def _split2(x):
    hi = x.astype(BF16)
    mid = (x - hi.astype(F32)).astype(BF16)
    return hi, mid


def _level_codes(c, rev):
    t = lax.broadcasted_iota(jnp.int32, (c, c), 0)
    s = lax.broadcasted_iota(jnp.int32, (c, c), 1)
    x = t ^ s
    code = jnp.full((c, c), HG_DIAG, jnp.int32)
    for li in reversed(range(len(HG_LEVELS))):
        code = jnp.where(x >= HG_LEVELS[li], li, code)
    return jnp.where((s >= t) if rev else (s <= t), code, -1)


def _pivot_rows(gc, m, rev):
    c = gc.shape[0]
    blk = 2 * m
    if blk >= 8:
        piv = m if rev else m - 1
        parts = []
        for s in range(0, c, blk):
            parts.append(jnp.broadcast_to(gc[s + piv:s + piv + 1, :], (blk, gc.shape[1])))
        return parts[0] if len(parts) == 1 else jnp.concatenate(parts, axis=0)
    pos = lax.broadcasted_iota(jnp.int32, gc.shape, 0) & (blk - 1)
    piv = m if rev else m - 1
    out = gc
    for p in range(blk):
        if p == piv:
            continue
        shift = (p - piv) % c
        out = jnp.where(pos == p, pltpu.roll(gc, shift, 0), out)
    return out


def _neg_abs(x):
    bits = lax.bitcast_convert_type(x, jnp.uint32) | jnp.uint32(0x80000000)
    return lax.bitcast_convert_type(bits, F32)


def _hgrn_prepare(zs, qs, lbs, dirs, tri_ref, codes_ref):
    c = qs[0].shape[0]
    sub = 8
    units = range(len(zs))
    f = [lb + (1.0 - lb) * _sigmoid(z) for z, lb in zip(zs, lbs)]
    k = [1.0 - x for x in f]
    parts = [_split2(jnp.log2(x)) for x in f]
    gc = [_dot(tri_ref[int(dirs[d])], parts[d][0]) + _dot(tri_ref[int(dirs[d])], parts[d][1]) for d in units]
    pos = lax.broadcasted_iota(jnp.int32, qs[0].shape, 0)

    def tiles(x):
        return [x[i:i + sub] for i in range(0, c, sub)]

    def code_tiles(d):
        return [codes_ref[int(dirs[d]), i:i + sub, :] for i in range(0, c, sub)]

    a = []
    for d in units:
        diag = jnp.sum(qs[d] * k[d], axis=-1, keepdims=True)
        a.append([jnp.where(ct == HG_DIAG, dt, 0.0) for ct, dt in zip(code_tiles(d), tiles(diag))])

    def merge(d, li, prod):
        a[d] = [jnp.where(ct == li, pt, at) for ct, pt, at in zip(code_tiles(d), tiles(prod), a[d])]

    def small_level(m, decay):
        li = HG_LEVELS.index(m)
        for d, rev in enumerate(dirs):
            is_q = ((pos & m) == 0) if rev else ((pos & m) != 0)
            xl = (jnp.where(is_q, qs[d], k[d]) * decay[d]).astype(BF16)
            merge(d, li, _dot_nt(xl, xl))

    if 1 in HG_LEVELS:
        ones = jnp.ones_like(qs[0])
        dec1 = []
        for d, rev in enumerate(dirs):
            is_q = ((pos & 1) == 0) if rev else ((pos & 1) != 0)
            dec1.append(jnp.where(is_q, f[d], ones))
        small_level(1, dec1)
    if 2 in HG_LEVELS:
        p4 = pos & 3
        dec2 = []
        for d, rev in enumerate(dirs):
            nxt = pltpu.roll(f[d], c - 1, 0)
            prv = pltpu.roll(f[d], 1, 0)
            if rev:
                e = jnp.where(p4 == 0, f[d] * nxt, jnp.where(p4 == 1, f[d], jnp.where(p4 == 2, 1.0, prv)))
            else:
                e = jnp.where(p4 == 0, nxt, jnp.where(p4 == 1, 1.0, jnp.where(p4 == 2, f[d], f[d] * prv)))
            dec2.append(e)
        small_level(2, dec2)

    for li, m in enumerate(HG_LEVELS):
        if m <= 2:
            continue
        e = [jnp.exp2(_neg_abs(gc[d] - _pivot_rows(gc[d], m, rev))) for d, rev in enumerate(dirs)]
        if m < sub:
            small_level(m, e)
            continue
        for d, rev in enumerate(dirs):
            q_rows, rhs = [], []
            for s in range(0, c, 2 * m):
                lo, hi = slice(s, s + m), slice(s + m, s + 2 * m)
                q_sl, k_sl = (lo, hi) if rev else (hi, lo)
                q_rows.append(q_sl)
                kx = k[d][k_sl] * e[d][k_sl]
                rhs += [k[d][lo], kx] if rev else [kx, k[d][hi]]
            lhs = jnp.concatenate([qs[d][r] * e[d][r] for r in q_rows], axis=0).astype(BF16)
            prod = tiles(_dot_nt(lhs, jnp.concatenate(rhs, axis=0).astype(BF16)))
            ct = code_tiles(d)
            j = 0
            for r in q_rows:
                for i in range(r.start // sub, r.stop // sub):
                    a[d][i] = jnp.where(ct[i] == li, prod[j], a[d][i])
                    j += 1

    out = []
    for d, rev in enumerate(dirs):
        g_tot = gc[d][0:1, :] if rev else gc[d][c - 1:c, :]
        qg = (qs[d] * jnp.exp2(gc[d])).astype(BF16)
        kd = (k[d] * jnp.exp2(g_tot - gc[d])).astype(BF16)
        out.append((jnp.concatenate(a[d], axis=0).astype(BF16), qg, kd, jnp.exp2(g_tot)))
    return out


HG_DEC_ROWS = 8
HG_PASS_ROWS = 256
HG_HEADS_PER_STEP = 2
HG_LOOP_UNROLL = 3


def _row_block(idx, size):
    start = idx * size
    return pl.ds(start if isinstance(start, int) else pl.multiple_of(start, size), size)


def _hgrn_kernel(q_ref, ff_ref, fb_ref, i_ref, g_ref, lbl_ref, nw_ref, o_ref,
                 tri_ref, codes_ref, a_ref, qg_ref, kd_ref, dec_ref, qa_ref, vb_ref, acc_ref, *, layer):
    t = q_ref.shape[1]
    c = HG_CHUNK
    n = t // c
    dk = dv = HEAD_DIM
    heads = q_ref.shape[2] // HEAD_DIM
    units = [(hh, rev) for hh in range(heads) for rev in (False, True)]
    dirs = [rev for _, rev in units]

    def cols(hh):
        return slice(hh * HEAD_DIM, (hh + 1) * HEAD_DIM)

    lg = lbl_ref[...]
    e = jnp.exp(lg - jnp.max(lg, axis=0, keepdims=True))
    p = e / jnp.sum(e, axis=0, keepdims=True)
    lb = jnp.sum(p[:layer + 1, :], axis=0, keepdims=True) - p[0:1, :]
    lbs = [lb[:, cols(hh)] for hh, _ in units]

    row = lax.broadcasted_iota(jnp.int32, (c, c), 0)
    col = lax.broadcasted_iota(jnp.int32, (c, c), 1)
    tri_ref[0] = jnp.where(col <= row, 1.0, 0.0).astype(BF16)
    tri_ref[1] = jnp.where(col >= row, 1.0, 0.0).astype(BF16)
    codes_ref[0] = _level_codes(c, False)
    codes_ref[1] = _level_codes(c, True)

    act_rows = HG_PASS_ROWS

    def activate(i, carry):
        rows = _row_block(i, act_rows)
        qa_ref[rows, :] = _silu(q_ref[0, rows, :])
        vb_ref[rows, :] = i_ref[0, rows, :].astype(BF16)
        return carry

    lax.fori_loop(0, t // act_rows, activate, 0)

    def chunk_rows(i):
        return [_row_block(i, c), _row_block(n - 1 - i, c)]

    def dec_rows(i, size):
        return [pl.ds(r.start, size) for r in (_row_block(i, HG_DEC_ROWS), _row_block(n - 1 - i, HG_DEC_ROWS))]

    def prepare(i):
        rows = chunk_rows(i)
        zs = [(fb_ref if rev else ff_ref)[0, rows[rev], cols(hh)] for hh, rev in units]
        qs = [qa_ref[rows[rev], cols(hh)] for hh, rev in units]
        staged = _hgrn_prepare(zs, qs, lbs, dirs, tri_ref, codes_ref)
        for u, ((_, rev), (a, qg, kd, dec)) in enumerate(zip(units, staged)):
            a_ref[u, rows[rev], :] = a
            qg_ref[u, rows[rev], :] = qg
            kd_ref[u, rows[rev], :] = kd
            dec_ref[u, dec_rows(i, HG_DEC_ROWS)[rev], :] = jnp.broadcast_to(dec, (HG_DEC_ROWS, dk))

    def recur(i, states):
        rows = chunk_rows(i)
        new = []
        for u, (hh, rev) in enumerate(units):
            r = rows[rev]
            vb = vb_ref[r, cols(hh)]
            acc_ref[u, r, :] = _dot(a_ref[u, r, :], vb) + _dot_nt(qg_ref[u, r, :], states[u].astype(BF16))
            dec = dec_ref[u, dec_rows(i, 1)[rev], :]
            new.append(states[u] * dec + _dot_tn(vb, kd_ref[u, r, :]))
        return tuple(new)

    def body(i, states):
        states = recur(i, states)
        prepare(i + 1)
        return states

    zero = jnp.zeros((dv, dk), F32)
    prepare(0)
    recur(n - 1, lax.fori_loop(0, n - 1, body, (zero,) * len(units), unroll=HG_LOOP_UNROLL))

    rows = HG_PASS_ROWS

    def finish(i, carry):
        r = _row_block(i, rows)
        for hh in range(heads):
            o = acc_ref[2 * hh, r, :] + acc_ref[2 * hh + 1, r, :]
            ms = jnp.mean(o * o, axis=-1, keepdims=True)
            y = o * lax.rsqrt(ms + RMS_EPS) * nw_ref[:, cols(hh)]
            o_ref[0, r, cols(hh)] = (y * _silu(g_ref[0, r, cols(hh)])).astype(o_ref.dtype)
        return carry

    lax.fori_loop(0, t // rows, finish, 0, unroll=4)


def _hgrn(proj, lb_logits, norm_w, layer):
    b, t, _ = proj.shape
    nh = HG_HEADS
    g = HG_HEADS_PER_STEP
    n_layers = lb_logits.shape[0]
    width = g * HEAD_DIM
    blk = (1, t, width)
    base = 3 * HALF // width
    units = 2 * g

    def col(k):
        return lambda i, h: (i, 0, base + k * (nh // g) + h)

    return pl.pallas_call(
        functools.partial(_hgrn_kernel, layer=layer),
        grid=(b, nh // g),
        in_specs=[pl.BlockSpec(blk, col(0)), pl.BlockSpec(blk, col(1)), pl.BlockSpec(blk, col(2)),
                  pl.BlockSpec(blk, col(3)), pl.BlockSpec(blk, col(4)),
                  pl.BlockSpec((n_layers, width), lambda i, h: (0, h)),
                  pl.BlockSpec((1, width), lambda i, h: (0, h))],
        out_specs=pl.BlockSpec(blk, lambda i, h: (i, 0, h)),
        out_shape=jax.ShapeDtypeStruct((b, t, HALF), BF16),
        scratch_shapes=[pltpu.VMEM((2, HG_CHUNK, HG_CHUNK), BF16),
                        pltpu.VMEM((2, HG_CHUNK, HG_CHUNK), jnp.int32),
                        pltpu.VMEM((units, t, HG_CHUNK), BF16),
                        pltpu.VMEM((units, t, HEAD_DIM), BF16),
                        pltpu.VMEM((units, t, HEAD_DIM), BF16),
                        pltpu.VMEM((units, t // HG_CHUNK * HG_DEC_ROWS, HEAD_DIM), F32),
                        pltpu.VMEM((t, width), F32),
                        pltpu.VMEM((t, width), BF16),
                        pltpu.VMEM((units, t, HEAD_DIM), F32)],
        compiler_params=_params(("parallel", "parallel")),
        name="hgrn2",
    )(proj, proj, proj, proj, proj, lb_logits, norm_w.reshape(1, HALF))


POOL_PAD = 16
POOL_ROWS = 256


def _pool_kernel(x_ref, w_ref, s_ref, o_ref, xp_ref, sa_ref, sb_ref):
    t = x_ref.shape[1]
    width = x_ref.shape[2]
    gd = POOL_GROUP_DIM
    half = POOL_PAD // 2
    lo, hi = half, t + POOL_PAD + half
    xp_ref[0:POOL_PAD, :] = jnp.zeros((POOL_PAD, width), F32)
    xp_ref[POOL_PAD + t:, :] = jnp.zeros((POOL_PAD, width), F32)
    xp_ref[POOL_PAD:POOL_PAD + t, :] = x_ref[0]
    for ref in (sa_ref, sb_ref):
        ref[0:lo, :] = jnp.zeros((lo, gd), F32)
        ref[hi:, :] = jnp.zeros((t + 2 * POOL_PAD - hi, gd), F32)

    for gi, win in enumerate(POOL_WINDOWS):
        c0, c1 = gi * gd, (gi + 1) * gd
        sa_ref[lo:hi, :] = xp_ref[lo - 1:hi - 1, c0:c1] + xp_ref[lo:hi, c0:c1]
        src, dst, span = sa_ref, sb_ref, 2
        while span < win:
            sh = span // 2
            dst[lo:hi, :] = src[lo - sh:hi - sh, :] + src[lo + sh:hi + sh, :]
            src, dst, span = dst, src, 2 * span
        hw = win // 2
        for t0 in range(0, t, POOL_ROWS):
            pos = t0 + lax.broadcasted_iota(jnp.int32, (POOL_ROWS, gd), 0)
            cnt = (jnp.minimum(pos + hw, t) - jnp.maximum(pos - hw, 0)).astype(F32)
            tot = src[POOL_PAD + t0:POOL_PAD + t0 + POOL_ROWS, :]
            pooled = tot / cnt - x_ref[0, t0:t0 + POOL_ROWS, c0:c1]
            y = _dot(pooled.astype(BF16), w_ref[gi]) * s_ref[:, c0:c1]
            o_ref[0, t0:t0 + POOL_ROWS, c0:c1] = y.astype(o_ref.dtype)


def _pool(proj, pool_w, scale):
    b, t, _ = proj.shape
    ng = len(POOL_WINDOWS)
    return pl.pallas_call(
        _pool_kernel,
        grid=(b,),
        in_specs=[pl.BlockSpec((1, t, HALF), lambda i: (i, 0, 0)),
                  pl.BlockSpec((ng, POOL_GROUP_DIM, POOL_GROUP_DIM), lambda i: (0, 0, 0)),
                  pl.BlockSpec((1, HALF), lambda i: (0, 0))],
        out_specs=pl.BlockSpec((1, t, HALF), lambda i: (i, 0, 0)),
        out_shape=jax.ShapeDtypeStruct((b, t, HALF), BF16),
        scratch_shapes=[pltpu.VMEM((t + 2 * POOL_PAD, HALF), F32),
                        pltpu.VMEM((t + 2 * POOL_PAD, POOL_GROUP_DIM), F32),
                        pltpu.VMEM((t + 2 * POOL_PAD, POOL_GROUP_DIM), F32)],
        compiler_params=_params(("parallel",)),
        name="pool",
    )(proj, pool_w, scale.reshape(1, HALF))


def _rope_tables(t):
    pos = jnp.arange(t)
    row = (pos // GRID_W).astype(F32)
    col = (pos % GRID_W).astype(F32)
    n_freq = HEAD_DIM // 4
    inv = ROPE_THETA ** (-jnp.arange(n_freq, dtype=F32) / n_freq)
    ang = jnp.concatenate([row[:, None] * inv, col[:, None] * inv], axis=-1)
    cos = jnp.repeat(jnp.cos(ang), 2, axis=-1)
    sin = jnp.repeat(jnp.sin(ang), 2, axis=-1)
    sign = jnp.where(jnp.arange(HEAD_DIM) % 2 == 0, -1.0, 1.0).astype(F32)
    return cos, sin * sign


def _norm_rope(xs, g, cos, sin_signed, scale):
    ms = [jnp.mean(x * x, axis=-1, keepdims=True) for x in xs]
    xn = [x * lax.rsqrt(m + RMS_EPS) * g for x, m in zip(xs, ms)]
    lane = lax.broadcasted_iota(jnp.int32, xs[0].shape, 1)
    even = (lane & 1) == 0
    partner = [jnp.where(even, pltpu.roll(x, HEAD_DIM - 1, 1), pltpu.roll(x, 1, 1)) for x in xn]
    return [(x * cos + p * sin_signed) * scale for x, p in zip(xn, partner)]


QKV_PREP_HEADS = 4


def _qkv_prep_kernel(q_ref, k_ref, v_ref, qn_ref, kn_ref, cos_ref, sin_ref, qo_ref, kt_ref, vo_ref):
    cos = cos_ref[...]
    sin = sin_ref[...]

    def head_cols(h):
        return slice(h * HEAD_DIM, (h + 1) * HEAD_DIM)

    for h0 in range(0, GQA_Q_HEADS, QKV_PREP_HEADS):
        hs = range(h0, h0 + QKV_PREP_HEADS)
        outs = _norm_rope([q_ref[:, head_cols(h)] for h in hs], qn_ref[...], cos, sin, ATTN_SCALE_LOG2)
        for h, o in zip(hs, outs):
            qo_ref[:, head_cols(h)] = o.astype(qo_ref.dtype)
    hs = range(GQA_KV_HEADS)
    outs = _norm_rope([k_ref[:, head_cols(h)] for h in hs], kn_ref[...], cos, sin, 1.0)
    for h, o in zip(hs, outs):
        kt_ref[0, head_cols(h), :] = o.T.astype(kt_ref.dtype)
    ones = jnp.ones((v_ref.shape[0], HEAD_DIM), vo_ref.dtype)
    for h in range(GQA_KV_HEADS):
        vo_ref[:, 2 * h * HEAD_DIM:(2 * h + 1) * HEAD_DIM] = (
            v_ref[:, h * HEAD_DIM:(h + 1) * HEAD_DIM].astype(vo_ref.dtype))
        vo_ref[:, (2 * h + 1) * HEAD_DIM:(2 * h + 2) * HEAD_DIM] = ones


def _qkv_prep(proj2d, q_norm, k_norm, cos, sin, b, t, tm):
    m = proj2d.shape[0]
    tb = t // tm
    kv_blk = (2 * HALF) // KV_W
    return pl.pallas_call(
        _qkv_prep_kernel,
        grid=(m // tm,),
        in_specs=[pl.BlockSpec((tm, HALF), lambda i: (i, 1)),
                  pl.BlockSpec((tm, KV_W), lambda i: (i, kv_blk)),
                  pl.BlockSpec((tm, KV_W), lambda i: (i, kv_blk + 1)),
                  pl.BlockSpec((1, HEAD_DIM), lambda i: (0, 0)),
                  pl.BlockSpec((1, HEAD_DIM), lambda i: (0, 0)),
                  pl.BlockSpec((tm, HEAD_DIM), lambda i: (i % tb, 0)),
                  pl.BlockSpec((tm, HEAD_DIM), lambda i: (i % tb, 0))],
        out_specs=[pl.BlockSpec((tm, HALF), lambda i: (i, 0)),
                   pl.BlockSpec((1, KV_W, tm), lambda i: (i // tb, 0, i % tb)),
                   pl.BlockSpec((tm, 2 * KV_W), lambda i: (i, 0))],
        out_shape=[jax.ShapeDtypeStruct((m, HALF), BF16),
                   jax.ShapeDtypeStruct((b, KV_W, t), BF16),
                   jax.ShapeDtypeStruct((m, 2 * KV_W), BF16)],
        compiler_params=_params(("parallel",)),
        name="qkv_prep",
    )(proj2d, proj2d, proj2d, q_norm.reshape(1, HEAD_DIM), k_norm.reshape(1, HEAD_DIM), cos, sin)


GQA_GROUP = GQA_Q_HEADS // GQA_KV_HEADS


def _gqa_kernel(q_ref, kt_ref, v_ref, o_ref):
    def scores(h):
        return _dot(q_ref[0, :, h * HEAD_DIM:(h + 1) * HEAD_DIM], kt_ref[0])

    s_next = scores(0)
    for h in range(GQA_GROUP):
        s = s_next
        if h + 1 < GQA_GROUP:
            s_next = scores(h + 1)
        m = jnp.max(s, axis=-1, keepdims=True)
        p = jnp.exp2(s - m).astype(BF16)
        ol = _dot(p, v_ref[0])
        o = ol[:, :HEAD_DIM] / ol[:, HEAD_DIM:]
        o_ref[0, :, h * HEAD_DIM:(h + 1) * HEAD_DIM] = o.astype(o_ref.dtype)


def _gqa(q, kt, v, tq):
    b, t, _ = q.shape
    gw = GQA_GROUP * HEAD_DIM
    return pl.pallas_call(
        _gqa_kernel,
        grid=(b, GQA_KV_HEADS, t // tq),
        in_specs=[pl.BlockSpec((1, tq, gw), lambda i, h, j: (i, j, h)),
                  pl.BlockSpec((1, HEAD_DIM, t), lambda i, h, j: (i, h, 0)),
                  pl.BlockSpec((1, t, 2 * HEAD_DIM), lambda i, h, j: (i, 0, h))],
        out_specs=pl.BlockSpec((1, tq, gw), lambda i, h, j: (i, j, h)),
        out_shape=jax.ShapeDtypeStruct((b, t, HALF), BF16),
        compiler_params=_params(("parallel", "parallel", "parallel")),
        name="gqa",
    )(q, kt, v)


def kernel(x, ab_w_in, ab_w_out, na_rpb, hg_lb_logits, hg_norm_w, cd_w_in, cd_w_out, pool_w,
           pool_scale, d_q_norm, d_k_norm, ln_mix_g, ln_mix_b, ffn_w_gate, ffn_w_up, ffn_w_down,
           ln_ffn_g, ln_ffn_b):
    b, t, d = x.shape
    m = b * t
    cos, sin = _rope_tables(t)
    xf = x.reshape(m, d)
    xb = xf
    ab_w_out_b = ab_w_out.astype(BF16)
    cd_w_out_b = cd_w_out.astype(BF16)
    pool_w_b = pool_w.astype(BF16)
    ffn_w_down_b = ffn_w_down.astype(BF16)
    for layer in range(DEPTH):
        j = layer // 2
        if layer % 2 == 0:
            proj = _matmul(xb, ab_w_in, j, F32, *AB_IN_TILE, "ab_in_proj").reshape(b, t, -1)
            y1 = _na_attention(proj, na_rpb[j])
            y2 = _hgrn(proj, hg_lb_logits, hg_norm_w[j], j)
            w_out = ab_w_out_b
        else:
            proj = _matmul(xb, cd_w_in, j, F32, *CD_IN_TILE, "cd_in_proj")
            y1 = _pool(proj.reshape(b, t, -1), pool_w_b[j], pool_scale[j])
            qn, kt, vn = _qkv_prep(proj, d_q_norm[j], d_k_norm[j], cos, sin, b, t, QKV_PREP_ROWS)
            y2 = _gqa(qn.reshape(b, t, HALF), kt, vn.reshape(b, t, 2 * KV_W), GQA_QUERY_ROWS)
            w_out = cd_w_out_b
        xf, xb = _mix_out(y1.reshape(m, HALF), y2.reshape(m, HALF), w_out, j, xf,
                          ln_mix_g[layer], ln_mix_b[layer], MIX_OUT_ROWS)
        h = _ffn_up(xb, ffn_w_gate, ffn_w_up, layer, *FFN_UP_TILE)
        xf, xb = _ffn_down(h, ffn_w_down_b, layer, xf, ln_ffn_g[layer], ln_ffn_b[layer],
                           FFN_DOWN_ROWS)
    return xf.reshape(b, t, d)
```
